```python
import math
import jax, jax.numpy as jnp
from jax import lax
import numpy as np

D_MODEL = 1024
BATCH = 32
SEQ = 256
DEPTH = 2
DEC_BATCH = 2
DEC_SEQ = 4096
PAST_LEN = 256

GRID_W = 64
N_MIXERS = 4
GROUP_W = D_MODEL // N_MIXERS
MIX_W = N_MIXERS * GROUP_W
EPS = 1e-6

S5_W = GROUP_W
S5_GSIZE = 16
S5_G = S5_W // S5_GSIZE
S5_P = 64

MLA_H = 4
MLA_NOPE = 64
MLA_ROPE = 32
MLA_V = GROUP_W // MLA_H
MLA_QLORA = D_MODEL // 4
MLA_KVLORA = D_MODEL // 8
MLA_QBLK = 128
ROPE_BASE = 10000.0

SSD_W = GROUP_W
SSD_P = 64
SSD_H = SSD_W // SSD_P
SSD_G = 2
SSD_N = 64
SSD_CONV = 3
SSD_CHUNK = 128
SSD_XBC = SSD_W + 2 * SSD_G * SSD_N

CONF_W = GROUP_W
CONF_K = 31

MOE_GROUPS = 4
MOE_PER_GROUP = 4
MOE_E = MOE_GROUPS * MOE_PER_GROUP
MOE_TOPK = 2
MOE_HID = D_MODEL // 4

OFF_S5 = 0
OFF_MLA_Q = OFF_S5 + S5_W
OFF_MLA_KV = OFF_MLA_Q + MLA_QLORA
OFF_MLA_KR = OFF_MLA_KV + MLA_KVLORA
OFF_SSD_Z = OFF_MLA_KR + MLA_ROPE
OFF_SSD_XBC = OFF_SSD_Z + SSD_W
OFF_SSD_DT = OFF_SSD_XBC + SSD_XBC
OFF_CONF = OFF_SSD_DT + SSD_H
IN_COLS = OFF_CONF + 2 * CONF_W

kernel_name = 'hybrid_parallel_ssm_mla_ssd_conformer_hmoe_step'

F32 = jnp.float32


def rmsnorm(x, g):
    xf = x.astype(F32)
    y = xf * lax.rsqrt(jnp.mean(xf * xf, axis=-1, keepdims=True) + EPS)
    return (y * g.astype(F32)).astype(x.dtype)


def layernorm(x, g, b):
    xf = x.astype(F32)
    mu = jnp.mean(xf, axis=-1, keepdims=True)
    var = jnp.mean(jnp.square(xf - mu), axis=-1, keepdims=True)
    y = (xf - mu) * lax.rsqrt(var + EPS)
    return (y * g.astype(F32) + b.astype(F32)).astype(x.dtype)


def dwconv_centred(x, w, b):
    k = w.shape[0]
    pad = k // 2
    y = lax.conv_general_dilated(x, w[:, None, :].astype(x.dtype), (1,), [(pad, pad)],
                                 dimension_numbers=('NWC', 'WIO', 'NWC'),
                                 feature_group_count=x.shape[-1])
    return y + b.astype(x.dtype)


def axial_rope_tables(length):
    rows = length // GRID_W
    row = jnp.repeat(jnp.arange(rows), GRID_W).astype(F32)
    col = jnp.tile(jnp.arange(GRID_W), rows).astype(F32)
    half = MLA_ROPE // 2
    inv = ROPE_BASE ** (-jnp.arange(0, half, 2, dtype=F32) / half)
    ang = jnp.concatenate([row[:, None] * inv, col[:, None] * inv], axis=-1)
    return jnp.cos(ang), jnp.sin(ang)


def apply_axial_rope(x, cos, sin):
    shp = x.shape
    q = MLA_ROPE // 4
    xr = x.astype(F32).reshape(shp[:-1] + (2, 2, q))
    bshape = (1, shp[1]) + (1,) * (x.ndim - 3) + (2, q)
    c = cos.reshape(bshape)
    s = sin.reshape(bshape)
    x1, x2 = xr[..., 0, :], xr[..., 1, :]
    out = jnp.stack([x1 * c - x2 * s, x1 * s + x2 * c], axis=-2)
    return out.reshape(shp).astype(x.dtype)


def s5_combine(e1, e2):
    a1r, a1i, b1r, b1i = e1
    a2r, a2i, b2r, b2i = e2
    ar = a2r * a1r - a2i * a1i
    ai = a2r * a1i + a2i * a1r
    br = a2r * b1r - a2i * b1i + b2r
    bi = a2r * b1i + a2i * b1r + b2i
    return ar, ai, br, bi


def s5_direction(u, lam_re, lam_im, log_dt, b_re, b_im, c_re, c_im, h0):
    lam_re = lam_re.astype(F32)
    lam_im = lam_im.astype(F32)
    dt = jnp.exp(log_dt.astype(F32))[:, None]
    mag = jnp.exp(lam_re * dt)
    ang = lam_im * dt
    abr = mag * jnp.cos(ang)
    abi = mag * jnp.sin(ang)
    den = lam_re * lam_re + lam_im * lam_im
    fr = ((abr - 1.0) * lam_re + abi * lam_im) / den
    fi = (abi * lam_re - (abr - 1.0) * lam_im) / den
    b_re = b_re.astype(F32)
    b_im = b_im.astype(F32)
    bbr = fr[..., None] * b_re - fi[..., None] * b_im
    bbi = fr[..., None] * b_im + fi[..., None] * b_re
    bur = jnp.einsum('blgi,gpi->blgp', u, bbr)
    bui = jnp.einsum('blgi,gpi->blgp', u, bbi)
    h0r, h0i = h0[..., 0], h0[..., 1]
    bur = bur.at[:, 0].add(abr * h0r - abi * h0i)
    bui = bui.at[:, 0].add(abr * h0i + abi * h0r)
    ar = jnp.broadcast_to(abr, bur.shape)
    ai = jnp.broadcast_to(abi, bui.shape)
    _, _, hr, hi = lax.associative_scan(s5_combine, (ar, ai, bur, bui), axis=1)
    y = (jnp.einsum('blgp,gop->blgo', hr, c_re.astype(F32))
         - jnp.einsum('blgp,gop->blgo', hi, c_im.astype(F32)))
    return y, jnp.stack([hr[:, -1], hi[:, -1]], axis=-1)


def s5_mixer(u, P, h0):
    b, L, _ = u.shape
    uf = u.astype(F32)
    ug = uf.reshape(b, L, S5_G, S5_GSIZE)
    ys, sts = [], []
    for d in range(2):
        src = ug if d == 0 else ug[:, ::-1]
        y, st = s5_direction(src, P['s5_lam_re'][d], P['s5_lam_im'][d], P['s5_log_dt'][d],
                             P['s5_b_re'][d], P['s5_b_im'][d], P['s5_c_re'][d], P['s5_c_im'][d],
                             h0[:, d].astype(F32))
        ys.append(y if d == 0 else y[:, ::-1])
        sts.append(st)
    y = (ys[0] + ys[1]).reshape(b, L, S5_W) + P['s5_d'].astype(F32) * uf
    a = jax.nn.gelu(y)
    out = a * jax.nn.sigmoid(a @ P['s5_w_glu'].astype(F32) + P['s5_b_glu'].astype(F32))
    return out.astype(u.dtype), jnp.stack(sts, axis=1)


def blocked_attention(q, k, v):
    b, lq, h, dk = q.shape
    nb = lq // MLA_QBLK
    qb = jnp.moveaxis(q.reshape(b, nb, MLA_QBLK, h, dk), 1, 0)
    scale = 1.0 / math.sqrt(dk)

    def one_block(qblk):
        s = jnp.einsum('bqhd,bkhd->bhqk', qblk, k).astype(F32) * scale
        p = jax.nn.softmax(s, axis=-1).astype(v.dtype)
        return jnp.einsum('bhqk,bkhv->bqhv', p, v)

    out = lax.map(one_block, qb)
    return jnp.moveaxis(out, 0, 1).reshape(b, lq, h * v.shape[-1])


def mla_mixer(proj, P, ctx_kv):
    b, L, _ = proj.shape
    cq = rmsnorm(proj[..., OFF_MLA_Q:OFF_MLA_KV], P['mla_qnorm'])
    ckv = rmsnorm(proj[..., OFF_MLA_KV:OFF_MLA_KR], P['mla_kvnorm'])
    kr = proj[..., OFF_MLA_KR:OFF_SSD_Z]
    q = jnp.einsum('bld,dhk->blhk', cq, P['mla_wuq'].astype(cq.dtype))
    q_nope, q_rope = q[..., :MLA_NOPE], q[..., MLA_NOPE:]
    if ctx_kv is None:
        ckv_keys, kr_keys = ckv, kr
    else:
        cos, sin = axial_rope_tables(L)
        q_rope = apply_axial_rope(q_rope, cos, sin)
        kr = apply_axial_rope(kr, cos, sin)
        ctx_ckv, ctx_kr = ctx_kv
        ckv_keys = jnp.concatenate([ctx_ckv.astype(ckv.dtype), ckv], axis=1)
        kr_keys = jnp.concatenate([ctx_kr.astype(kr.dtype), kr], axis=1)
    k_nope = jnp.einsum('bld,dhk->blhk', ckv_keys, P['mla_wuk'].astype(ckv_keys.dtype))
    v = jnp.einsum('bld,dhk->blhk', ckv_keys, P['mla_wuv'].astype(ckv_keys.dtype))
    k_rope = jnp.broadcast_to(kr_keys[:, :, None, :], k_nope.shape[:-1] + (MLA_ROPE,))
    k = jnp.concatenate([k_nope, k_rope.astype(k_nope.dtype)], axis=-1)
    qf = jnp.concatenate([q_nope, q_rope.astype(q_nope.dtype)], axis=-1)
    out = blocked_attention(qf, k, v)
    return out, ckv, kr


def ssd_chunked(x, dt, a, bh, ch, h0):
    b, L, H, Pd = x.shape
    Q = SSD_CHUNK
    nc = L // Q
    xc = x.reshape(b, nc, Q, H, Pd)
    dtc = dt.reshape(b, nc, Q, H)
    bc = bh.reshape(b, nc, Q, H, SSD_N)
    cc = ch.reshape(b, nc, Q, H, SSD_N)
    a_cs = jnp.cumsum(dtc * a, axis=2)
    seg = a_cs[:, :, :, None, :] - a_cs[:, :, None, :, :]
    mask = jnp.tril(jnp.ones((Q, Q), dtype=bool))[None, None, :, :, None]
    lmat = jnp.exp(jnp.where(mask, seg, -jnp.inf))
    cb = jnp.einsum('bcihn,bcjhn->bcijh', cc, bc)
    y_diag = jnp.einsum('bcijh,bcjhp->bcihp', cb * lmat * dtc[:, :, None, :, :], xc)
    decay_end = jnp.exp(a_cs[:, :, -1:, :] - a_cs)
    chunk_states = jnp.einsum('bcjhn,bcjhp->bchpn', bc * (decay_end * dtc)[..., None], xc)
    chunk_decay = jnp.exp(a_cs[:, :, -1, :])

    def step(h, inp):
        dec, st = inp
        return dec[:, :, None, None] * h + st, h

    h_t, h_in = lax.scan(step, h0, (jnp.moveaxis(chunk_decay, 1, 0), jnp.moveaxis(chunk_states, 1, 0)))
    h_in = jnp.moveaxis(h_in, 0, 1)
    y_off = jnp.einsum('bcihn,bchpn->bcihp', cc, h_in) * jnp.exp(a_cs)[..., None]
    return (y_diag + y_off).reshape(b, L, H, Pd), h_t


def ssd_mixer(proj, P, h0):
    b, L, _ = proj.shape
    z = proj[..., OFF_SSD_Z:OFF_SSD_XBC].astype(F32)
    xbc = jax.nn.silu(dwconv_centred(proj[..., OFF_SSD_XBC:OFF_SSD_DT], P['ssd_conv_w'], P['ssd_conv_b'])).astype(F32)
    dt_raw = proj[..., OFF_SSD_DT:OFF_CONF].astype(F32)
    xs = xbc[..., :SSD_W].reshape(b, L, SSD_H, SSD_P)
    bm = xbc[..., SSD_W:SSD_W + SSD_G * SSD_N].reshape(b, L, SSD_G, SSD_N)
    cm = xbc[..., SSD_W + SSD_G * SSD_N:].reshape(b, L, SSD_G, SSD_N)
    rep = SSD_H // SSD_G
    bh = jnp.repeat(bm, rep, axis=2)
    ch = jnp.repeat(cm, rep, axis=2)
    ys, sts = [], []
    for d in range(2):
        dt = jax.nn.softplus(dt_raw + P['ssd_dt_bias'][d].astype(F32))
        a = -jnp.exp(P['ssd_a_log'][d].astype(F32))
        h0d = h0[:, d].astype(F32)
        if d == 0:
            y, st = ssd_chunked(xs, dt, a, bh, ch, h0d)
        else:
            y, st = ssd_chunked(xs[:, ::-1], dt[:, ::-1], a, bh[:, ::-1], ch[:, ::-1], h0d)
            y = y[:, ::-1]
        ys.append(y)
        sts.append(st)
    y = ys[0] + ys[1] + P['ssd_d'].astype(F32)[:, None] * xs
    y = rmsnorm(y.reshape(b, L, SSD_W) * jax.nn.silu(z), P['ssd_norm_w'])
    return y.astype(proj.dtype), jnp.stack(sts, axis=1)


def conformer_conv(inp, P):
    v = inp[..., :CONF_W] * jax.nn.sigmoid(inp[..., CONF_W:])
    v = dwconv_centred(v, P['conf_dw_w'], P['conf_dw_b'])
    v = layernorm(v, P['conf_ln_g'], P['conf_ln_b'])
    return jax.nn.silu(v)


def hier_moe(h, P):
    b, L, d = h.shape
    t = h.reshape(-1, d)
    gl = (t @ P['moe_wg'].astype(t.dtype) + P['moe_bg'].astype(t.dtype)).astype(F32)
    pg = jax.nn.softmax(gl, axis=-1)
    gsel = jnp.argmax(gl, axis=-1)
    el = (t @ P['moe_we'].astype(t.dtype) + P['moe_be'].astype(t.dtype)).astype(F32)
    el = el.reshape(-1, MOE_GROUPS, MOE_PER_GROUP)
    el_sel = jnp.take_along_axis(el, gsel[:, None, None], axis=1)[:, 0]
    topv, topi = lax.top_k(el_sel, MOE_TOPK)
    w = jax.nn.softmax(topv, axis=-1) * jnp.take_along_axis(pg, gsel[:, None], axis=1)
    eidx = gsel[:, None] * MOE_PER_GROUP + topi
    gate = jnp.einsum('tk,tke->te', w, jax.nn.one_hot(eidx, MOE_E, dtype=F32))
    hid = (jax.nn.silu(jnp.einsum('td,edf->tef', t, P['moe_w_gate'].astype(t.dtype)))
           * jnp.einsum('td,edf->tef', t, P['moe_w_up'].astype(t.dtype)))
    out = jnp.einsum('tef,efd->td', hid * gate[:, :, None].astype(hid.dtype), P['moe_w_down'].astype(t.dtype))
    return out.reshape(b, L, d)


def trunk_layer(x, ada, P, ctx_cache):
    sh1, sc1, g1, sh2, sc2, g2 = [a[:, None, :].astype(x.dtype) for a in jnp.split(ada, 6, axis=-1)]
    b = x.shape[0]
    h = rmsnorm(x, P['norm_pre1']) * (1.0 + sc1) + sh1
    proj = h @ P['w_in'].astype(h.dtype)
    if ctx_cache is None:
        s5_h0 = jnp.zeros((b, 2, S5_G, S5_P, 2), F32)
        ssd_h0 = jnp.zeros((b, 2, SSD_H, SSD_P, SSD_N), F32)
        ctx_kv = None
    else:
        ckv_c, kr_c, s5_h0, ssd_h0 = ctx_cache
        ctx_kv = (ckv_c, kr_c)
    y_s5, st_s5 = s5_mixer(proj[..., OFF_S5:OFF_MLA_Q], P, s5_h0)
    y_mla, ckv, kr = mla_mixer(proj, P, ctx_kv)
    y_ssd, st_ssd = ssd_mixer(proj, P, ssd_h0)
    y_conf = conformer_conv(proj[..., OFF_CONF:IN_COLS], P)
    mix = jnp.concatenate([y_s5.astype(x.dtype), y_mla.astype(x.dtype),
                           y_ssd.astype(x.dtype), y_conf.astype(x.dtype)], axis=-1)
    x = x + g1 * rmsnorm(mix @ P['w_out'].astype(mix.dtype), P['norm_post1'])
    h = rmsnorm(x, P['norm_pre2']) * (1.0 + sc2) + sh2
    x = x + g2 * rmsnorm(hier_moe(h, P), P['norm_post2'])
    return x, (ckv, kr, st_s5, st_ssd)


def setup_inputs(seed: int = 0) -> dict:
    key = jax.random.key(seed)
    ks = jax.random.split(key, 64)
    ctr = [0]

    def nk():
        ctr[0] += 1
        return ks[ctr[0] - 1]

    def nrm(shape, scale=1.0):
        return jax.random.normal(nk(), shape, F32) * scale

    def gain(shape):
        return 1.0 + 0.01 * jax.random.normal(nk(), shape, F32)

    def unif(shape, lo, hi):
        return jax.random.uniform(nk(), shape, F32, lo, hi)

    d = D_MODEL
    inp = {}
    inp['x_prompt'] = nrm((BATCH, SEQ, d))
    inp['x_sample'] = nrm((DEC_BATCH, DEC_SEQ, d))
    inp['cache_mla_ckv'] = nrm((DEC_BATCH, DEPTH, PAST_LEN, MLA_KVLORA))
    inp['cache_mla_krope'] = nrm((DEC_BATCH, DEPTH, PAST_LEN, MLA_ROPE))
    inp['state_s5'] = nrm((DEC_BATCH, DEPTH, 2, S5_G, S5_P, 2), 0.5)
    inp['state_ssd'] = nrm((DEC_BATCH, DEPTH, 2, SSD_H, SSD_P, SSD_N), 0.5)
    inp['c'] = nrm((DEC_BATCH, d))
    inp['c_ctx'] = nrm((d,))
    inp['w_ada'] = nrm((DEPTH, d, 6 * d), 0.5 * d ** -0.5)
    inp['b_ada'] = nrm((DEPTH, 6 * d), 0.01)
    inp['norm_pre1'] = gain((DEPTH, d))
    inp['norm_post1'] = gain((DEPTH, d))
    inp['norm_pre2'] = gain((DEPTH, d))
    inp['norm_post2'] = gain((DEPTH, d))
    inp['w_in'] = nrm((DEPTH, d, IN_COLS), d ** -0.5)
    inp['w_out'] = nrm((DEPTH, MIX_W, d), MIX_W ** -0.5)
    inp['s5_lam_re'] = -0.5 + nrm((DEPTH, 2, S5_G, S5_P), 0.01)
    inp['s5_lam_im'] = (jnp.pi * jnp.arange(S5_P, dtype=F32)) + nrm((DEPTH, 2, S5_G, S5_P), 0.01)
    inp['s5_log_dt'] = unif((DEPTH, 2, S5_G), math.log(1e-3), math.log(1e-1))
    inp['s5_b_re'] = nrm((DEPTH, 2, S5_G, S5_P, S5_GSIZE), (2 * S5_GSIZE) ** -0.5)
    inp['s5_b_im'] = nrm((DEPTH, 2, S5_G, S5_P, S5_GSIZE), (2 * S5_GSIZE) ** -0.5)
    inp['s5_c_re'] = nrm((DEPTH, 2, S5_G, S5_GSIZE, S5_P), (2 * S5_P) ** -0.5)
    inp['s5_c_im'] = nrm((DEPTH, 2, S5_G, S5_GSIZE, S5_P), (2 * S5_P) ** -0.5)
    inp['s5_d'] = nrm((DEPTH, S5_W))
    inp['s5_w_glu'] = nrm((DEPTH, S5_W, S5_W), S5_W ** -0.5)
    inp['s5_b_glu'] = nrm((DEPTH, S5_W), 0.01)
    inp['mla_qnorm'] = gain((DEPTH, MLA_QLORA))
    inp['mla_kvnorm'] = gain((DEPTH, MLA_KVLORA))
    inp['mla_wuq'] = nrm((DEPTH, MLA_QLORA, MLA_H, MLA_NOPE + MLA_ROPE), MLA_QLORA ** -0.5)
    inp['mla_wuk'] = nrm((DEPTH, MLA_KVLORA, MLA_H, MLA_NOPE), MLA_KVLORA ** -0.5)
    inp['mla_wuv'] = nrm((DEPTH, MLA_KVLORA, MLA_H, MLA_V), MLA_KVLORA ** -0.5)
    inp['ssd_conv_w'] = nrm((DEPTH, SSD_CONV, SSD_XBC), SSD_CONV ** -0.5)
    inp['ssd_conv_b'] = nrm((DEPTH, SSD_XBC), 0.01)
    dt0 = jnp.exp(unif((DEPTH, 2, SSD_H), math.log(1e-3), math.log(1e-1)))
    inp['ssd_dt_bias'] = dt0 + jnp.log(-jnp.expm1(-dt0))
    inp['ssd_a_log'] = jnp.log(unif((DEPTH, 2, SSD_H), 1.0, 16.0))
    inp['ssd_d'] = gain((DEPTH, SSD_H))
    inp['ssd_norm_w'] = gain((DEPTH, SSD_W))
    inp['conf_dw_w'] = nrm((DEPTH, CONF_K, CONF_W), CONF_K ** -0.5)
    inp['conf_dw_b'] = nrm((DEPTH, CONF_W), 0.01)
    inp['conf_ln_g'] = gain((DEPTH, CONF_W))
    inp['conf_ln_b'] = nrm((DEPTH, CONF_W), 0.01)
    inp['moe_wg'] = nrm((DEPTH, d, MOE_GROUPS), d ** -0.5)
    inp['moe_bg'] = nrm((DEPTH, MOE_GROUPS), 0.01)
    inp['moe_we'] = nrm((DEPTH, d, MOE_E), d ** -0.5)
    inp['moe_be'] = nrm((DEPTH, MOE_E), 0.01)
    inp['moe_w_gate'] = nrm((DEPTH, MOE_E, d, MOE_HID), d ** -0.5)
    inp['moe_w_up'] = nrm((DEPTH, MOE_E, d, MOE_HID), d ** -0.5)
    inp['moe_w_down'] = nrm((DEPTH, MOE_E, MOE_HID, d), MOE_HID ** -0.5)
    return inp


def reference(x_prompt, x_sample, cache_mla_ckv, cache_mla_krope, state_s5, state_ssd, c, c_ctx,
              w_ada, b_ada, norm_pre1, norm_post1, norm_pre2, norm_post2, w_in, w_out,
              s5_lam_re, s5_lam_im, s5_log_dt, s5_b_re, s5_b_im, s5_c_re, s5_c_im, s5_d, s5_w_glu, s5_b_glu,
              mla_qnorm, mla_kvnorm, mla_wuq, mla_wuk, mla_wuv,
              ssd_conv_w, ssd_conv_b, ssd_dt_bias, ssd_a_log, ssd_d, ssd_norm_w,
              conf_dw_w, conf_dw_b, conf_ln_g, conf_ln_b,
              moe_wg, moe_bg, moe_we, moe_be, moe_w_gate, moe_w_up, moe_w_down):
    yp = x_prompt
    ys = x_sample
    ckv_l, kr_l, s5_l, ssd_l = [], [], [], []
    for l in range(DEPTH):
        P = dict(
            norm_pre1=norm_pre1[l], norm_post1=norm_post1[l], norm_pre2=norm_pre2[l], norm_post2=norm_post2[l],
            w_in=w_in[l], w_out=w_out[l],
            s5_lam_re=s5_lam_re[l], s5_lam_im=s5_lam_im[l], s5_log_dt=s5_log_dt[l],
            s5_b_re=s5_b_re[l], s5_b_im=s5_b_im[l], s5_c_re=s5_c_re[l], s5_c_im=s5_c_im[l],
            s5_d=s5_d[l], s5_w_glu=s5_w_glu[l], s5_b_glu=s5_b_glu[l],
            mla_qnorm=mla_qnorm[l], mla_kvnorm=mla_kvnorm[l], mla_wuq=mla_wuq[l], mla_wuk=mla_wuk[l],
            mla_wuv=mla_wuv[l],
            ssd_conv_w=ssd_conv_w[l], ssd_conv_b=ssd_conv_b[l], ssd_dt_bias=ssd_dt_bias[l],
            ssd_a_log=ssd_a_log[l], ssd_d=ssd_d[l], ssd_norm_w=ssd_norm_w[l],
            conf_dw_w=conf_dw_w[l], conf_dw_b=conf_dw_b[l], conf_ln_g=conf_ln_g[l], conf_ln_b=conf_ln_b[l],
            moe_wg=moe_wg[l], moe_bg=moe_bg[l], moe_we=moe_we[l], moe_be=moe_be[l],
            moe_w_gate=moe_w_gate[l], moe_w_up=moe_w_up[l], moe_w_down=moe_w_down[l])
        ada_ctx = (jax.nn.silu(c_ctx) @ w_ada[l] + b_ada[l])[None, :]
        ada_lat = jax.nn.silu(c) @ w_ada[l] + b_ada[l]
        yp, (ckv, kr, st_s5, st_ssd) = trunk_layer(yp, ada_ctx, P, None)
        ckv_l.append(ckv)
        kr_l.append(kr)
        s5_l.append(st_s5)
        ssd_l.append(st_ssd)
        ys, _ = trunk_layer(ys, ada_lat, P, (cache_mla_ckv[:, l], cache_mla_krope[:, l],
                                              state_s5[:, l], state_ssd[:, l]))
    new_ckv = jnp.stack(ckv_l, axis=1)
    new_kr = jnp.stack(kr_l, axis=1)
    new_s5 = jnp.stack(s5_l, axis=1)
    new_ssd = jnp.stack(ssd_l, axis=1)
    return (yp, ys, new_ckv, new_kr, new_s5, new_ssd)
```

```python
import functools
import math

import jax
import jax.numpy as jnp
import numpy as np
from jax import lax
from jax.experimental import pallas as pl
from jax.experimental.pallas import tpu as pltpu

F32 = jnp.float32
BF16 = jnp.bfloat16

D_MODEL = 1024
DEPTH = 2
GRID_W = 64
EPS = 1e-6

S5_W = 256
S5_GSIZE = 16
S5_G = 16
S5_P = 64
S5_CHUNK = 16

MLA_H = 4
MLA_NOPE = 64
MLA_ROPE = 32
MLA_V = 64
MLA_QLORA = 256
MLA_KVLORA = 128
ROPE_BASE = 10000.0
MLA_TQ = 256
MLA_KC = 256

SSD_W = 256
SSD_P = 64
SSD_H = 4
SSD_G = 2
SSD_N = 64
SSD_CHUNK = 128
SSD_XBC = 512

CONF_W = 256
CONF_K = 31
CONF_Q = 64
CONF_PAD = 16

MOE_GROUPS = 4
MOE_PER_GROUP = 4
MOE_E = 16
MOE_HID = 256
MOE_TM = 512

OFF_S5 = 0
OFF_MLA_Q = 256
OFF_MLA_KV = 512
OFF_MLA_KR = 640
OFF_SSD_Z = 672
OFF_SSD_XBC = 928
OFF_SSD_DT = 1440
OFF_CONF = 1444
IN_COLS = 1956

LANE = 128
SUBLANE = 8
TM = 256
P_U, P_Q, P_KV, P_KRA, P_KRB, P_Z, P_XBC, P_DT, P_CA, P_CB, P_END = (
    0, 256, 512, 640, 768, 896, 1152, 1664, 1792, 2048, 2304)

VMEM_LIMIT = 56 * 1024 * 1024


def _bdot(a, b):
    return jnp.dot(a.astype(BF16), b.astype(BF16), preferred_element_type=F32)


def _bdot_nt(a, b):
    return lax.dot_general(a.astype(BF16), b.astype(BF16), (((1,), (1,)), ((), ())), preferred_element_type=F32)


def _rms(x, g):
    return x * lax.rsqrt(jnp.mean(x * x, axis=-1, keepdims=True) + EPS) * g


def _cparams(*sem):
    return pltpu.CompilerParams(dimension_semantics=sem if sem else None, vmem_limit_bytes=VMEM_LIMIT)


def _ada_kernel(c_ref, w_ref, b_ref, o_ref):
    o_ref[0] = _bdot(jax.nn.silu(c_ref[...]), w_ref[0]) + b_ref[0]


def _ada_call(crows, w_ada, b_ada):
    tn = 1024
    d6 = 6 * D_MODEL
    return pl.pallas_call(
        _ada_kernel, grid=(DEPTH, d6 // tn),
        in_specs=[pl.BlockSpec((8, D_MODEL), lambda l, j: (0, 0)),
                  pl.BlockSpec((1, D_MODEL, tn), lambda l, j: (l, 0, j)),
                  pl.BlockSpec((1, 1, tn), lambda l, j: (l, 0, j))],
        out_specs=pl.BlockSpec((1, 8, tn), lambda l, j: (l, 0, j)),
        out_shape=jax.ShapeDtypeStruct((DEPTH, 8, d6), F32),
        compiler_params=_cparams("arbitrary", "arbitrary"), name="ada",
    )(crows, w_ada, b_ada.reshape(DEPTH, 1, d6))


def _inproj_kernel(x_ref, ada_ref, npre_ref, w_ref, qn_ref, kvn_ref,
                   u_ref, cq_ref, ckv_ref, krab_ref, z_ref, xbc_ref, dt_ref, v_ref):
    ada = ada_ref[0]
    h = _rms(x_ref[...], npre_ref[...]) * (1.0 + ada[1:2]) + ada[0:1]
    p = _bdot(h, w_ref[...])
    u_ref[...] = p[:, P_U:P_Q]
    cq_ref[...] = _rms(p[:, P_Q:P_KV], qn_ref[...]).astype(BF16)
    ckv_ref[...] = _rms(p[:, P_KV:P_KRA], kvn_ref[...])
    krab_ref[...] = p[:, P_KRA:P_Z]
    z_ref[...] = p[:, P_Z:P_XBC]
    xbc_ref[...] = p[:, P_XBC:P_DT]
    dt_ref[...] = p[:, P_DT:P_CA]
    v_ref[...] = p[:, P_CA:P_CB] * jax.nn.sigmoid(p[:, P_CB:P_END])


def _inproj_call(x, ada_t, npre, w, qn, kvn):
    t = x.shape[0]
    row = lambda n: pl.BlockSpec((TM, n), lambda i: (i, 0))
    full = lambda a: pl.BlockSpec(a.shape, lambda i: (0,) * a.ndim)
    widths = (256, 256, 128, 256, 256, 512, 128, 256)
    dts = (F32, BF16, F32, F32, F32, F32, F32, F32)
    return pl.pallas_call(
        _inproj_kernel, grid=(t // TM,),
        in_specs=[row(D_MODEL), pl.BlockSpec((1, 8, D_MODEL), lambda i: (i, 0, 0)), full(npre), full(w), full(qn),
                  full(kvn)],
        out_specs=[row(n) for n in widths],
        out_shape=[jax.ShapeDtypeStruct((t, n), d) for n, d in zip(widths, dts)],
        compiler_params=_cparams("arbitrary"), name="inproj",
    )(x, ada_t, npre, w, qn, kvn)


def _s5_kernel(u_ref, r_ref, vf_ref, vb_ref, a_ref, h0f_ref, h0fs_ref, h0b_ref, h0bs_ref,
               y_ref, stf_ref, stb_ref, sf, sfs, sb, sbs, *, nb, nc):
    gw = 2 * S5_P
    for g in range(S5_G):
        r = _bdot(u_ref[g], r_ref[g])
        y_ref[g] = r[:, 0:256]
        sl = slice(g * gw, (g + 1) * gw)
        sf[:, sl] = r[:, 256:384]
        sfs[:, sl] = r[:, 384:512]
        sb[:, sl] = r[:, 512:640]
        sbs[:, sl] = r[:, 640:768]
    arf, aif, aisf = a_ref[0:1], a_ref[1:2], a_ref[2:3]
    arb, aib, aisb = a_ref[3:4], a_ref[4:5], a_ref[5:6]
    cpt = 1 if nb % SUBLANE == 0 else SUBLANE // nb
    rows = cpt * nb
    nt = nc // cpt

    def body(i, carry):
        hf, hfs, hb, hbs = carry
        rf = pl.ds(pl.multiple_of(i * rows, rows), rows)
        rb = pl.ds(pl.multiple_of((nt - 1 - i) * rows, rows), rows)
        inc_f, inc_fs, inc_b, inc_bs = sf[rf, :], sfs[rf, :], sb[rb, :], sbs[rb, :]
        in_f, in_b = [None] * cpt, [None] * cpt
        for k in range(cpt):
            kf = slice(k * nb, (k + 1) * nb)
            kb = slice((cpt - 1 - k) * nb, (cpt - k) * nb)
            in_f[k] = hf
            in_b[cpt - 1 - k] = hb
            hf, hfs = arf * hf + aif * hfs + inc_f[kf], arf * hfs + aisf * hf + inc_fs[kf]
            hb, hbs = arb * hb + aib * hbs + inc_b[kb], arb * hbs + aisb * hb + inc_bs[kb]
        sf[rf, :] = in_f[0] if cpt == 1 else jnp.concatenate(in_f, 0)
        sb[rb, :] = in_b[0] if cpt == 1 else jnp.concatenate(in_b, 0)
        return hf, hfs, hb, hbs

    hf, _, hb, _ = lax.fori_loop(0, nt, body, (h0f_ref[...], h0fs_ref[...], h0b_ref[...], h0bs_ref[...]))
    stf_ref[...] = hf
    stb_ref[...] = hb
    for g in range(S5_G):
        sl = slice(g * gw, (g + 1) * gw)
        y_ref[g] += _bdot(sf[:, sl], vf_ref[g]) + _bdot(sb[:, sl], vb_ref[g])


def _s5_call(u, tabs, h0, nb):
    n = u.shape[1]
    r, vf, vb, avec = tabs
    sw = S5_G * 2 * S5_P
    st = jax.ShapeDtypeStruct((nb, sw), F32)
    return pl.pallas_call(
        functools.partial(_s5_kernel, nb=nb, nc=n // nb),
        out_shape=[jax.ShapeDtypeStruct(u.shape, F32), st, st],
        scratch_shapes=[pltpu.VMEM((n, sw), F32) for _ in range(4)],
        compiler_params=_cparams(), name="s5",
    )(u, r, vf, vb, avec, *h0)


def _s5_tables(lam_re, lam_im, log_dt, b_re, b_im, c_re, c_im, dskip):
    q = S5_CHUNK
    hp = lax.Precision.HIGHEST
    dt = jnp.exp(log_dt)[..., None]
    ld_re, ld_im = lam_re * dt, lam_im * dt
    j = jnp.arange(q + 1, dtype=F32)[:, None, None, None]
    mag = jnp.exp(j * ld_re)
    pw_re, pw_im = mag * jnp.cos(j * ld_im), mag * jnp.sin(j * ld_im)
    abr, abi = pw_re[1], pw_im[1]
    den = lam_re * lam_re + lam_im * lam_im
    fr = ((abr - 1.0) * lam_re + abi * lam_im) / den
    fi = (abi * lam_re - (abr - 1.0) * lam_im) / den
    bbr = fr[..., None] * b_re - fi[..., None] * b_im
    bbi = fr[..., None] * b_im + fi[..., None] * b_re
    cbr = jnp.einsum('dgop,dgpi->dgopi', c_re, bbr) - jnp.einsum('dgop,dgpi->dgopi', c_im, bbi)
    cbi = jnp.einsum('dgop,dgpi->dgopi', c_re, bbi) + jnp.einsum('dgop,dgpi->dgopi', c_im, bbr)
    kk = (jnp.einsum('jdgp,dgopi->dgjoi', pw_re[:q], cbr, precision=hp)
          - jnp.einsum('jdgp,dgopi->dgjoi', pw_im[:q], cbi, precision=hp))
    s_idx = jnp.arange(q)[:, None]
    t_idx = jnp.arange(q)[None, :]
    lag_f = jnp.clip(t_idx - s_idx, 0, q - 1)
    lag_b = jnp.clip(s_idx - t_idx, 0, q - 1)
    tf = jnp.where((t_idx >= s_idx)[None, :, :, None, None], kk[0][:, lag_f], 0.0)
    tb = jnp.where((s_idx >= t_idx)[None, :, :, None, None], kk[1][:, lag_b], 0.0)
    eye_t = jnp.eye(q, dtype=F32)[None, :, :, None, None]
    eye_c = jnp.eye(S5_GSIZE, dtype=F32)[None, None, None, :, :]
    tsk = eye_t * eye_c * dskip.reshape(S5_G, 1, 1, S5_GSIZE, 1)
    tt = jnp.transpose(tf + tb + tsk, (0, 1, 4, 2, 3)).reshape(S5_G, 256, 256)

    def state_tab(d, powers):
        pr = pw_re[powers, d]
        pi = pw_im[powers, d]
        wr = pr[..., None] * bbr[d][None] - pi[..., None] * bbi[d][None]
        wi = pr[..., None] * bbi[d][None] + pi[..., None] * bbr[d][None]
        wr = jnp.transpose(wr, (1, 0, 3, 2)).reshape(S5_G, 256, S5_P)
        wi = jnp.transpose(wi, (1, 0, 3, 2)).reshape(S5_G, 256, S5_P)
        return jnp.concatenate([wr, wi], -1), jnp.concatenate([wi, wr], -1)

    wf, wfs = state_tab(0, q - 1 - jnp.arange(q))
    wb, wbs = state_tab(1, jnp.arange(q))
    rtab = jnp.concatenate([tt, wf, wfs, wb, wbs], -1).astype(BF16)

    def out_tab(d, powers):
        pr = pw_re[powers, d]
        pi = pw_im[powers, d]
        vr = c_re[d][None] * pr[:, :, None, :] - c_im[d][None] * pi[:, :, None, :]
        vi = c_re[d][None] * pi[:, :, None, :] + c_im[d][None] * pr[:, :, None, :]
        vr = jnp.transpose(vr, (1, 3, 0, 2)).reshape(S5_G, S5_P, 256)
        vi = jnp.transpose(vi, (1, 3, 0, 2)).reshape(S5_G, S5_P, 256)
        return jnp.concatenate([vr, -vi], 1).astype(BF16)

    vf = out_tab(0, 1 + jnp.arange(q))
    vb = out_tab(1, q - jnp.arange(q))

    def lanes(x_first, x_second):
        return jnp.concatenate([x_first, x_second], -1).reshape(1, S5_G * 2 * S5_P)

    ar, ai = pw_re[q], pw_im[q]
    rows = []
    for d in range(2):
        rows += [lanes(ar[d], ar[d]), lanes(-ai[d], ai[d]), lanes(ai[d], -ai[d])]
    avec = jnp.concatenate(rows + [jnp.zeros((2, S5_G * 2 * S5_P), F32)], 0)
    return rtab, vf, vb, avec


def _s5_layout_in(u, nb, seq):
    nc = seq // S5_CHUNK
    return jnp.transpose(u.reshape(nb, nc, S5_CHUNK, S5_G, S5_GSIZE), (3, 1, 0, 2, 4)).reshape(S5_G, nc * nb, 256)


def _s5_layout_out(y, nb, seq):
    nc = seq // S5_CHUNK
    return jnp.transpose(y.reshape(S5_G, nc, nb, S5_CHUNK, S5_GSIZE), (2, 1, 3, 0, 4)).reshape(nb * seq, S5_W)


def _s5_state_in(state):
    b = state.shape[0]
    sw = jnp.transpose(state, (0, 1, 2, 4, 3))
    plain = sw.reshape(b, 2, S5_G * 2 * S5_P)
    swapped = sw[:, :, :, ::-1, :].reshape(b, 2, S5_G * 2 * S5_P)
    return plain[:, 0], swapped[:, 0], plain[:, 1], swapped[:, 1]


def _s5_state_out(stf, stb):
    b = stf.shape[0]
    st = jnp.stack([stf, stb], 1).reshape(b, 2, S5_G, 2, S5_P)
    return jnp.transpose(st, (0, 1, 2, 4, 3))


def _mla_kernel(*refs, seq, lctx):
    if lctx:
        (cq_ref, ckv_ref, krab_ref, cosq_ref, sinq_ref, cosk_ref, sink_ref, cckv_ref, ckr_ref,
         wqa_ref, wqb_ref, wk_ref, wv_ref, o_ref, k_scr, v_scr) = refs
    else:
        (cq_ref, ckv_ref, krab_ref, cosq_ref, sinq_ref, cosk_ref, sink_ref,
         wqa_ref, wqb_ref, wk_ref, wv_ref, o_ref, k_scr, v_scr) = refs
    lk = lctx + seq
    rb = min(seq, 512)

    @pl.when(pl.program_id(1) == 0)
    def _build_keys():
        if lctx:
            ckv = cckv_ref[0]
            kn = _bdot(ckv, wk_ref[...])
            vn = _bdot(ckv, wv_ref[...])
            for h in range(MLA_H):
                k_scr[h, 0:lctx, :] = (kn[:, h * LANE:(h + 1) * LANE] + ckr_ref[0]).astype(BF16)
                v_scr[h, 0:lctx, :] = vn[:, h * LANE:(h + 1) * LANE].astype(BF16)

        def chunk(i, _):
            r0 = pl.multiple_of(i * rb, rb)
            rows = pl.ds(r0, rb)
            ckv = ckv_ref[rows, :]
            kn = _bdot(ckv, wk_ref[...])
            vn = _bdot(ckv, wv_ref[...])
            krab = krab_ref[rows, :]
            krx = krab[:, :LANE] * cosk_ref[rows, :] + krab[:, LANE:] * sink_ref[rows, :]
            dst = pl.ds(pl.multiple_of(lctx + r0, rb if lctx % rb == 0 else 256), rb)
            for h in range(MLA_H):
                k_scr[h, dst, :] = (kn[:, h * LANE:(h + 1) * LANE] + krx).astype(BF16)
                v_scr[h, dst, :] = vn[:, h * LANE:(h + 1) * LANE].astype(BF16)
            return 0

        lax.fori_loop(0, seq // rb, chunk, 0)

    cq = cq_ref[...]
    qa = _bdot(cq, wqa_ref[...])
    qb = _bdot(cq, wqb_ref[...])
    cosq, sinq = cosq_ref[...], sinq_ref[...]
    scale = 1.0 / math.sqrt(MLA_NOPE + MLA_ROPE)
    outs = []
    for h in range(MLA_H):
        hs = slice(h * LANE, (h + 1) * LANE)
        qh = ((qa[:, hs] * cosq + qb[:, hs] * sinq) * scale).astype(BF16)

        def step(j, carry, h=h, qh=qh):
            m, l, acc = carry
            rows = pl.ds(pl.multiple_of(j * MLA_KC, MLA_KC), MLA_KC)
            s = _bdot_nt(qh, k_scr[h, rows, :])
            mn = jnp.maximum(m, jnp.max(s, axis=-1, keepdims=True))
            alpha = jnp.exp(m - mn)
            p = jnp.exp(s - mn)
            l = alpha * l + jnp.sum(p, axis=-1, keepdims=True)
            acc = alpha * acc + _bdot(p, v_scr[h, rows, :])
            return mn, l, acc

        init = (jnp.full((MLA_TQ, 1), -jnp.inf, F32), jnp.zeros((MLA_TQ, 1), F32), jnp.zeros((MLA_TQ, LANE), F32))
        _, l, acc = lax.fori_loop(0, lk // MLA_KC, step, init)
        outs.append((acc / l)[:, :MLA_V])
    o_ref[...] = jnp.concatenate(outs, axis=-1)


def _mla_call(cq, ckv, krab, rope, ctx, w, nb, seq, row0):
    lctx = 0 if ctx is None else ctx[0].shape[1]
    nq = seq // MLA_TQ
    q0, s0 = row0 // MLA_TQ, row0 // seq
    cosx, sinx = rope
    qrow = lambda n: pl.BlockSpec((MLA_TQ, n), lambda b, i: (q0 + b * nq + i, 0))
    srow = lambda n: pl.BlockSpec((seq, n), lambda b, i: (s0 + b, 0))
    full = lambda a: pl.BlockSpec(a.shape, lambda b, i: (0,) * a.ndim)
    in_specs = [qrow(256), srow(128), srow(256), pl.BlockSpec((MLA_TQ, LANE), lambda b, i: (i, 0)),
                pl.BlockSpec((MLA_TQ, LANE), lambda b, i: (i, 0)), full(cosx), full(sinx)]
    args = [cq, ckv, krab, cosx, sinx, cosx, sinx]
    if lctx:
        in_specs += [pl.BlockSpec((1, lctx, LANE), lambda b, i: (b, 0, 0))] * 2
        args += list(ctx)
    in_specs += [full(a) for a in w]
    args += list(w)
    lk = lctx + seq
    return pl.pallas_call(
        functools.partial(_mla_kernel, seq=seq, lctx=lctx), grid=(nb, nq),
        in_specs=in_specs,
        out_specs=pl.BlockSpec((MLA_TQ, 256), lambda b, i: (b * nq + i, 0)),
        out_shape=jax.ShapeDtypeStruct((nb * seq, 256), F32),
        scratch_shapes=[pltpu.VMEM((MLA_H, lk, LANE), BF16), pltpu.VMEM((MLA_H, lk, LANE), BF16)],
        compiler_params=_cparams("arbitrary", "arbitrary"), name="mla",
    )(*args)


def _swap8(w):
    s = w.shape
    return w.reshape(s[:-1] + (s[-1] // 16, 2, 8))[..., ::-1, :].reshape(s)


def _mla_weights(wuq, wuk, wuv):
    z32 = jnp.zeros((MLA_QLORA, 32), F32)
    z64 = jnp.zeros((MLA_QLORA, 64), F32)
    k64 = jnp.zeros((MLA_KVLORA, 64), F32)
    qa, qb, wk, wv = [], [], [], []
    for h in range(MLA_H):
        rope_w = wuq[:, h, MLA_NOPE:]
        qa += [wuq[:, h, :MLA_NOPE], rope_w, z32]
        qb += [z64, _swap8(rope_w), z32]
        wk += [wuk[:, h], k64]
        wv += [wuv[:, h], k64]
    cat = lambda xs: jnp.concatenate(xs, -1).astype(BF16)
    return cat(qa), cat(qb), cat(wk), cat(wv)


def _rope_tables(seq, rotate):
    ones = jnp.ones((seq, MLA_NOPE), F32)
    z32 = jnp.zeros((seq, 32), F32)
    if not rotate:
        return (jnp.concatenate([ones, jnp.ones((seq, MLA_ROPE), F32), z32], -1), jnp.zeros((seq, LANE), F32))
    pos = jnp.arange(seq)
    row = (pos // GRID_W).astype(F32)
    col = (pos % GRID_W).astype(F32)
    half = MLA_ROPE // 2
    inv = ROPE_BASE ** (-jnp.arange(0, half, 2, dtype=F32) / half)
    ang = jnp.concatenate([row[:, None] * inv, col[:, None] * inv], -1)
    cos, sin = jnp.cos(ang), jnp.sin(ang)
    q = MLA_ROPE // 4
    cos32 = jnp.concatenate([cos[:, :q], cos[:, :q], cos[:, q:], cos[:, q:]], -1)
    sin32 = jnp.concatenate([-sin[:, :q], sin[:, :q], -sin[:, q:], sin[:, q:]], -1)
    return jnp.concatenate([ones, cos32, z32], -1), jnp.concatenate([jnp.zeros_like(ones), sin32, z32], -1)


def _cumsum_rows(x, reverse):
    n = x.shape[0]
    row = lax.broadcasted_iota(jnp.int32, x.shape, 0)
    s = 1
    while s < n:
        if reverse:
            x = x + jnp.where(row < n - s, pltpu.roll(x, n - s, 0), 0.0)
        else:
            x = x + jnp.where(row >= s, pltpu.roll(x, s, 0), 0.0)
        s *= 2
    return x


def _ssd_kernel(xbc_ref, z_ref, dt_ref, cw_ref, cb_ref, par_ref, dsk_ref, nw_ref, h0_ref,
                y_ref, st_ref, xc_scr, yf_scr, yb_scr, hf_scr, hb_scr, *, seq):
    q = SSD_CHUNK
    nc = seq // q
    row = lax.broadcasted_iota(jnp.int32, (q, 1), 0)

    def conv(c, _):
        t0 = pl.multiple_of(c * q, q)
        cur = xbc_ref[pl.ds(t0, q), :]
        prev8 = xbc_ref[pl.ds(pl.multiple_of(jnp.maximum(t0 - 8, 0), 8), 8), :]
        next8 = xbc_ref[pl.ds(pl.multiple_of(jnp.minimum(t0 + q, seq - 8), 8), 8), :]
        prow = jnp.where(c > 0, prev8[7:8], 0.0)
        nrow = jnp.where(c < nc - 1, next8[0:1], 0.0)
        up = jnp.where(row == 0, prow, pltpu.roll(cur, 1, 0))
        dn = jnp.where(row == q - 1, nrow, pltpu.roll(cur, q - 1, 0))
        acc = up * cw_ref[0:1] + cur * cw_ref[1:2] + dn * cw_ref[2:3] + cb_ref[...]
        xc_scr[pl.ds(t0, q), :] = jax.nn.silu(acc)
        return 0

    lax.fori_loop(0, nc, conv, 0)
    hf_scr[...] = h0_ref[0, 0]
    hb_scr[...] = h0_ref[0, 1]
    ii = lax.broadcasted_iota(jnp.int32, (q, q), 0)
    jj = lax.broadcasted_iota(jnp.int32, (q, q), 1)

    def one_chunk(c, d, h_scr, y_scr):
        t0 = pl.multiple_of(c * q, q)
        xc = xc_scr[pl.ds(t0, q), :]
        xs = xc[:, :SSD_W]
        xs_t = jnp.transpose(xs)
        dt = jax.nn.softplus(dt_ref[pl.ds(t0, q), :] + par_ref[d:d + 1])
        dta = dt * par_ref[2 + d:3 + d]
        cs = _cumsum_rows(dta, reverse=(d == 1))
        cs_t = jnp.transpose(cs)
        dt_t = jnp.transpose(dt)
        edge = cs[q - 1:q] if d == 0 else cs[0:1]
        wts = jnp.exp(edge - cs) * dt
        ecs = jnp.exp(cs)
        mask = (ii >= jj) if d == 0 else (ii <= jj)
        for g in range(SSD_G):
            bm = xc[:, SSD_W + g * SSD_N:SSD_W + (g + 1) * SSD_N]
            cm = xc[:, SSD_W + SSD_G * SSD_N + g * SSD_N:SSD_W + SSD_G * SSD_N + (g + 1) * SSD_N]
            cbm = _bdot_nt(cm, bm)
            for hh in range(SSD_H // SSD_G):
                h = g * (SSD_H // SSD_G) + hh
                ps = slice(h * SSD_P, (h + 1) * SSD_P)
                lmat = jnp.exp(jnp.where(mask, cs[:, h:h + 1] - cs_t[h:h + 1, :], -jnp.inf))
                y = _bdot(cbm * lmat * dt_t[h:h + 1, :], xs[:, ps])
                h_in = h_scr[ps, :]
                y = y + _bdot_nt(cm, h_in) * ecs[:, h:h + 1]
                y_scr[pl.ds(t0, q), ps] = y
                st = _bdot(xs_t[ps, :], bm * wts[:, h:h + 1])
                h_scr[ps, :] = jnp.exp(edge[:, h:h + 1]) * h_in + st

    def chunks(c, _):
        one_chunk(c, 0, hf_scr, yf_scr)
        one_chunk(nc - 1 - c, 1, hb_scr, yb_scr)
        return 0

    lax.fori_loop(0, nc, chunks, 0)
    st_ref[0, 0] = hf_scr[...]
    st_ref[0, 1] = hb_scr[...]

    def finish(c, _):
        rows = pl.ds(pl.multiple_of(c * q, q), q)
        y = yf_scr[rows, :] + yb_scr[rows, :] + dsk_ref[...] * xc_scr[rows, 0:SSD_W]
        y_ref[rows, :] = _rms(y * jax.nn.silu(z_ref[rows, :]), nw_ref[...])
        return 0

    lax.fori_loop(0, nc, finish, 0)


def _ssd_call(xbc, z, dt, w, h0, nb, seq, row0):
    s0 = row0 // seq
    srow = lambda n: pl.BlockSpec((seq, n), lambda b: (s0 + b, 0))
    full = lambda a: pl.BlockSpec(a.shape, lambda b: (0,) * a.ndim)
    hp = SSD_H * SSD_P
    st_spec = pl.BlockSpec((1, 2, hp, SSD_N), lambda b: (b, 0, 0, 0))
    return pl.pallas_call(
        functools.partial(_ssd_kernel, seq=seq), grid=(nb,),
        in_specs=[srow(SSD_XBC), srow(SSD_W), srow(LANE)] + [full(a) for a in w] + [st_spec],
        out_specs=[pl.BlockSpec((seq, SSD_W), lambda b: (b, 0)), st_spec],
        out_shape=[jax.ShapeDtypeStruct((nb * seq, SSD_W), F32), jax.ShapeDtypeStruct((nb, 2, hp, SSD_N), F32)],
        scratch_shapes=[pltpu.VMEM((seq, SSD_XBC), F32), pltpu.VMEM((seq, SSD_W), F32), pltpu.VMEM((seq, SSD_W), F32),
                        pltpu.VMEM((hp, SSD_N), F32), pltpu.VMEM((hp, SSD_N), F32)],
        compiler_params=_cparams("arbitrary"), name="ssd",
    )(xbc, z, dt, *w, h0)


def _ssd_weights(conv_w, conv_b, dt_bias, a_log, dskip, norm_w):
    pad = lambda x: jnp.pad(x, ((0, 0), (0, LANE - SSD_H)))
    par = jnp.concatenate([pad(dt_bias), pad(-jnp.exp(a_log)), jnp.zeros((4, LANE), F32)], 0)
    cw = jnp.concatenate([conv_w, jnp.zeros((5, SSD_XBC), F32)], 0)
    return (cw, conv_b.reshape(1, SSD_XBC), par, jnp.repeat(dskip, SSD_P).reshape(1, SSD_W),
            norm_w.reshape(1, SSD_W))


def _conf_kernel(v_ref, w_ref, b_ref, g_ref, bt_ref, o_ref, pad_scr, *, seq):
    q = CONF_Q
    pad_scr[0:CONF_PAD, :] = jnp.zeros((CONF_PAD, CONF_W), F32)
    pad_scr[seq + CONF_PAD:seq + 2 * CONF_PAD, :] = jnp.zeros((CONF_PAD, CONF_W), F32)
    pad_scr[CONF_PAD:seq + CONF_PAD, :] = v_ref[...]
    lo = CONF_PAD - CONF_K // 2

    def body(c, _):
        t0 = pl.multiple_of(c * q, q)
        big = pad_scr[pl.ds(t0, q + 2 * CONF_PAD), :]
        acc = jnp.zeros((q, CONF_W), F32) + b_ref[...]
        for k in range(CONF_K):
            acc = acc + big[lo + k:lo + k + q] * w_ref[k:k + 1, :]
        mu = jnp.mean(acc, axis=-1, keepdims=True)
        xc = acc - mu
        var = jnp.mean(xc * xc, axis=-1, keepdims=True)
        o_ref[pl.ds(t0, q), :] = jax.nn.silu(xc * lax.rsqrt(var + EPS) * g_ref[...] + bt_ref[...])
        return 0

    lax.fori_loop(0, seq // q, body, 0)


def _conf_call(v, w, nb, seq, row0):
    s0 = row0 // seq
    full = lambda a: pl.BlockSpec(a.shape, lambda b: (0,) * a.ndim)
    return pl.pallas_call(
        functools.partial(_conf_kernel, seq=seq), grid=(nb,),
        in_specs=[pl.BlockSpec((seq, CONF_W), lambda b: (s0 + b, 0))] + [full(a) for a in w],
        out_specs=pl.BlockSpec((seq, CONF_W), lambda b: (b, 0)),
        out_shape=jax.ShapeDtypeStruct((nb * seq, CONF_W), F32),
        scratch_shapes=[pltpu.VMEM((seq + 2 * CONF_PAD, CONF_W), F32)],
        compiler_params=_cparams("arbitrary"), name="conf",
    )(v, *w)


def _split_bf16(x):
    hi = x.astype(BF16)
    return hi, (x - hi.astype(F32)).astype(BF16)


def _outproj_kernel(x_ref, ada_ref, ys5_ref, ymla_ref, yssd_ref, yconf_ref, wglu_ref, bglu_ref, wo_ref,
                    npost_ref, npre_ref, wrh_ref, wrl_ref, br_ref, x1_ref, h2_ref, gate_ref):
    ada = ada_ref[0]
    a = jax.nn.gelu(ys5_ref[...])
    s5 = a * jax.nn.sigmoid(_bdot(a, wglu_ref[...]) + bglu_ref[...])
    mix = (_bdot(s5, wo_ref[0:256, :]) + _bdot(ymla_ref[...], wo_ref[256:512, :])
           + _bdot(yssd_ref[...], wo_ref[512:768, :]) + _bdot(yconf_ref[...], wo_ref[768:1024, :]))
    x1 = x_ref[...] + ada[2:3] * _rms(mix, npost_ref[...])
    x1_ref[...] = x1
    h2 = _rms(x1, npre_ref[...]) * (1.0 + ada[4:5]) + ada[3:4]
    h2_ref[...] = h2.astype(BF16)
    hi, lo = _split_bf16(h2)
    lg = (jnp.dot(hi, wrh_ref[...], preferred_element_type=F32) + jnp.dot(lo, wrh_ref[...], preferred_element_type=F32)
          + jnp.dot(hi, wrl_ref[...], preferred_element_type=F32) + br_ref[...])
    lane = lax.broadcasted_iota(jnp.int32, lg.shape, 1)
    neg = -jnp.inf
    big = jnp.int32(1 << 20)
    first = lambda hit: jnp.min(jnp.where(hit, lane, big), axis=-1, keepdims=True)
    glm = jnp.where((lane >= MOE_E) & (lane < MOE_E + MOE_GROUPS), lg, neg)
    gmax = jnp.max(glm, axis=-1, keepdims=True)
    p_group = 1.0 / jnp.sum(jnp.exp(glm - gmax), axis=-1, keepdims=True)
    gsel = first(glm == gmax) - MOE_E
    elm = jnp.where((lane < MOE_E) & ((lane // MOE_PER_GROUP) == gsel), lg, neg)
    v1 = jnp.max(elm, axis=-1, keepdims=True)
    i1 = first(elm == v1)
    elm2 = jnp.where(lane == i1, neg, elm)
    v2 = jnp.max(elm2, axis=-1, keepdims=True)
    i2 = first(elm2 == v2)
    e2 = jnp.exp(v2 - v1)
    w1 = p_group / (1.0 + e2)
    w2 = p_group * e2 / (1.0 + e2)
    gate_ref[...] = jnp.where(lane == i1, w1, 0.0) + jnp.where(lane == i2, w2, 0.0)


def _outproj_call(x, ada_t, ys5, ymla, yssd, yconf, w):
    t = x.shape[0]
    row = lambda n: pl.BlockSpec((TM, n), lambda i: (i, 0))
    full = lambda a: pl.BlockSpec(a.shape, lambda i: (0,) * a.ndim)
    return pl.pallas_call(
        _outproj_kernel, grid=(t // TM,),
        in_specs=[row(D_MODEL), pl.BlockSpec((1, 8, D_MODEL), lambda i: (i, 0, 0)), row(256), row(256), row(256),
                  row(256)] + [full(a) for a in w],
        out_specs=[row(D_MODEL), row(D_MODEL), row(LANE)],
        out_shape=[jax.ShapeDtypeStruct((t, D_MODEL), F32), jax.ShapeDtypeStruct((t, D_MODEL), BF16),
                   jax.ShapeDtypeStruct((t, LANE), F32)],
        compiler_params=_cparams("arbitrary"), name="outproj",
    )(x, ada_t, ys5, ymla, yssd, yconf, *w)


def _moe_kernel(h_ref, gate_ref, x1_ref, ada_ref, wgu_ref, wd_ref, npost_ref, o_ref, acc_ref):
    e = pl.program_id(1)

    @pl.when(e == 0)
    def _zero():
        acc_ref[...] = jnp.zeros_like(acc_ref)

    a = jnp.dot(h_ref[...], wgu_ref[0], preferred_element_type=F32)
    lane = lax.broadcasted_iota(jnp.int32, gate_ref.shape, 1)
    ge = jnp.sum(jnp.where(lane == e, gate_ref[...], 0.0), axis=-1, keepdims=True)
    hid = jax.nn.silu(a[:, :MOE_HID]) * a[:, MOE_HID:] * ge
    acc_ref[...] += jnp.dot(hid.astype(BF16), wd_ref[0], preferred_element_type=F32)

    @pl.when(e == MOE_E - 1)
    def _finish():
        o_ref[...] = x1_ref[...] + ada_ref[0][5:6] * _rms(acc_ref[...], npost_ref[...])


def _moe_call(h2, gate, x1, ada_t, wgu, wd, npost):
    t = h2.shape[0]
    tm = MOE_TM
    row = lambda n: pl.BlockSpec((tm, n), lambda i, e: (i, 0))
    return pl.pallas_call(
        _moe_kernel, grid=(t // tm, MOE_E),
        in_specs=[row(D_MODEL), row(LANE), row(D_MODEL),
                  pl.BlockSpec((1, 8, D_MODEL), lambda i, e: (i * (tm // TM), 0, 0)),
                  pl.BlockSpec((1, D_MODEL, 2 * MOE_HID), lambda i, e: (e, 0, 0)),
                  pl.BlockSpec((1, MOE_HID, D_MODEL), lambda i, e: (e, 0, 0)),
                  pl.BlockSpec(npost.shape, lambda i, e: (0, 0))],
        out_specs=row(D_MODEL),
        out_shape=jax.ShapeDtypeStruct((t, D_MODEL), F32),
        scratch_shapes=[pltpu.VMEM((tm, D_MODEL), F32)],
        compiler_params=_cparams("arbitrary", "arbitrary"), name="moe",
    )(h2, gate, x1, ada_t, wgu, wd, npost)


def _pack_w_in(w):
    d = w.shape[0]
    z32, z64 = jnp.zeros((d, 32), F32), jnp.zeros((d, 64), F32)
    kr = w[:, OFF_MLA_KR:OFF_SSD_Z]
    dtw = jnp.pad(w[:, OFF_SSD_DT:OFF_CONF], ((0, 0), (0, LANE - SSD_H)))
    cols = [w[:, OFF_S5:OFF_MLA_Q], w[:, OFF_MLA_Q:OFF_MLA_KV], w[:, OFF_MLA_KV:OFF_MLA_KR],
            z64, kr, z32, z64, _swap8(kr), z32,
            w[:, OFF_SSD_Z:OFF_SSD_XBC], w[:, OFF_SSD_XBC:OFF_SSD_DT], dtw, w[:, OFF_CONF:IN_COLS]]
    return jnp.concatenate(cols, -1).astype(BF16)


def _row(v):
    return v.reshape(1, -1)


def kernel(x_prompt, x_sample, cache_mla_ckv, cache_mla_krope, state_s5, state_ssd, c, c_ctx, w_ada, b_ada, norm_pre1, norm_post1, norm_pre2, norm_post2, w_in, w_out, s5_lam_re, s5_lam_im, s5_log_dt, s5_b_re, s5_b_im, s5_c_re, s5_c_im, s5_d, s5_w_glu, s5_b_glu, mla_qnorm, mla_kvnorm, mla_wuq, mla_wuk, mla_wuv, ssd_conv_w, ssd_conv_b, ssd_dt_bias, ssd_a_log, ssd_d, ssd_norm_w, conf_dw_w, conf_dw_b, conf_ln_g, conf_ln_b, moe_wg, moe_bg, moe_we, moe_be, moe_w_gate, moe_w_up, moe_w_down):
    bp, lp, d = x_prompt.shape
    bs, ls, _ = x_sample.shape
    tp, ts = bp * lp, bs * ls
    x = jnp.concatenate([x_prompt.reshape(tp, d), x_sample.reshape(ts, d)], 0)

    crows = jnp.concatenate([c_ctx[None], c, jnp.zeros((8 - 1 - bs, d), F32)], 0)
    ada = _ada_call(crows, w_ada, b_ada)
    tile_row = np.concatenate([np.zeros(tp // TM, np.int32), 1 + np.repeat(np.arange(bs, dtype=np.int32), ls // TM)])

    rope_p = _rope_tables(lp, rotate=False)
    rope_s = _rope_tables(ls, rotate=True)
    zeros_s5 = tuple(jnp.zeros((bp, S5_G * 2 * S5_P), F32) for _ in range(4))
    zeros_ssd = jnp.zeros((bp, 2, SSD_H * SSD_P, SSD_N), F32)
    pad_kr = lambda kr: jnp.pad(kr, ((0, 0), (0, 0), (MLA_NOPE, LANE - MLA_NOPE - MLA_ROPE)))

    ckv_l, kr_l, s5_l, ssd_l = [], [], [], []
    for l in range(DEPTH):
        ada_t = jnp.pad(ada[l][tile_row].reshape(-1, 6, d), ((0, 0), (0, 2), (0, 0)))
        u, cq, ckv, krab, z, xbc, dt, v = _inproj_call(
            x, ada_t, _row(norm_pre1[l]), _pack_w_in(w_in[l]), _row(mla_qnorm[l]), _row(mla_kvnorm[l]))

        tabs = _s5_tables(s5_lam_re[l], s5_lam_im[l], s5_log_dt[l], s5_b_re[l], s5_b_im[l], s5_c_re[l], s5_c_im[l],
                          s5_d[l])
        yp, stf, stb = _s5_call(_s5_layout_in(u[:tp], bp, lp), tabs, zeros_s5, bp)
        ys, _, _ = _s5_call(_s5_layout_in(u[tp:], bs, ls), tabs, _s5_state_in(state_s5[:, l]), bs)
        y_s5 = jnp.concatenate([_s5_layout_out(yp, bp, lp), _s5_layout_out(ys, bs, ls)], 0)
        s5_l.append(_s5_state_out(stf, stb))

        mw = _mla_weights(mla_wuq[l], mla_wuk[l], mla_wuv[l])
        ymla_p = _mla_call(cq, ckv, krab, rope_p, None, mw, bp, lp, 0)
        ymla_s = _mla_call(cq, ckv, krab, rope_s, (cache_mla_ckv[:, l], pad_kr(cache_mla_krope[:, l])), mw, bs, ls, tp)
        y_mla = jnp.concatenate([ymla_p, ymla_s], 0)
        ckv_l.append(ckv[:tp].reshape(bp, lp, MLA_KVLORA))
        kr_l.append(krab[:tp, MLA_NOPE:MLA_NOPE + MLA_ROPE].reshape(bp, lp, MLA_ROPE))

        sw = _ssd_weights(ssd_conv_w[l], ssd_conv_b[l], ssd_dt_bias[l], ssd_a_log[l], ssd_d[l], ssd_norm_w[l])
        yssd_p, st_ssd = _ssd_call(xbc, z, dt, sw, zeros_ssd, bp, lp, 0)
        yssd_s, _ = _ssd_call(xbc, z, dt, sw, state_ssd[:, l].reshape(bs, 2, SSD_H * SSD_P, SSD_N), bs, ls, tp)
        y_ssd = jnp.concatenate([yssd_p, yssd_s], 0)
        ssd_l.append(st_ssd.reshape(bp, 2, SSD_H, SSD_P, SSD_N))

        cw = (jnp.concatenate([conf_dw_w[l], jnp.zeros((1, CONF_W), F32)], 0), _row(conf_dw_b[l]),
              _row(conf_ln_g[l]), _row(conf_ln_b[l]))
        y_conf = jnp.concatenate([_conf_call(v, cw, bp, lp, 0), _conf_call(v, cw, bs, ls, tp)], 0)

        wr = jnp.pad(jnp.concatenate([moe_we[l], moe_wg[l]], -1), ((0, 0), (0, LANE - MOE_E - MOE_GROUPS)))
        wrh, wrl = _split_bf16(wr)
        br = jnp.pad(jnp.concatenate([moe_be[l], moe_bg[l]]), (0, LANE - MOE_E - MOE_GROUPS)).reshape(1, LANE)
        ow = (s5_w_glu[l].astype(BF16), _row(s5_b_glu[l]), w_out[l].astype(BF16), _row(norm_post1[l]),
              _row(norm_pre2[l]), wrh, wrl, br)
        x1, h2, gate = _outproj_call(x, ada_t, y_s5, y_mla, y_ssd, y_conf, ow)

        wgu = jnp.concatenate([moe_w_gate[l], moe_w_up[l]], -1).astype(BF16)
        x = _moe_call(h2, gate, x1, ada_t, wgu, moe_w_down[l].astype(BF16), _row(norm_post2[l]))

    return (x[:tp].reshape(bp, lp, d), x[tp:].reshape(bs, ls, d),
            jnp.stack(ckv_l, 1), jnp.stack(kr_l, 1), jnp.stack(s5_l, 1), jnp.stack(ssd_l, 1))
```

```python
import functools
import math

import jax
import jax.numpy as jnp
import numpy as np
from jax import lax
from jax.experimental import pallas as pl
from jax.experimental.pallas import tpu as pltpu

F32 = jnp.float32
BF16 = jnp.bfloat16

D_MODEL = 1024
DEPTH = 2
GRID_W = 64
EPS = 1e-6

S5_W = 256
S5_GSIZE = 16
S5_G = 16
S5_P = 64
S5_CHUNK = 16

MLA_H = 4
MLA_NOPE = 64
MLA_ROPE = 32
MLA_V = 64
MLA_QLORA = 256
MLA_KVLORA = 128
ROPE_BASE = 10000.0
MLA_TQ = 256
MLA_KC = 256

SSD_W = 256
SSD_P = 64
SSD_H = 4
SSD_G = 2
SSD_N = 64
SSD_CHUNK = 128
SSD_XBC = 512

CONF_W = 256
CONF_K = 31
CONF_Q = 64
CONF_PAD = 16

MOE_GROUPS = 4
MOE_PER_GROUP = 4
MOE_E = 16
MOE_HID = 256
MOE_TM = 512

OFF_S5 = 0
OFF_MLA_Q = 256
OFF_MLA_KV = 512
OFF_MLA_KR = 640
OFF_SSD_Z = 672
OFF_SSD_XBC = 928
OFF_SSD_DT = 1440
OFF_CONF = 1444
IN_COLS = 1956

LANE = 128
SUBLANE = 8
TM = 256
P_U, P_Q, P_KV, P_KRA, P_KRB, P_Z, P_XBC, P_DT, P_CA, P_CB, P_END = (
    0, 256, 512, 640, 768, 896, 1152, 1664, 1792, 2048, 2304)

VMEM_LIMIT = 56 * 1024 * 1024


def _bdot(a, b):
    return jnp.dot(a.astype(BF16), b.astype(BF16), preferred_element_type=F32)


def _bdot_nt(a, b):
    return lax.dot_general(a.astype(BF16), b.astype(BF16), (((1,), (1,)), ((), ())), preferred_element_type=F32)


def _rms(x, g):
    return x * lax.rsqrt(jnp.mean(x * x, axis=-1, keepdims=True) + EPS) * g


def _cparams(*sem):
    return pltpu.CompilerParams(dimension_semantics=sem if sem else None, vmem_limit_bytes=VMEM_LIMIT)


def _ada_kernel(c_ref, w_ref, b_ref, o_ref):
    o_ref[0] = _bdot(jax.nn.silu(c_ref[...]), w_ref[0]) + b_ref[0]


def _ada_call(crows, w_ada, b_ada):
    tn = 1024
    d6 = 6 * D_MODEL
    return pl.pallas_call(
        _ada_kernel, grid=(DEPTH, d6 // tn),
        in_specs=[pl.BlockSpec((8, D_MODEL), lambda l, j: (0, 0)),
                  pl.BlockSpec((1, D_MODEL, tn), lambda l, j: (l, 0, j)),
                  pl.BlockSpec((1, 1, tn), lambda l, j: (l, 0, j))],
        out_specs=pl.BlockSpec((1, 8, tn), lambda l, j: (l, 0, j)),
        out_shape=jax.ShapeDtypeStruct((DEPTH, 8, d6), F32),
        compiler_params=_cparams("arbitrary", "arbitrary"), name="ada",
    )(crows, w_ada, b_ada.reshape(DEPTH, 1, d6))


def _inproj_kernel(x_ref, ada_ref, npre_ref, w_ref, qn_ref, kvn_ref,
                   u_ref, cq_ref, ckv_ref, krab_ref, z_ref, xbc_ref, dt_ref, v_ref):
    ada = ada_ref[0]
    h = _rms(x_ref[...], npre_ref[...]) * (1.0 + ada[1:2]) + ada[0:1]
    p = _bdot(h, w_ref[...])
    u_ref[...] = p[:, P_U:P_Q]
    cq_ref[...] = _rms(p[:, P_Q:P_KV], qn_ref[...]).astype(BF16)
    ckv_ref[...] = _rms(p[:, P_KV:P_KRA], kvn_ref[...])
    krab_ref[...] = p[:, P_KRA:P_Z]
    z_ref[...] = p[:, P_Z:P_XBC]
    xbc_ref[...] = p[:, P_XBC:P_DT]
    dt_ref[...] = p[:, P_DT:P_CA]
    v_ref[...] = p[:, P_CA:P_CB] * jax.nn.sigmoid(p[:, P_CB:P_END])


def _inproj_call(x, ada_t, npre, w, qn, kvn):
    t = x.shape[0]
    row = lambda n: pl.BlockSpec((TM, n), lambda i: (i, 0))
    full = lambda a: pl.BlockSpec(a.shape, lambda i: (0,) * a.ndim)
    widths = (256, 256, 128, 256, 256, 512, 128, 256)
    dts = (F32, BF16, F32, F32, F32, F32, F32, F32)
    return pl.pallas_call(
        _inproj_kernel, grid=(t // TM,),
        in_specs=[row(D_MODEL), pl.BlockSpec((1, 8, D_MODEL), lambda i: (i, 0, 0)), full(npre), full(w), full(qn),
                  full(kvn)],
        out_specs=[row(n) for n in widths],
        out_shape=[jax.ShapeDtypeStruct((t, n), d) for n, d in zip(widths, dts)],
        compiler_params=_cparams("arbitrary"), name="inproj",
    )(x, ada_t, npre, w, qn, kvn)


def _s5_kernel(u_ref, r_ref, vf_ref, vb_ref, a_ref, h0f_ref, h0fs_ref, h0b_ref, h0bs_ref,
               y_ref, stf_ref, stb_ref, sf, sfs, sb, sbs, *, nb, nc):
    gw = 2 * S5_P
    for g in range(S5_G):
        r = _bdot(u_ref[g], r_ref[g])
        y_ref[g] = r[:, 0:256]
        sl = slice(g * gw, (g + 1) * gw)
        sf[:, sl] = r[:, 256:384]
        sfs[:, sl] = r[:, 384:512]
        sb[:, sl] = r[:, 512:640]
        sbs[:, sl] = r[:, 640:768]
    arf, aif, aisf = a_ref[0:1], a_ref[1:2], a_ref[2:3]
    arb, aib, aisb = a_ref[3:4], a_ref[4:5], a_ref[5:6]
    cpt = 1 if nb % SUBLANE == 0 else SUBLANE // nb
    rows = cpt * nb
    nt = nc // cpt

    def body(i, carry):
        hf, hfs, hb, hbs = carry
        rf = pl.ds(pl.multiple_of(i * rows, rows), rows)
        rb = pl.ds(pl.multiple_of((nt - 1 - i) * rows, rows), rows)
        inc_f, inc_fs, inc_b, inc_bs = sf[rf, :], sfs[rf, :], sb[rb, :], sbs[rb, :]
        in_f, in_b = [None] * cpt, [None] * cpt
        for k in range(cpt):
            kf = slice(k * nb, (k + 1) * nb)
            kb = slice((cpt - 1 - k) * nb, (cpt - k) * nb)
            in_f[k] = hf
            in_b[cpt - 1 - k] = hb
            hf, hfs = arf * hf + aif * hfs + inc_f[kf], arf * hfs + aisf * hf + inc_fs[kf]
            hb, hbs = arb * hb + aib * hbs + inc_b[kb], arb * hbs + aisb * hb + inc_bs[kb]
        sf[rf, :] = in_f[0] if cpt == 1 else jnp.concatenate(in_f, 0)
        sb[rb, :] = in_b[0] if cpt == 1 else jnp.concatenate(in_b, 0)
        return hf, hfs, hb, hbs

    hf, _, hb, _ = lax.fori_loop(0, nt, body, (h0f_ref[...], h0fs_ref[...], h0b_ref[...], h0bs_ref[...]))
    stf_ref[...] = hf
    stb_ref[...] = hb
    for g in range(S5_G):
        sl = slice(g * gw, (g + 1) * gw)
        y_ref[g] += _bdot(sf[:, sl], vf_ref[g]) + _bdot(sb[:, sl], vb_ref[g])


def _s5_call(u, tabs, h0, nb):
    n = u.shape[1]
    r, vf, vb, avec = tabs
    sw = S5_G * 2 * S5_P
    st = jax.ShapeDtypeStruct((nb, sw), F32)
    return pl.pallas_call(
        functools.partial(_s5_kernel, nb=nb, nc=n // nb),
        out_shape=[jax.ShapeDtypeStruct(u.shape, F32), st, st],
        scratch_shapes=[pltpu.VMEM((n, sw), F32) for _ in range(4)],
        compiler_params=_cparams(), name="s5",
    )(u, r, vf, vb, avec, *h0)


def _s5_tables(lam_re, lam_im, log_dt, b_re, b_im, c_re, c_im, dskip):
    q = S5_CHUNK
    hp = lax.Precision.HIGHEST
    dt = jnp.exp(log_dt)[..., None]
    ld_re, ld_im = lam_re * dt, lam_im * dt
    j = jnp.arange(q + 1, dtype=F32)[:, None, None, None]
    mag = jnp.exp(j * ld_re)
    pw_re, pw_im = mag * jnp.cos(j * ld_im), mag * jnp.sin(j * ld_im)
    abr, abi = pw_re[1], pw_im[1]
    den = lam_re * lam_re + lam_im * lam_im
    fr = ((abr - 1.0) * lam_re + abi * lam_im) / den
    fi = (abi * lam_re - (abr - 1.0) * lam_im) / den
    bbr = fr[..., None] * b_re - fi[..., None] * b_im
    bbi = fr[..., None] * b_im + fi[..., None] * b_re
    cbr = jnp.einsum('dgop,dgpi->dgopi', c_re, bbr) - jnp.einsum('dgop,dgpi->dgopi', c_im, bbi)
    cbi = jnp.einsum('dgop,dgpi->dgopi', c_re, bbi) + jnp.einsum('dgop,dgpi->dgopi', c_im, bbr)
    kk = (jnp.einsum('jdgp,dgopi->dgjoi', pw_re[:q], cbr, precision=hp)
          - jnp.einsum('jdgp,dgopi->dgjoi', pw_im[:q], cbi, precision=hp))
    s_idx = jnp.arange(q)[:, None]
    t_idx = jnp.arange(q)[None, :]
    lag_f = jnp.clip(t_idx - s_idx, 0, q - 1)
    lag_b = jnp.clip(s_idx - t_idx, 0, q - 1)
    tf = jnp.where((t_idx >= s_idx)[None, :, :, None, None], kk[0][:, lag_f], 0.0)
    tb = jnp.where((s_idx >= t_idx)[None, :, :, None, None], kk[1][:, lag_b], 0.0)
    eye_t = jnp.eye(q, dtype=F32)[None, :, :, None, None]
    eye_c = jnp.eye(S5_GSIZE, dtype=F32)[None, None, None, :, :]
    tsk = eye_t * eye_c * dskip.reshape(S5_G, 1, 1, S5_GSIZE, 1)
    tt = jnp.transpose(tf + tb + tsk, (0, 1, 4, 2, 3)).reshape(S5_G, 256, 256)

    def state_tab(d, powers):
        pr = pw_re[powers, d]
        pi = pw_im[powers, d]
        wr = pr[..., None] * bbr[d][None] - pi[..., None] * bbi[d][None]
        wi = pr[..., None] * bbi[d][None] + pi[..., None] * bbr[d][None]
        wr = jnp.transpose(wr, (1, 0, 3, 2)).reshape(S5_G, 256, S5_P)
        wi = jnp.transpose(wi, (1, 0, 3, 2)).reshape(S5_G, 256, S5_P)
        return jnp.concatenate([wr, wi], -1), jnp.concatenate([wi, wr], -1)

    wf, wfs = state_tab(0, q - 1 - jnp.arange(q))
    wb, wbs = state_tab(1, jnp.arange(q))
    rtab = jnp.concatenate([tt, wf, wfs, wb, wbs], -1).astype(BF16)

    def out_tab(d, powers):
        pr = pw_re[powers, d]
        pi = pw_im[powers, d]
        vr = c_re[d][None] * pr[:, :, None, :] - c_im[d][None] * pi[:, :, None, :]
        vi = c_re[d][None] * pi[:, :, None, :] + c_im[d][None] * pr[:, :, None, :]
        vr = jnp.transpose(vr, (1, 3, 0, 2)).reshape(S5_G, S5_P, 256)
        vi = jnp.transpose(vi, (1, 3, 0, 2)).reshape(S5_G, S5_P, 256)
        return jnp.concatenate([vr, -vi], 1).astype(BF16)

    vf = out_tab(0, 1 + jnp.arange(q))
    vb = out_tab(1, q - jnp.arange(q))

    def lanes(x_first, x_second):
        return jnp.concatenate([x_first, x_second], -1).reshape(1, S5_G * 2 * S5_P)

    ar, ai = pw_re[q], pw_im[q]
    rows = []
    for d in range(2):
        rows += [lanes(ar[d], ar[d]), lanes(-ai[d], ai[d]), lanes(ai[d], -ai[d])]
    avec = jnp.concatenate(rows + [jnp.zeros((2, S5_G * 2 * S5_P), F32)], 0)
    return rtab, vf, vb, avec


def _s5_layout_in(u, nb, seq):
    nc = seq // S5_CHUNK
    return jnp.transpose(u.reshape(nb, nc, S5_CHUNK, S5_G, S5_GSIZE), (3, 1, 0, 2, 4)).reshape(S5_G, nc * nb, 256)


def _s5_layout_out(y, nb, seq):
    nc = seq // S5_CHUNK
    return jnp.transpose(y.reshape(S5_G, nc, nb, S5_CHUNK, S5_GSIZE), (2, 1, 3, 0, 4)).reshape(nb * seq, S5_W)


def _s5_state_in(state):
    b = state.shape[0]
    sw = jnp.transpose(state, (0, 1, 2, 4, 3))
    plain = sw.reshape(b, 2, S5_G * 2 * S5_P)
    swapped = sw[:, :, :, ::-1, :].reshape(b, 2, S5_G * 2 * S5_P)
    return plain[:, 0], swapped[:, 0], plain[:, 1], swapped[:, 1]


def _s5_state_out(stf, stb):
    b = stf.shape[0]
    st = jnp.stack([stf, stb], 1).reshape(b, 2, S5_G, 2, S5_P)
    return jnp.transpose(st, (0, 1, 2, 4, 3))


def _mla_kernel(*refs, seq, lctx):
    if lctx:
        (cq_ref, ckv_ref, krab_ref, cosq_ref, sinq_ref, cosk_ref, sink_ref, cckv_ref, ckr_ref,
         wqa_ref, wqb_ref, wk_ref, wv_ref, o_ref, k_scr, v_scr, s_scr) = refs
    else:
        (cq_ref, ckv_ref, krab_ref, cosq_ref, sinq_ref, cosk_ref, sink_ref,
         wqa_ref, wqb_ref, wk_ref, wv_ref, o_ref, k_scr, v_scr, s_scr) = refs
    lk = lctx + seq
    rb = min(seq, 512)

    @pl.when(pl.program_id(1) == 0)
    def _build_keys():
        if lctx:
            ckv = cckv_ref[0]
            kn = _bdot(ckv, wk_ref[...])
            vn = _bdot(ckv, wv_ref[...])
            for h in range(MLA_H):
                k_scr[h, 0:lctx, :] = (kn[:, h * LANE:(h + 1) * LANE] + ckr_ref[0]).astype(BF16)
                v_scr[h, 0:lctx, :] = vn[:, h * LANE:(h + 1) * LANE].astype(BF16)

        def chunk(i, _):
            r0 = pl.multiple_of(i * rb, rb)
            rows = pl.ds(r0, rb)
            ckv = ckv_ref[rows, :]
            kn = _bdot(ckv, wk_ref[...])
            vn = _bdot(ckv, wv_ref[...])
            krab = krab_ref[rows, :]
            krx = krab[:, :LANE] * cosk_ref[rows, :] + krab[:, LANE:] * sink_ref[rows, :]
            dst = pl.ds(pl.multiple_of(lctx + r0, rb if lctx % rb == 0 else 256), rb)
            for h in range(MLA_H):
                k_scr[h, dst, :] = (kn[:, h * LANE:(h + 1) * LANE] + krx).astype(BF16)
                v_scr[h, dst, :] = vn[:, h * LANE:(h + 1) * LANE].astype(BF16)
            return 0

        lax.fori_loop(0, seq // rb, chunk, 0)

    cq = cq_ref[...]
    qa = _bdot(cq, wqa_ref[...])
    qb = _bdot(cq, wqb_ref[...])
    cosq, sinq = cosq_ref[...], sinq_ref[...]
    scale = 1.0 / math.sqrt(MLA_NOPE + MLA_ROPE)
    outs = []
    chunks = [slice(j * MLA_KC, (j + 1) * MLA_KC) for j in range(lk // MLA_KC)]
    for h in range(MLA_H):
        hs = slice(h * LANE, (h + 1) * LANE)
        qh = ((qa[:, hs] * cosq + qb[:, hs] * sinq) * scale).astype(BF16)
        mp = None
        for ks in chunks:
            s = _bdot_nt(qh, k_scr[h, ks, :])
            s_scr[:, ks] = s
            for t in range(MLA_KC // LANE):
                part = s[:, t * LANE:(t + 1) * LANE]
                mp = part if mp is None else jnp.maximum(mp, part)
        m = jnp.max(mp, axis=-1, keepdims=True)
        lp = jnp.zeros((MLA_TQ, LANE), F32)
        acc = jnp.zeros((MLA_TQ, LANE), F32)
        for ks in chunks:
            e = jnp.exp(s_scr[:, ks] - m)
            for t in range(MLA_KC // LANE):
                lp = lp + e[:, t * LANE:(t + 1) * LANE]
            acc = acc + _bdot(e, v_scr[h, ks, :])
        outs.append((acc / jnp.sum(lp, axis=-1, keepdims=True))[:, :MLA_V])
    o_ref[...] = jnp.concatenate(outs, axis=-1)


def _mla_call(cq, ckv, krab, rope, ctx, w, nb, seq, row0):
    lctx = 0 if ctx is None else ctx[0].shape[1]
    nq = seq // MLA_TQ
    q0, s0 = row0 // MLA_TQ, row0 // seq
    cosx, sinx = rope
    qrow = lambda n: pl.BlockSpec((MLA_TQ, n), lambda b, i: (q0 + b * nq + i, 0))
    srow = lambda n: pl.BlockSpec((seq, n), lambda b, i: (s0 + b, 0))
    full = lambda a: pl.BlockSpec(a.shape, lambda b, i: (0,) * a.ndim)
    in_specs = [qrow(256), srow(128), srow(256), pl.BlockSpec((MLA_TQ, LANE), lambda b, i: (i, 0)),
                pl.BlockSpec((MLA_TQ, LANE), lambda b, i: (i, 0)), full(cosx), full(sinx)]
    args = [cq, ckv, krab, cosx, sinx, cosx, sinx]
    if lctx:
        in_specs += [pl.BlockSpec((1, lctx, LANE), lambda b, i: (b, 0, 0))] * 2
        args += list(ctx)
    in_specs += [full(a) for a in w]
    args += list(w)
    lk = lctx + seq
    return pl.pallas_call(
        functools.partial(_mla_kernel, seq=seq, lctx=lctx), grid=(nb, nq),
        in_specs=in_specs,
        out_specs=pl.BlockSpec((MLA_TQ, 256), lambda b, i: (b * nq + i, 0)),
        out_shape=jax.ShapeDtypeStruct((nb * seq, 256), F32),
        scratch_shapes=[pltpu.VMEM((MLA_H, lk, LANE), BF16), pltpu.VMEM((MLA_H, lk, LANE), BF16),
                        pltpu.VMEM((MLA_TQ, lk), F32)],
        compiler_params=_cparams("arbitrary", "arbitrary"), name="mla",
    )(*args)


def _swap8(w):
    s = w.shape
    return w.reshape(s[:-1] + (s[-1] // 16, 2, 8))[..., ::-1, :].reshape(s)


def _mla_weights(wuq, wuk, wuv):
    z32 = jnp.zeros((MLA_QLORA, 32), F32)
    z64 = jnp.zeros((MLA_QLORA, 64), F32)
    k64 = jnp.zeros((MLA_KVLORA, 64), F32)
    qa, qb, wk, wv = [], [], [], []
    for h in range(MLA_H):
        rope_w = wuq[:, h, MLA_NOPE:]
        qa += [wuq[:, h, :MLA_NOPE], rope_w, z32]
        qb += [z64, _swap8(rope_w), z32]
        wk += [wuk[:, h], k64]
        wv += [wuv[:, h], k64]
    cat = lambda xs: jnp.concatenate(xs, -1).astype(BF16)
    return cat(qa), cat(qb), cat(wk), cat(wv)


def _rope_tables(seq, rotate):
    ones = jnp.ones((seq, MLA_NOPE), F32)
    z32 = jnp.zeros((seq, 32), F32)
    if not rotate:
        return (jnp.concatenate([ones, jnp.ones((seq, MLA_ROPE), F32), z32], -1), jnp.zeros((seq, LANE), F32))
    pos = jnp.arange(seq)
    row = (pos // GRID_W).astype(F32)
    col = (pos % GRID_W).astype(F32)
    half = MLA_ROPE // 2
    inv = ROPE_BASE ** (-jnp.arange(0, half, 2, dtype=F32) / half)
    ang = jnp.concatenate([row[:, None] * inv, col[:, None] * inv], -1)
    cos, sin = jnp.cos(ang), jnp.sin(ang)
    q = MLA_ROPE // 4
    cos32 = jnp.concatenate([cos[:, :q], cos[:, :q], cos[:, q:], cos[:, q:]], -1)
    sin32 = jnp.concatenate([-sin[:, :q], sin[:, :q], -sin[:, q:], sin[:, q:]], -1)
    return jnp.concatenate([ones, cos32, z32], -1), jnp.concatenate([jnp.zeros_like(ones), sin32, z32], -1)


def _cumsum_rows(x, reverse):
    n = x.shape[0]
    row = lax.broadcasted_iota(jnp.int32, x.shape, 0)
    s = 1
    while s < n:
        if reverse:
            x = x + jnp.where(row < n - s, pltpu.roll(x, n - s, 0), 0.0)
        else:
            x = x + jnp.where(row >= s, pltpu.roll(x, s, 0), 0.0)
        s *= 2
    return x


def _ssd_kernel(xbc_ref, z_ref, dt_ref, cw_ref, cb_ref, par_ref, dsk_ref, nw_ref, h0_ref,
                y_ref, st_ref, xc_scr, yf_scr, yb_scr, hf_scr, hb_scr, *, seq):
    q = SSD_CHUNK
    nc = seq // q
    row = lax.broadcasted_iota(jnp.int32, (q, 1), 0)

    def conv(c, _):
        t0 = pl.multiple_of(c * q, q)
        cur = xbc_ref[pl.ds(t0, q), :]
        prev8 = xbc_ref[pl.ds(pl.multiple_of(jnp.maximum(t0 - 8, 0), 8), 8), :]
        next8 = xbc_ref[pl.ds(pl.multiple_of(jnp.minimum(t0 + q, seq - 8), 8), 8), :]
        prow = jnp.where(c > 0, prev8[7:8], 0.0)
        nrow = jnp.where(c < nc - 1, next8[0:1], 0.0)
        up = jnp.where(row == 0, prow, pltpu.roll(cur, 1, 0))
        dn = jnp.where(row == q - 1, nrow, pltpu.roll(cur, q - 1, 0))
        acc = up * cw_ref[0:1] + cur * cw_ref[1:2] + dn * cw_ref[2:3] + cb_ref[...]
        xc_scr[pl.ds(t0, q), :] = jax.nn.silu(acc)
        return 0

    lax.fori_loop(0, nc, conv, 0)
    hf_scr[...] = h0_ref[0, 0]
    hb_scr[...] = h0_ref[0, 1]
    ii = lax.broadcasted_iota(jnp.int32, (q, q), 0)
    jj = lax.broadcasted_iota(jnp.int32, (q, q), 1)

    def one_chunk(c, d, h_scr, y_scr):
        t0 = pl.multiple_of(c * q, q)
        xc = xc_scr[pl.ds(t0, q), :]
        xs = xc[:, :SSD_W]
        xs_t = jnp.transpose(xs)
        dt = jax.nn.softplus(dt_ref[pl.ds(t0, q), :] + par_ref[d:d + 1])
        dta = dt * par_ref[2 + d:3 + d]
        cs = _cumsum_rows(dta, reverse=(d == 1))
        cs_t = jnp.transpose(cs)
        dt_t = jnp.transpose(dt)
        edge = cs[q - 1:q] if d == 0 else cs[0:1]
        wts = jnp.exp(edge - cs) * dt
        ecs = jnp.exp(cs)
        mask = (ii >= jj) if d == 0 else (ii <= jj)
        for g in range(SSD_G):
            bm = xc[:, SSD_W + g * SSD_N:SSD_W + (g + 1) * SSD_N]
            cm = xc[:, SSD_W + SSD_G * SSD_N + g * SSD_N:SSD_W + SSD_G * SSD_N + (g + 1) * SSD_N]
            cbm = _bdot_nt(cm, bm)
            for hh in range(SSD_H // SSD_G):
                h = g * (SSD_H // SSD_G) + hh
                ps = slice(h * SSD_P, (h + 1) * SSD_P)
                lmat = jnp.exp(jnp.where(mask, cs[:, h:h + 1] - cs_t[h:h + 1, :], -jnp.inf))
                y = _bdot(cbm * lmat * dt_t[h:h + 1, :], xs[:, ps])
                h_in = h_scr[ps, :]
                y = y + _bdot_nt(cm, h_in) * ecs[:, h:h + 1]
                y_scr[pl.ds(t0, q), ps] = y
                st = _bdot(xs_t[ps, :], bm * wts[:, h:h + 1])
                h_scr[ps, :] = jnp.exp(edge[:, h:h + 1]) * h_in + st

    def chunks(c, _):
        one_chunk(c, 0, hf_scr, yf_scr)
        one_chunk(nc - 1 - c, 1, hb_scr, yb_scr)
        return 0

    lax.fori_loop(0, nc, chunks, 0)
    st_ref[0, 0] = hf_scr[...]
    st_ref[0, 1] = hb_scr[...]

    def finish(c, _):
        rows = pl.ds(pl.multiple_of(c * q, q), q)
        y = yf_scr[rows, :] + yb_scr[rows, :] + dsk_ref[...] * xc_scr[rows, 0:SSD_W]
        y_ref[rows, :] = _rms(y * jax.nn.silu(z_ref[rows, :]), nw_ref[...])
        return 0

    lax.fori_loop(0, nc, finish, 0)


def _ssd_call(xbc, z, dt, w, h0, nb, seq, row0):
    s0 = row0 // seq
    srow = lambda n: pl.BlockSpec((seq, n), lambda b: (s0 + b, 0))
    full = lambda a: pl.BlockSpec(a.shape, lambda b: (0,) * a.ndim)
    hp = SSD_H * SSD_P
    st_spec = pl.BlockSpec((1, 2, hp, SSD_N), lambda b: (b, 0, 0, 0))
    return pl.pallas_call(
        functools.partial(_ssd_kernel, seq=seq), grid=(nb,),
        in_specs=[srow(SSD_XBC), srow(SSD_W), srow(LANE)] + [full(a) for a in w] + [st_spec],
        out_specs=[pl.BlockSpec((seq, SSD_W), lambda b: (b, 0)), st_spec],
        out_shape=[jax.ShapeDtypeStruct((nb * seq, SSD_W), F32), jax.ShapeDtypeStruct((nb, 2, hp, SSD_N), F32)],
        scratch_shapes=[pltpu.VMEM((seq, SSD_XBC), F32), pltpu.VMEM((seq, SSD_W), F32), pltpu.VMEM((seq, SSD_W), F32),
                        pltpu.VMEM((hp, SSD_N), F32), pltpu.VMEM((hp, SSD_N), F32)],
        compiler_params=_cparams("arbitrary"), name="ssd",
    )(xbc, z, dt, *w, h0)


def _ssd_weights(conv_w, conv_b, dt_bias, a_log, dskip, norm_w):
    pad = lambda x: jnp.pad(x, ((0, 0), (0, LANE - SSD_H)))
    par = jnp.concatenate([pad(dt_bias), pad(-jnp.exp(a_log)), jnp.zeros((4, LANE), F32)], 0)
    cw = jnp.concatenate([conv_w, jnp.zeros((5, SSD_XBC), F32)], 0)
    return (cw, conv_b.reshape(1, SSD_XBC), par, jnp.repeat(dskip, SSD_P).reshape(1, SSD_W),
            norm_w.reshape(1, SSD_W))


def _conf_kernel(v_ref, w_ref, b_ref, g_ref, bt_ref, o_ref, pad_scr, *, seq):
    q = CONF_Q
    pad_scr[0:CONF_PAD, :] = jnp.zeros((CONF_PAD, CONF_W), F32)
    pad_scr[seq + CONF_PAD:seq + 2 * CONF_PAD, :] = jnp.zeros((CONF_PAD, CONF_W), F32)
    pad_scr[CONF_PAD:seq + CONF_PAD, :] = v_ref[...]
    lo = CONF_PAD - CONF_K // 2

    def body(c, _):
        t0 = pl.multiple_of(c * q, q)
        big = pad_scr[pl.ds(t0, q + 2 * CONF_PAD), :]
        acc = jnp.zeros((q, CONF_W), F32) + b_ref[...]
        for k in range(CONF_K):
            acc = acc + big[lo + k:lo + k + q] * w_ref[k:k + 1, :]
        mu = jnp.mean(acc, axis=-1, keepdims=True)
        xc = acc - mu
        var = jnp.mean(xc * xc, axis=-1, keepdims=True)
        o_ref[pl.ds(t0, q), :] = jax.nn.silu(xc * lax.rsqrt(var + EPS) * g_ref[...] + bt_ref[...])
        return 0

    lax.fori_loop(0, seq // q, body, 0)


def _conf_call(v, w, nb, seq, row0):
    s0 = row0 // seq
    full = lambda a: pl.BlockSpec(a.shape, lambda b: (0,) * a.ndim)
    return pl.pallas_call(
        functools.partial(_conf_kernel, seq=seq), grid=(nb,),
        in_specs=[pl.BlockSpec((seq, CONF_W), lambda b: (s0 + b, 0))] + [full(a) for a in w],
        out_specs=pl.BlockSpec((seq, CONF_W), lambda b: (b, 0)),
        out_shape=jax.ShapeDtypeStruct((nb * seq, CONF_W), F32),
        scratch_shapes=[pltpu.VMEM((seq + 2 * CONF_PAD, CONF_W), F32)],
        compiler_params=_cparams("arbitrary"), name="conf",
    )(v, *w)


def _split_bf16(x):
    hi = x.astype(BF16)
    return hi, (x - hi.astype(F32)).astype(BF16)


def _outproj_kernel(x_ref, ada_ref, ys5_ref, ymla_ref, yssd_ref, yconf_ref, wglu_ref, bglu_ref, wo_ref,
                    npost_ref, npre_ref, wrh_ref, wrl_ref, br_ref, x1_ref, h2_ref, gate_ref):
    ada = ada_ref[0]
    a = jax.nn.gelu(ys5_ref[...])
    s5 = a * jax.nn.sigmoid(_bdot(a, wglu_ref[...]) + bglu_ref[...])
    mix = (_bdot(s5, wo_ref[0:256, :]) + _bdot(ymla_ref[...], wo_ref[256:512, :])
           + _bdot(yssd_ref[...], wo_ref[512:768, :]) + _bdot(yconf_ref[...], wo_ref[768:1024, :]))
    x1 = x_ref[...] + ada[2:3] * _rms(mix, npost_ref[...])
    x1_ref[...] = x1
    h2 = _rms(x1, npre_ref[...]) * (1.0 + ada[4:5]) + ada[3:4]
    h2_ref[...] = h2.astype(BF16)
    hi, lo = _split_bf16(h2)
    lg = (jnp.dot(hi, wrh_ref[...], preferred_element_type=F32) + jnp.dot(lo, wrh_ref[...], preferred_element_type=F32)
          + jnp.dot(hi, wrl_ref[...], preferred_element_type=F32) + br_ref[...])
    lane = lax.broadcasted_iota(jnp.int32, lg.shape, 1)
    neg = -jnp.inf
    big = jnp.int32(1 << 20)
    first = lambda hit: jnp.min(jnp.where(hit, lane, big), axis=-1, keepdims=True)
    glm = jnp.where((lane >= MOE_E) & (lane < MOE_E + MOE_GROUPS), lg, neg)
    gmax = jnp.max(glm, axis=-1, keepdims=True)
    p_group = 1.0 / jnp.sum(jnp.exp(glm - gmax), axis=-1, keepdims=True)
    gsel = first(glm == gmax) - MOE_E
    elm = jnp.where((lane < MOE_E) & ((lane // MOE_PER_GROUP) == gsel), lg, neg)
    v1 = jnp.max(elm, axis=-1, keepdims=True)
    i1 = first(elm == v1)
    elm2 = jnp.where(lane == i1, neg, elm)
    v2 = jnp.max(elm2, axis=-1, keepdims=True)
    i2 = first(elm2 == v2)
    e2 = jnp.exp(v2 - v1)
    w1 = p_group / (1.0 + e2)
    w2 = p_group * e2 / (1.0 + e2)
    gate_ref[...] = jnp.where(lane == i1, w1, 0.0) + jnp.where(lane == i2, w2, 0.0)


def _outproj_call(x, ada_t, ys5, ymla, yssd, yconf, w):
    t = x.shape[0]
    row = lambda n: pl.BlockSpec((TM, n), lambda i: (i, 0))
    full = lambda a: pl.BlockSpec(a.shape, lambda i: (0,) * a.ndim)
    return pl.pallas_call(
        _outproj_kernel, grid=(t // TM,),
        in_specs=[row(D_MODEL), pl.BlockSpec((1, 8, D_MODEL), lambda i: (i, 0, 0)), row(256), row(256), row(256),
                  row(256)] + [full(a) for a in w],
        out_specs=[row(D_MODEL), row(D_MODEL), row(LANE)],
        out_shape=[jax.ShapeDtypeStruct((t, D_MODEL), F32), jax.ShapeDtypeStruct((t, D_MODEL), BF16),
                   jax.ShapeDtypeStruct((t, LANE), F32)],
        compiler_params=_cparams("arbitrary"), name="outproj",
    )(x, ada_t, ys5, ymla, yssd, yconf, *w)


def _moe_kernel(h_ref, gate_ref, x1_ref, ada_ref, wgu_ref, wd_ref, npost_ref, o_ref, acc_ref):
    e = pl.program_id(1)

    @pl.when(e == 0)
    def _zero():
        acc_ref[...] = jnp.zeros_like(acc_ref)

    a = jnp.dot(h_ref[...], wgu_ref[0], preferred_element_type=F32)
    lane = lax.broadcasted_iota(jnp.int32, gate_ref.shape, 1)
    ge = jnp.sum(jnp.where(lane == e, gate_ref[...], 0.0), axis=-1, keepdims=True)
    hid = jax.nn.silu(a[:, :MOE_HID]) * a[:, MOE_HID:] * ge
    acc_ref[...] += jnp.dot(hid.astype(BF16), wd_ref[0], preferred_element_type=F32)

    @pl.when(e == MOE_E - 1)
    def _finish():
        o_ref[...] = x1_ref[...] + ada_ref[0][5:6] * _rms(acc_ref[...], npost_ref[...])


def _moe_call(h2, gate, x1, ada_t, wgu, wd, npost):
    t = h2.shape[0]
    tm = MOE_TM
    row = lambda n: pl.BlockSpec((tm, n), lambda i, e: (i, 0))
    return pl.pallas_call(
        _moe_kernel, grid=(t // tm, MOE_E),
        in_specs=[row(D_MODEL), row(LANE), row(D_MODEL),
                  pl.BlockSpec((1, 8, D_MODEL), lambda i, e: (i * (tm // TM), 0, 0)),
                  pl.BlockSpec((1, D_MODEL, 2 * MOE_HID), lambda i, e: (e, 0, 0)),
                  pl.BlockSpec((1, MOE_HID, D_MODEL), lambda i, e: (e, 0, 0)),
                  pl.BlockSpec(npost.shape, lambda i, e: (0, 0))],
        out_specs=row(D_MODEL),
        out_shape=jax.ShapeDtypeStruct((t, D_MODEL), F32),
        scratch_shapes=[pltpu.VMEM((tm, D_MODEL), F32)],
        compiler_params=_cparams("arbitrary", "arbitrary"), name="moe",
    )(h2, gate, x1, ada_t, wgu, wd, npost)


def _pack_w_in(w):
    d = w.shape[0]
    z32, z64 = jnp.zeros((d, 32), F32), jnp.zeros((d, 64), F32)
    kr = w[:, OFF_MLA_KR:OFF_SSD_Z]
    dtw = jnp.pad(w[:, OFF_SSD_DT:OFF_CONF], ((0, 0), (0, LANE - SSD_H)))
    cols = [w[:, OFF_S5:OFF_MLA_Q], w[:, OFF_MLA_Q:OFF_MLA_KV], w[:, OFF_MLA_KV:OFF_MLA_KR],
            z64, kr, z32, z64, _swap8(kr), z32,
            w[:, OFF_SSD_Z:OFF_SSD_XBC], w[:, OFF_SSD_XBC:OFF_SSD_DT], dtw, w[:, OFF_CONF:IN_COLS]]
    return jnp.concatenate(cols, -1).astype(BF16)


def _row(v):
    return v.reshape(1, -1)


def kernel(x_prompt, x_sample, cache_mla_ckv, cache_mla_krope, state_s5, state_ssd, c, c_ctx, w_ada, b_ada, norm_pre1, norm_post1, norm_pre2, norm_post2, w_in, w_out, s5_lam_re, s5_lam_im, s5_log_dt, s5_b_re, s5_b_im, s5_c_re, s5_c_im, s5_d, s5_w_glu, s5_b_glu, mla_qnorm, mla_kvnorm, mla_wuq, mla_wuk, mla_wuv, ssd_conv_w, ssd_conv_b, ssd_dt_bias, ssd_a_log, ssd_d, ssd_norm_w, conf_dw_w, conf_dw_b, conf_ln_g, conf_ln_b, moe_wg, moe_bg, moe_we, moe_be, moe_w_gate, moe_w_up, moe_w_down):
    bp, lp, d = x_prompt.shape
    bs, ls, _ = x_sample.shape
    tp, ts = bp * lp, bs * ls
    x = jnp.concatenate([x_prompt.reshape(tp, d), x_sample.reshape(ts, d)], 0)

    crows = jnp.concatenate([c_ctx[None], c, jnp.zeros((8 - 1 - bs, d), F32)], 0)
    ada = _ada_call(crows, w_ada, b_ada)
    tile_row = np.concatenate([np.zeros(tp // TM, np.int32), 1 + np.repeat(np.arange(bs, dtype=np.int32), ls // TM)])

    rope_p = _rope_tables(lp, rotate=False)
    rope_s = _rope_tables(ls, rotate=True)
    zeros_s5 = tuple(jnp.zeros((bp, S5_G * 2 * S5_P), F32) for _ in range(4))
    zeros_ssd = jnp.zeros((bp, 2, SSD_H * SSD_P, SSD_N), F32)
    pad_kr = lambda kr: jnp.pad(kr, ((0, 0), (0, 0), (MLA_NOPE, LANE - MLA_NOPE - MLA_ROPE)))

    ckv_l, kr_l, s5_l, ssd_l = [], [], [], []
    for l in range(DEPTH):
        ada_t = jnp.pad(ada[l][tile_row].reshape(-1, 6, d), ((0, 0), (0, 2), (0, 0)))
        u, cq, ckv, krab, z, xbc, dt, v = _inproj_call(
            x, ada_t, _row(norm_pre1[l]), _pack_w_in(w_in[l]), _row(mla_qnorm[l]), _row(mla_kvnorm[l]))

        tabs = _s5_tables(s5_lam_re[l], s5_lam_im[l], s5_log_dt[l], s5_b_re[l], s5_b_im[l], s5_c_re[l], s5_c_im[l],
                          s5_d[l])
        yp, stf, stb = _s5_call(_s5_layout_in(u[:tp], bp, lp), tabs, zeros_s5, bp)
        ys, _, _ = _s5_call(_s5_layout_in(u[tp:], bs, ls), tabs, _s5_state_in(state_s5[:, l]), bs)
        y_s5 = jnp.concatenate([_s5_layout_out(yp, bp, lp), _s5_layout_out(ys, bs, ls)], 0)
        s5_l.append(_s5_state_out(stf, stb))

        mw = _mla_weights(mla_wuq[l], mla_wuk[l], mla_wuv[l])
        ymla_p = _mla_call(cq, ckv, krab, rope_p, None, mw, bp, lp, 0)
        ymla_s = _mla_call(cq, ckv, krab, rope_s, (cache_mla_ckv[:, l], pad_kr(cache_mla_krope[:, l])), mw, bs, ls, tp)
        y_mla = jnp.concatenate([ymla_p, ymla_s], 0)
        ckv_l.append(ckv[:tp].reshape(bp, lp, MLA_KVLORA))
        kr_l.append(krab[:tp, MLA_NOPE:MLA_NOPE + MLA_ROPE].reshape(bp, lp, MLA_ROPE))

        sw = _ssd_weights(ssd_conv_w[l], ssd_conv_b[l], ssd_dt_bias[l], ssd_a_log[l], ssd_d[l], ssd_norm_w[l])
        yssd_p, st_ssd = _ssd_call(xbc, z, dt, sw, zeros_ssd, bp, lp, 0)
        yssd_s, _ = _ssd_call(xbc, z, dt, sw, state_ssd[:, l].reshape(bs, 2, SSD_H * SSD_P, SSD_N), bs, ls, tp)
        y_ssd = jnp.concatenate([yssd_p, yssd_s], 0)
        ssd_l.append(st_ssd.reshape(bp, 2, SSD_H, SSD_P, SSD_N))

        cw = (jnp.concatenate([conf_dw_w[l], jnp.zeros((1, CONF_W), F32)], 0), _row(conf_dw_b[l]),
              _row(conf_ln_g[l]), _row(conf_ln_b[l]))
        y_conf = jnp.concatenate([_conf_call(v, cw, bp, lp, 0), _conf_call(v, cw, bs, ls, tp)], 0)

        wr = jnp.pad(jnp.concatenate([moe_we[l], moe_wg[l]], -1), ((0, 0), (0, LANE - MOE_E - MOE_GROUPS)))
        wrh, wrl = _split_bf16(wr)
        br = jnp.pad(jnp.concatenate([moe_be[l], moe_bg[l]]), (0, LANE - MOE_E - MOE_GROUPS)).reshape(1, LANE)
        ow = (s5_w_glu[l].astype(BF16), _row(s5_b_glu[l]), w_out[l].astype(BF16), _row(norm_post1[l]),
              _row(norm_pre2[l]), wrh, wrl, br)
        x1, h2, gate = _outproj_call(x, ada_t, y_s5, y_mla, y_ssd, y_conf, ow)

        wgu = jnp.concatenate([moe_w_gate[l], moe_w_up[l]], -1).astype(BF16)
        x = _moe_call(h2, gate, x1, ada_t, wgu, moe_w_down[l].astype(BF16), _row(norm_post2[l]))

    return (x[:tp].reshape(bp, lp, d), x[tp:].reshape(bs, ls, d),
            jnp.stack(ckv_l, 1), jnp.stack(kr_l, 1), jnp.stack(s5_l, 1), jnp.stack(ssd_l, 1))
```

```python
import functools
import math

import jax
import jax.numpy as jnp
import numpy as np
from jax import lax
from jax.experimental import pallas as pl
from jax.experimental.pallas import tpu as pltpu

F32 = jnp.float32
BF16 = jnp.bfloat16

D_MODEL = 1024
DEPTH = 2
GRID_W = 64
EPS = 1e-6

S5_W = 256
S5_GSIZE = 16
S5_G = 16
S5_P = 64
S5_CHUNK = 16

MLA_H = 4
MLA_NOPE = 64
MLA_ROPE = 32
MLA_V = 64
MLA_QLORA = 256
MLA_KVLORA = 128
ROPE_BASE = 10000.0
MLA_TQ = 256
MLA_KC = 256

SSD_W = 256
SSD_P = 64
SSD_H = 4
SSD_G = 2
SSD_N = 64
SSD_CHUNK = 128
SSD_XBC = 512

CONF_W = 256
CONF_K = 31
CONF_Q = 64
CONF_PAD = 16

MOE_GROUPS = 4
MOE_PER_GROUP = 4
MOE_E = 16
MOE_HID = 256
MOE_TM = 512

OFF_S5 = 0
OFF_MLA_Q = 256
OFF_MLA_KV = 512
OFF_MLA_KR = 640
OFF_SSD_Z = 672
OFF_SSD_XBC = 928
OFF_SSD_DT = 1440
OFF_CONF = 1444
IN_COLS = 1956

LANE = 128
SUBLANE = 8
TM = 256
P_U, P_Q, P_KV, P_KRA, P_KRB, P_Z, P_XBC, P_DT, P_CA, P_CB, P_END = (
    0, 256, 512, 640, 768, 896, 1152, 1664, 1792, 2048, 2304)

VMEM_LIMIT = 56 * 1024 * 1024


def _bdot(a, b):
    return jnp.dot(a.astype(BF16), b.astype(BF16), preferred_element_type=F32)


def _bdot_nt(a, b):
    return lax.dot_general(a.astype(BF16), b.astype(BF16), (((1,), (1,)), ((), ())), preferred_element_type=F32)


def _rms(x, g):
    return x * lax.rsqrt(jnp.mean(x * x, axis=-1, keepdims=True) + EPS) * g


def _cparams(*sem):
    return pltpu.CompilerParams(dimension_semantics=sem if sem else None, vmem_limit_bytes=VMEM_LIMIT)


def _ada_kernel(c_ref, w_ref, b_ref, o_ref):
    o_ref[0] = _bdot(jax.nn.silu(c_ref[...]), w_ref[0]) + b_ref[0]


def _ada_call(crows, w_ada, b_ada):
    tn = 1024
    d6 = 6 * D_MODEL
    return pl.pallas_call(
        _ada_kernel, grid=(DEPTH, d6 // tn),
        in_specs=[pl.BlockSpec((8, D_MODEL), lambda l, j: (0, 0)),
                  pl.BlockSpec((1, D_MODEL, tn), lambda l, j: (l, 0, j)),
                  pl.BlockSpec((1, 1, tn), lambda l, j: (l, 0, j))],
        out_specs=pl.BlockSpec((1, 8, tn), lambda l, j: (l, 0, j)),
        out_shape=jax.ShapeDtypeStruct((DEPTH, 8, d6), F32),
        compiler_params=_cparams("arbitrary", "arbitrary"), name="ada",
    )(crows, w_ada, b_ada.reshape(DEPTH, 1, d6))


def _lane_block(shape):
    return lax.broadcasted_iota(jnp.int32, shape, 1) // S5_GSIZE


def _chunk_rows_to_group_rows(x, perm):
    r = jnp.dot(perm, x.astype(BF16), preferred_element_type=F32)
    blk = _lane_block((S5_CHUNK, S5_W))
    out = []
    for g in range(S5_G):
        acc = None
        for t in range(S5_CHUNK):
            src = r[t * S5_CHUNK:(t + 1) * S5_CHUNK, :]
            shift = ((t - g) * S5_GSIZE) % S5_W
            src = pltpu.roll(src, shift, 1) if shift else src
            acc = src if acc is None else jnp.where(blk == t, src, acc)
        out.append(acc)
    return out


def _group_rows_to_chunk_rows(ys, perm_t):
    blk = _lane_block((S5_CHUNK, S5_W))
    rows = []
    for t in range(S5_CHUNK):
        acc = None
        for g in range(S5_G):
            shift = ((g - t) * S5_GSIZE) % S5_W
            src = pltpu.roll(ys[g], shift, 1) if shift else ys[g]
            acc = src if acc is None else jnp.where(blk == g, src, acc)
        rows.append(acc)
    z = jnp.concatenate(rows, 0)
    return jnp.dot(perm_t, z.astype(BF16), preferred_element_type=F32)


def _pick(first, second, n_first):
    return jnp.where(pl.program_id(0) < n_first, first[...], second[...])


def _two_group_specs(block, n_first, lead=()):
    nl = len(lead)
    first = pl.BlockSpec(lead + block, lambda i, *_: (0,) * nl + (jnp.minimum(i, n_first - 1), 0))
    second = pl.BlockSpec(lead + block, lambda i, *_: (0,) * nl + (jnp.maximum(i - n_first, 0), 0))
    return [first, second]


def _inproj_kernel(xp_ref, xs_ref, ada_ref, npre_ref, w_ref, qn_ref, kvn_ref, perm_ref,
                   u_ref, cq_ref, ckv_ref, krab_ref, z_ref, xbc_ref, dt_ref, v_ref, *, n_first):
    ada = ada_ref[0]
    h = _rms(_pick(xp_ref, xs_ref, n_first), npre_ref[...]) * (1.0 + ada[1:2]) + ada[0:1]
    p = _bdot(h, w_ref[...])
    for g, ug in enumerate(_chunk_rows_to_group_rows(p[:, P_U:P_Q], perm_ref[...])):
        u_ref[g] = ug.astype(BF16)
    cq_ref[...] = _rms(p[:, P_Q:P_KV], qn_ref[...]).astype(BF16)
    ckv_ref[...] = _rms(p[:, P_KV:P_KRA], kvn_ref[...])
    krab_ref[...] = p[:, P_KRA:P_Z]
    z_ref[...] = p[:, P_Z:P_XBC]
    xbc_ref[...] = p[:, P_XBC:P_DT]
    dt_ref[...] = p[:, P_DT:P_CA]
    v_ref[...] = p[:, P_CA:P_CB] * jax.nn.sigmoid(p[:, P_CB:P_END])


def _chunk_perm():
    idx = np.arange(TM)
    src = (idx % S5_CHUNK) * S5_CHUNK + idx // S5_CHUNK
    return jnp.asarray(np.eye(TM, dtype=np.float32)[src], BF16)


def _inproj_call(xp, xs, ada_t, npre, w, qn, kvn):
    t = xp.shape[0] + xs.shape[0]
    n_first = xp.shape[0] // TM
    row = lambda n: pl.BlockSpec((TM, n), lambda i: (i, 0))
    full = lambda a: pl.BlockSpec(a.shape, lambda i: (0,) * a.ndim)
    widths = (256, 128, 256, 256, 512, 128, 256)
    dts = (BF16, F32, F32, F32, F32, F32, F32)
    perm = _chunk_perm()
    cpt = TM // S5_CHUNK
    return pl.pallas_call(
        functools.partial(_inproj_kernel, n_first=n_first), grid=(t // TM,),
        in_specs=_two_group_specs((TM, D_MODEL), n_first)
        + [pl.BlockSpec((1, 8, D_MODEL), lambda i: (i, 0, 0)), full(npre), full(w), full(qn), full(kvn), full(perm)],
        out_specs=[pl.BlockSpec((S5_G, cpt, S5_W), lambda i: (0, i, 0))] + [row(n) for n in widths],
        out_shape=[jax.ShapeDtypeStruct((S5_G, t // S5_CHUNK, S5_W), BF16)]
        + [jax.ShapeDtypeStruct((t, n), d) for n, d in zip(widths, dts)],
        compiler_params=_cparams("arbitrary"), name="inproj",
    )(xp, xs, ada_t, npre, w, qn, kvn, perm)


def _s5_kernel(u_ref, r_ref, vf_ref, vb_ref, a_ref, h0f_ref, h0fs_ref, h0b_ref, h0bs_ref,
               y_ref, stf_ref, stb_ref, sf, sfs, sb, sbs, *, nb, nc):
    gw = 2 * S5_P
    for g in range(S5_G):
        r = jnp.dot(u_ref[g], r_ref[g, :, S5_W:], preferred_element_type=F32)
        sl = slice(g * gw, (g + 1) * gw)
        sf[:, sl] = r[:, 0:128]
        sfs[:, sl] = r[:, 128:256]
        sb[:, sl] = r[:, 256:384]
        sbs[:, sl] = r[:, 384:512]
    arf, aif, aisf = a_ref[0:1], a_ref[1:2], a_ref[2:3]
    arb, aib, aisb = a_ref[3:4], a_ref[4:5], a_ref[5:6]
    cpt = 1 if nb % SUBLANE == 0 else SUBLANE // nb
    rows = cpt * nb
    nt = nc // cpt

    def body(i, carry):
        hf, hfs, hb, hbs = carry
        rf = pl.ds(pl.multiple_of(i * rows, rows), rows)
        rb = pl.ds(pl.multiple_of((nt - 1 - i) * rows, rows), rows)
        inc_f, inc_fs, inc_b, inc_bs = sf[rf, :], sfs[rf, :], sb[rb, :], sbs[rb, :]
        in_f, in_b = [None] * cpt, [None] * cpt
        for k in range(cpt):
            kf = slice(k * nb, (k + 1) * nb)
            kb = slice((cpt - 1 - k) * nb, (cpt - k) * nb)
            in_f[k] = hf
            in_b[cpt - 1 - k] = hb
            hf, hfs = arf * hf + aif * hfs + inc_f[kf], arf * hfs + aisf * hf + inc_fs[kf]
            hb, hbs = arb * hb + aib * hbs + inc_b[kb], arb * hbs + aisb * hb + inc_bs[kb]
        sf[rf, :] = in_f[0] if cpt == 1 else jnp.concatenate(in_f, 0)
        sb[rb, :] = in_b[0] if cpt == 1 else jnp.concatenate(in_b, 0)
        return hf, hfs, hb, hbs

    hf, _, hb, _ = lax.fori_loop(0, nt, body, (h0f_ref[...], h0fs_ref[...], h0b_ref[...], h0bs_ref[...]))
    stf_ref[...] = hf
    stb_ref[...] = hb
    for g in range(S5_G):
        sl = slice(g * gw, (g + 1) * gw)
        y = (jnp.dot(u_ref[g], r_ref[g, :, :S5_W], preferred_element_type=F32)
             + _bdot(sf[:, sl], vf_ref[g]) + _bdot(sb[:, sl], vb_ref[g]))
        y_ref[g] = y.astype(BF16)


def _s5_call(u, tabs, h0, nb):
    n = u.shape[1]
    r, vf, vb, avec = tabs
    sw = S5_G * 2 * S5_P
    st = jax.ShapeDtypeStruct((nb, sw), F32)
    return pl.pallas_call(
        functools.partial(_s5_kernel, nb=nb, nc=n // nb),
        out_shape=[jax.ShapeDtypeStruct(u.shape, BF16), st, st],
        scratch_shapes=[pltpu.VMEM((n, sw), F32) for _ in range(4)],
        compiler_params=_cparams(), name="s5",
    )(u, r, vf, vb, avec, *h0)


def _cpow(br, bi, e, nbits):
    pr = jnp.ones(e.shape, F32)
    pi = jnp.zeros(e.shape, F32)
    for k in range(nbits):
        bit = ((e >> k) & 1) == 1
        pr, pi = jnp.where(bit, pr * br - pi * bi, pr), jnp.where(bit, pr * bi + pi * br, pi)
        br, bi = br * br - bi * bi, 2.0 * br * bi
    return pr, pi


def _s5tab_kernel(prow_ref, pcol_ref, bt_ref, ct_ref, dsk_ref, etile_ref, r_ref, vf_ref, vb_ref, a_ref):
    q = S5_CHUNK
    hp = lax.Precision.HIGHEST

    def abar(lre, lim, ldt):
        dt = jnp.exp(ldt)
        mag = jnp.exp(lre * dt)
        return mag * jnp.cos(lim * dt), mag * jnp.sin(lim * dt)

    def per_step(x):
        return jnp.dot(x, etile_ref[...], precision=hp, preferred_element_type=F32)

    step_of_col = lax.broadcasted_iota(jnp.int32, (S5_P, S5_W), 1) // S5_GSIZE
    step_of_row = lax.broadcasted_iota(jnp.int32, (S5_W, S5_P), 0) // S5_GSIZE
    kcat, wtabs, arows = [], [], []
    for d in range(2):
        prow = prow_ref[d, 0]
        lre, lim = prow[0:1], prow[1:2]
        abr, abi = abar(lre, lim, prow[2:3])
        den = lre * lre + lim * lim
        fr = ((abr - 1.0) * lre + abi * lim) / den
        fi = (abi * lre - (abr - 1.0) * lim) / den
        btr, bti = bt_ref[d, 0, 0], bt_ref[d, 0, 1]
        bbr, bbi = fr * btr - fi * bti, fr * bti + fi * btr
        pcol = pcol_ref[d, 0]
        cbr, cbi = abar(pcol[:, 0:1], pcol[:, 1:2], pcol[:, 2:3])
        ctr, cti = per_step(ct_ref[d, 0, 0]), per_step(ct_ref[d, 0, 1])

        def c_times_power(e):
            pr, pi = _cpow(cbr, cbi, e, 5)
            return ctr * pr - cti * pi, ctr * pi + cti * pr

        vr, vi = c_times_power(step_of_col + 1 if d == 0 else q - step_of_col)
        (vf_ref if d == 0 else vb_ref)[0] = jnp.concatenate([vr, -vi], 0).astype(BF16)
        lr, li = c_times_power(step_of_col if d == 0 else q - 1 - step_of_col)
        kcat.append(jnp.dot(bbr, lr, precision=hp, preferred_element_type=F32)
                    - jnp.dot(bbi, li, precision=hp, preferred_element_type=F32))
        pr, pi = _cpow(abr, abi, q - 1 - step_of_row if d == 0 else step_of_row, 4)
        tbr, tbi = jnp.concatenate([bbr] * q, 0), jnp.concatenate([bbi] * q, 0)
        wr, wi = pr * tbr - pi * tbi, pr * tbi + pi * tbr
        wtabs += [wr, wi, wi, wr]
        ar, ai = _cpow(abr, abi, jnp.full((1, S5_P), q, jnp.int32), 5)
        arows += [jnp.concatenate([ar, ar], 1), jnp.concatenate([-ai, ai], 1), jnp.concatenate([ai, -ai], 1)]
    lane = lax.broadcasted_iota(jnp.int32, (S5_GSIZE, S5_W), 1)
    rows = []
    for s in range(q):
        shr = S5_GSIZE * s
        shl = S5_GSIZE * (q - 1 - s)
        f = jnp.where(lane >= shr, pltpu.roll(kcat[0], shr, 1), 0.0) if shr else kcat[0]
        b = jnp.where(lane < S5_W - shl, pltpu.roll(kcat[1], S5_W - shl, 1), 0.0) if shl else kcat[1]
        rows.append(f + b)
    ri = lax.broadcasted_iota(jnp.int32, (S5_W, S5_W), 0)
    ci = lax.broadcasted_iota(jnp.int32, (S5_W, S5_W), 1)
    tt = jnp.concatenate(rows, 0) + jnp.where(ri == ci, dsk_ref[0], 0.0)
    r_ref[0] = jnp.concatenate([tt] + wtabs, 1).astype(BF16)
    a_ref[...] = jnp.concatenate(arows + [jnp.zeros((2, 2 * S5_P), F32)], 0)


def _s5_tables(lam_re, lam_im, log_dt, b_re, b_im, c_re, c_im, dskip):
    zeros = jnp.zeros((2, S5_G, 5, S5_P), F32)
    prow = jnp.concatenate([lam_re[:, :, None], lam_im[:, :, None],
                            jnp.broadcast_to(log_dt[:, :, None, None], (2, S5_G, 1, S5_P)), zeros], 2)
    pcol = jnp.swapaxes(prow, 2, 3)
    bt = jnp.stack([jnp.swapaxes(b_re, 2, 3), jnp.swapaxes(b_im, 2, 3)], 2)
    ct = jnp.stack([jnp.swapaxes(c_re, 2, 3), jnp.swapaxes(c_im, 2, 3)], 2)
    dsk = jnp.tile(dskip.reshape(S5_G, 1, S5_GSIZE), (1, 1, S5_CHUNK))
    etile = jnp.asarray(np.tile(np.eye(S5_GSIZE, dtype=np.float32), (1, S5_CHUNK)))
    gblk = lambda a: pl.BlockSpec((2, 1) + a.shape[2:], lambda g: (0, g) + (0,) * (a.ndim - 2))
    sw = S5_G * 2 * S5_P
    return pl.pallas_call(
        _s5tab_kernel, grid=(S5_G,),
        in_specs=[gblk(prow), gblk(pcol), gblk(bt), gblk(ct), pl.BlockSpec((1, 1, S5_W), lambda g: (g, 0, 0)),
                  pl.BlockSpec(etile.shape, lambda g: (0, 0))],
        out_specs=[pl.BlockSpec((1, S5_W, 768), lambda g: (g, 0, 0)), pl.BlockSpec((1, 2 * S5_P, S5_W), lambda g: (g, 0, 0)),
                   pl.BlockSpec((1, 2 * S5_P, S5_W), lambda g: (g, 0, 0)), pl.BlockSpec((8, 2 * S5_P), lambda g: (0, g))],
        out_shape=[jax.ShapeDtypeStruct((S5_G, S5_W, 768), BF16), jax.ShapeDtypeStruct((S5_G, 2 * S5_P, S5_W), BF16),
                   jax.ShapeDtypeStruct((S5_G, 2 * S5_P, S5_W), BF16), jax.ShapeDtypeStruct((8, sw), F32)],
        compiler_params=_cparams("arbitrary"), name="s5tab",
    )(prow, pcol, bt, ct, dsk, etile)


def _s5_rows_in(u, nb):
    g, n, w = u.shape
    return jnp.swapaxes(u.reshape(g, nb, n // nb, w), 1, 2).reshape(g, n, w)


def _s5_rows_out(y, nb):
    g, n, w = y.shape
    return jnp.swapaxes(y.reshape(g, n // nb, nb, w), 1, 2).reshape(g, n, w)


def _s5_state_in(state):
    b = state.shape[0]
    sw = jnp.transpose(state, (0, 1, 2, 4, 3))
    plain = sw.reshape(b, 2, S5_G * 2 * S5_P)
    swapped = sw[:, :, :, ::-1, :].reshape(b, 2, S5_G * 2 * S5_P)
    return plain[:, 0], swapped[:, 0], plain[:, 1], swapped[:, 1]


def _s5_state_out(stf, stb):
    b = stf.shape[0]
    st = jnp.stack([stf, stb], 1).reshape(b, 2, S5_G, 2, S5_P)
    return jnp.transpose(st, (0, 1, 2, 4, 3))


def _mla_kernel(*refs, seq, lctx):
    if lctx:
        (cq_ref, ckv_ref, krab_ref, cosq_ref, sinq_ref, cosk_ref, sink_ref, cckv_ref, ckr_ref,
         wqa_ref, wqb_ref, wk_ref, wv_ref, o_ref, k_scr, v_scr, s_scr) = refs
    else:
        (cq_ref, ckv_ref, krab_ref, cosq_ref, sinq_ref, cosk_ref, sink_ref,
         wqa_ref, wqb_ref, wk_ref, wv_ref, o_ref, k_scr, v_scr, s_scr) = refs
    lk = lctx + seq
    rb = min(seq, 512)

    @pl.when(pl.program_id(1) == 0)
    def _build_keys():
        if lctx:
            ckv = cckv_ref[0]
            kn = _bdot(ckv, wk_ref[...])
            vn = _bdot(ckv, wv_ref[...])
            for h in range(MLA_H):
                k_scr[h, 0:lctx, :] = (kn[:, h * LANE:(h + 1) * LANE] + ckr_ref[0]).astype(BF16)
                v_scr[h, 0:lctx, :] = vn[:, h * LANE:(h + 1) * LANE].astype(BF16)

        def chunk(i, _):
            r0 = pl.multiple_of(i * rb, rb)
            rows = pl.ds(r0, rb)
            ckv = ckv_ref[rows, :]
            kn = _bdot(ckv, wk_ref[...])
            vn = _bdot(ckv, wv_ref[...])
            krab = krab_ref[rows, :]
            krx = krab[:, :LANE] * cosk_ref[rows, :] + krab[:, LANE:] * sink_ref[rows, :]
            dst = pl.ds(pl.multiple_of(lctx + r0, rb if lctx % rb == 0 else 256), rb)
            for h in range(MLA_H):
                k_scr[h, dst, :] = (kn[:, h * LANE:(h + 1) * LANE] + krx).astype(BF16)
                v_scr[h, dst, :] = vn[:, h * LANE:(h + 1) * LANE].astype(BF16)
            return 0

        lax.fori_loop(0, seq // rb, chunk, 0)

    cq = cq_ref[...]
    qa = _bdot(cq, wqa_ref[...])
    qb = _bdot(cq, wqb_ref[...])
    cosq, sinq = cosq_ref[...], sinq_ref[...]
    scale = 1.0 / math.sqrt(MLA_NOPE + MLA_ROPE)
    outs = []
    chunks = [slice(j * MLA_KC, (j + 1) * MLA_KC) for j in range(lk // MLA_KC)]
    for h in range(MLA_H):
        hs = slice(h * LANE, (h + 1) * LANE)
        qh = ((qa[:, hs] * cosq + qb[:, hs] * sinq) * scale).astype(BF16)
        mp = None
        for ks in chunks:
            s = _bdot_nt(qh, k_scr[h, ks, :])
            s_scr[:, ks] = s
            for t in range(MLA_KC // LANE):
                part = s[:, t * LANE:(t + 1) * LANE]
                mp = part if mp is None else jnp.maximum(mp, part)
        m = jnp.max(mp, axis=-1, keepdims=True)
        lp = jnp.zeros((MLA_TQ, LANE), F32)
        acc = jnp.zeros((MLA_TQ, LANE), F32)
        for ks in chunks:
            e = jnp.exp(s_scr[:, ks] - m)
            for t in range(MLA_KC // LANE):
                lp = lp + e[:, t * LANE:(t + 1) * LANE]
            acc = acc + _bdot(e, v_scr[h, ks, :])
        outs.append((acc / jnp.sum(lp, axis=-1, keepdims=True))[:, :MLA_V])
    o_ref[...] = jnp.concatenate(outs, axis=-1)


def _mla_call(cq, ckv, krab, rope, ctx, w, nb, seq, row0):
    lctx = 0 if ctx is None else ctx[0].shape[1]
    nq = seq // MLA_TQ
    q0, s0 = row0 // MLA_TQ, row0 // seq
    cosx, sinx = rope
    qrow = lambda n: pl.BlockSpec((MLA_TQ, n), lambda b, i: (q0 + b * nq + i, 0))
    srow = lambda n: pl.BlockSpec((seq, n), lambda b, i: (s0 + b, 0))
    full = lambda a: pl.BlockSpec(a.shape, lambda b, i: (0,) * a.ndim)
    in_specs = [qrow(256), srow(128), srow(256), pl.BlockSpec((MLA_TQ, LANE), lambda b, i: (i, 0)),
                pl.BlockSpec((MLA_TQ, LANE), lambda b, i: (i, 0)), full(cosx), full(sinx)]
    args = [cq, ckv, krab, cosx, sinx, cosx, sinx]
    if lctx:
        in_specs += [pl.BlockSpec((1, lctx, LANE), lambda b, i: (b, 0, 0))] * 2
        args += list(ctx)
    in_specs += [full(a) for a in w]
    args += list(w)
    lk = lctx + seq
    return pl.pallas_call(
        functools.partial(_mla_kernel, seq=seq, lctx=lctx), grid=(nb, nq),
        in_specs=in_specs,
        out_specs=pl.BlockSpec((MLA_TQ, 256), lambda b, i: (b * nq + i, 0)),
        out_shape=jax.ShapeDtypeStruct((nb * seq, 256), F32),
        scratch_shapes=[pltpu.VMEM((MLA_H, lk, LANE), BF16), pltpu.VMEM((MLA_H, lk, LANE), BF16),
                        pltpu.VMEM((MLA_TQ, lk), F32)],
        compiler_params=_cparams("arbitrary", "arbitrary"), name="mla",
    )(*args)


def _swap8(w):
    s = w.shape
    return w.reshape(s[:-1] + (s[-1] // 16, 2, 8))[..., ::-1, :].reshape(s)


def _mla_weights(wuq, wuk, wuv):
    z32 = jnp.zeros((MLA_QLORA, 32), F32)
    z64 = jnp.zeros((MLA_QLORA, 64), F32)
    k64 = jnp.zeros((MLA_KVLORA, 64), F32)
    qa, qb, wk, wv = [], [], [], []
    for h in range(MLA_H):
        rope_w = wuq[:, h, MLA_NOPE:]
        qa += [wuq[:, h, :MLA_NOPE], rope_w, z32]
        qb += [z64, _swap8(rope_w), z32]
        wk += [wuk[:, h], k64]
        wv += [wuv[:, h], k64]
    cat = lambda xs: jnp.concatenate(xs, -1).astype(BF16)
    return cat(qa), cat(qb), cat(wk), cat(wv)


def _rope_tables(seq, rotate):
    ones = jnp.ones((seq, MLA_NOPE), F32)
    z32 = jnp.zeros((seq, 32), F32)
    if not rotate:
        return (jnp.concatenate([ones, jnp.ones((seq, MLA_ROPE), F32), z32], -1), jnp.zeros((seq, LANE), F32))
    pos = jnp.arange(seq)
    row = (pos // GRID_W).astype(F32)
    col = (pos % GRID_W).astype(F32)
    half = MLA_ROPE // 2
    inv = ROPE_BASE ** (-jnp.arange(0, half, 2, dtype=F32) / half)
    ang = jnp.concatenate([row[:, None] * inv, col[:, None] * inv], -1)
    cos, sin = jnp.cos(ang), jnp.sin(ang)
    q = MLA_ROPE // 4
    cos32 = jnp.concatenate([cos[:, :q], cos[:, :q], cos[:, q:], cos[:, q:]], -1)
    sin32 = jnp.concatenate([-sin[:, :q], sin[:, :q], -sin[:, q:], sin[:, q:]], -1)
    return jnp.concatenate([ones, cos32, z32], -1), jnp.concatenate([jnp.zeros_like(ones), sin32, z32], -1)


def _cumsum_rows(x, reverse):
    n = x.shape[0]
    row = lax.broadcasted_iota(jnp.int32, x.shape, 0)
    s = 1
    while s < n:
        if reverse:
            x = x + jnp.where(row < n - s, pltpu.roll(x, n - s, 0), 0.0)
        else:
            x = x + jnp.where(row >= s, pltpu.roll(x, s, 0), 0.0)
        s *= 2
    return x


def _ssd_kernel(xbc_ref, z_ref, dt_ref, cw_ref, cb_ref, par_ref, dsk_ref, nw_ref, h0_ref,
                y_ref, st_ref, xc_scr, yf_scr, yb_scr, hf_scr, hb_scr, *, seq):
    q = SSD_CHUNK
    nc = seq // q
    row = lax.broadcasted_iota(jnp.int32, (q, 1), 0)

    def conv(c, _):
        t0 = pl.multiple_of(c * q, q)
        cur = xbc_ref[pl.ds(t0, q), :]
        prev8 = xbc_ref[pl.ds(pl.multiple_of(jnp.maximum(t0 - 8, 0), 8), 8), :]
        next8 = xbc_ref[pl.ds(pl.multiple_of(jnp.minimum(t0 + q, seq - 8), 8), 8), :]
        prow = jnp.where(c > 0, prev8[7:8], 0.0)
        nrow = jnp.where(c < nc - 1, next8[0:1], 0.0)
        up = jnp.where(row == 0, prow, pltpu.roll(cur, 1, 0))
        dn = jnp.where(row == q - 1, nrow, pltpu.roll(cur, q - 1, 0))
        acc = up * cw_ref[0:1] + cur * cw_ref[1:2] + dn * cw_ref[2:3] + cb_ref[...]
        xc_scr[pl.ds(t0, q), :] = jax.nn.silu(acc)
        return 0

    lax.fori_loop(0, nc, conv, 0)
    hf_scr[...] = h0_ref[0, 0]
    hb_scr[...] = h0_ref[0, 1]
    ii = lax.broadcasted_iota(jnp.int32, (q, q), 0)
    jj = lax.broadcasted_iota(jnp.int32, (q, q), 1)

    def one_chunk(c, d, h_scr, y_scr):
        t0 = pl.multiple_of(c * q, q)
        xc = xc_scr[pl.ds(t0, q), :]
        xs = xc[:, :SSD_W]
        xs_t = jnp.transpose(xs)
        dt = jax.nn.softplus(dt_ref[pl.ds(t0, q), :] + par_ref[d:d + 1])
        dta = dt * par_ref[2 + d:3 + d]
        cs = _cumsum_rows(dta, reverse=(d == 1))
        cs_t = jnp.transpose(cs)
        dt_t = jnp.transpose(dt)
        edge = cs[q - 1:q] if d == 0 else cs[0:1]
        wts = jnp.exp(edge - cs) * dt
        ecs = jnp.exp(cs)
        mask = (ii >= jj) if d == 0 else (ii <= jj)
        for g in range(SSD_G):
            bm = xc[:, SSD_W + g * SSD_N:SSD_W + (g + 1) * SSD_N]
            cm = xc[:, SSD_W + SSD_G * SSD_N + g * SSD_N:SSD_W + SSD_G * SSD_N + (g + 1) * SSD_N]
            cbm = _bdot_nt(cm, bm)
            for hh in range(SSD_H // SSD_G):
                h = g * (SSD_H // SSD_G) + hh
                ps = slice(h * SSD_P, (h + 1) * SSD_P)
                lmat = jnp.exp(jnp.where(mask, cs[:, h:h + 1] - cs_t[h:h + 1, :], -jnp.inf))
                y = _bdot(cbm * lmat * dt_t[h:h + 1, :], xs[:, ps])
                h_in = h_scr[ps, :]
                y = y + _bdot_nt(cm, h_in) * ecs[:, h:h + 1]
                y_scr[pl.ds(t0, q), ps] = y
                st = _bdot(xs_t[ps, :], bm * wts[:, h:h + 1])
                h_scr[ps, :] = jnp.exp(edge[:, h:h + 1]) * h_in + st

    def chunks(c, _):
        one_chunk(c, 0, hf_scr, yf_scr)
        one_chunk(nc - 1 - c, 1, hb_scr, yb_scr)
        return 0

    lax.fori_loop(0, nc, chunks, 0)
    st_ref[0, 0] = hf_scr[...]
    st_ref[0, 1] = hb_scr[...]

    def finish(c, _):
        rows = pl.ds(pl.multiple_of(c * q, q), q)
        y = yf_scr[rows, :] + yb_scr[rows, :] + dsk_ref[...] * xc_scr[rows, 0:SSD_W]
        y_ref[rows, :] = _rms(y * jax.nn.silu(z_ref[rows, :]), nw_ref[...])
        return 0

    lax.fori_loop(0, nc, finish, 0)


def _ssd_call(xbc, z, dt, w, h0, nb, seq, row0):
    s0 = row0 // seq
    srow = lambda n: pl.BlockSpec((seq, n), lambda b: (s0 + b, 0))
    full = lambda a: pl.BlockSpec(a.shape, lambda b: (0,) * a.ndim)
    hp = SSD_H * SSD_P
    st_spec = pl.BlockSpec((1, 2, hp, SSD_N), lambda b: (b, 0, 0, 0))
    return pl.pallas_call(
        functools.partial(_ssd_kernel, seq=seq), grid=(nb,),
        in_specs=[srow(SSD_XBC), srow(SSD_W), srow(LANE)] + [full(a) for a in w] + [st_spec],
        out_specs=[pl.BlockSpec((seq, SSD_W), lambda b: (b, 0)), st_spec],
        out_shape=[jax.ShapeDtypeStruct((nb * seq, SSD_W), F32), jax.ShapeDtypeStruct((nb, 2, hp, SSD_N), F32)],
        scratch_shapes=[pltpu.VMEM((seq, SSD_XBC), F32), pltpu.VMEM((seq, SSD_W), F32), pltpu.VMEM((seq, SSD_W), F32),
                        pltpu.VMEM((hp, SSD_N), F32), pltpu.VMEM((hp, SSD_N), F32)],
        compiler_params=_cparams("arbitrary"), name="ssd",
    )(xbc, z, dt, *w, h0)


def _ssd_weights(conv_w, conv_b, dt_bias, a_log, dskip, norm_w):
    pad = lambda x: jnp.pad(x, ((0, 0), (0, LANE - SSD_H)))
    par = jnp.concatenate([pad(dt_bias), pad(-jnp.exp(a_log)), jnp.zeros((4, LANE), F32)], 0)
    cw = jnp.concatenate([conv_w, jnp.zeros((5, SSD_XBC), F32)], 0)
    return (cw, conv_b.reshape(1, SSD_XBC), par, jnp.repeat(dskip, SSD_P).reshape(1, SSD_W),
            norm_w.reshape(1, SSD_W))


def _conf_kernel(v_ref, w_ref, b_ref, g_ref, bt_ref, o_ref, pad_scr, *, seq):
    q = CONF_Q
    pad_scr[0:CONF_PAD, :] = jnp.zeros((CONF_PAD, CONF_W), F32)
    pad_scr[seq + CONF_PAD:seq + 2 * CONF_PAD, :] = jnp.zeros((CONF_PAD, CONF_W), F32)
    pad_scr[CONF_PAD:seq + CONF_PAD, :] = v_ref[...]
    lo = CONF_PAD - CONF_K // 2

    def body(c, _):
        t0 = pl.multiple_of(c * q, q)
        big = pad_scr[pl.ds(t0, q + 2 * CONF_PAD), :]
        acc = jnp.zeros((q, CONF_W), F32) + b_ref[...]
        for k in range(CONF_K):
            acc = acc + big[lo + k:lo + k + q] * w_ref[k:k + 1, :]
        mu = jnp.mean(acc, axis=-1, keepdims=True)
        xc = acc - mu
        var = jnp.mean(xc * xc, axis=-1, keepdims=True)
        o_ref[pl.ds(t0, q), :] = jax.nn.silu(xc * lax.rsqrt(var + EPS) * g_ref[...] + bt_ref[...])
        return 0

    lax.fori_loop(0, seq // q, body, 0)


def _conf_call(v, w, nb, seq, row0):
    s0 = row0 // seq
    full = lambda a: pl.BlockSpec(a.shape, lambda b: (0,) * a.ndim)
    return pl.pallas_call(
        functools.partial(_conf_kernel, seq=seq), grid=(nb,),
        in_specs=[pl.BlockSpec((seq, CONF_W), lambda b: (s0 + b, 0))] + [full(a) for a in w],
        out_specs=pl.BlockSpec((seq, CONF_W), lambda b: (b, 0)),
        out_shape=jax.ShapeDtypeStruct((nb * seq, CONF_W), F32),
        scratch_shapes=[pltpu.VMEM((seq + 2 * CONF_PAD, CONF_W), F32)],
        compiler_params=_cparams("arbitrary"), name="conf",
    )(v, *w)


def _split_bf16(x):
    hi = x.astype(BF16)
    return hi, (x - hi.astype(F32)).astype(BF16)


def _outproj_kernel(xp_ref, xs_ref, ada_ref, s5p_ref, s5s_ref, mlap_ref, mlas_ref, ssdp_ref, ssds_ref, confp_ref,
                    confs_ref, wglu_ref, bglu_ref, wo_ref, npost_ref, npre_ref, wrh_ref, wrl_ref, br_ref, perm_ref,
                    x1_ref, h2_ref, gate_ref, *, n_first):
    ada = ada_ref[0]
    ys5 = _pick(s5p_ref, s5s_ref, n_first)
    a = jax.nn.gelu(_group_rows_to_chunk_rows([ys5[g].astype(F32) for g in range(S5_G)], perm_ref[...]))
    s5 = a * jax.nn.sigmoid(_bdot(a, wglu_ref[...]) + bglu_ref[...])
    mix = (_bdot(s5, wo_ref[0:256, :]) + _bdot(_pick(mlap_ref, mlas_ref, n_first), wo_ref[256:512, :])
           + _bdot(_pick(ssdp_ref, ssds_ref, n_first), wo_ref[512:768, :])
           + _bdot(_pick(confp_ref, confs_ref, n_first), wo_ref[768:1024, :]))
    x1 = _pick(xp_ref, xs_ref, n_first) + ada[2:3] * _rms(mix, npost_ref[...])
    x1_ref[...] = x1
    h2 = _rms(x1, npre_ref[...]) * (1.0 + ada[4:5]) + ada[3:4]
    h2_ref[...] = h2.astype(BF16)
    hi, lo = _split_bf16(h2)
    lg = (jnp.dot(hi, wrh_ref[...], preferred_element_type=F32) + jnp.dot(lo, wrh_ref[...], preferred_element_type=F32)
          + jnp.dot(hi, wrl_ref[...], preferred_element_type=F32) + br_ref[...])
    lane = lax.broadcasted_iota(jnp.int32, lg.shape, 1)
    neg = -jnp.inf
    big = jnp.int32(1 << 20)
    first = lambda hit: jnp.min(jnp.where(hit, lane, big), axis=-1, keepdims=True)
    glm = jnp.where((lane >= MOE_E) & (lane < MOE_E + MOE_GROUPS), lg, neg)
    gmax = jnp.max(glm, axis=-1, keepdims=True)
    p_group = 1.0 / jnp.sum(jnp.exp(glm - gmax), axis=-1, keepdims=True)
    gsel = first(glm == gmax) - MOE_E
    elm = jnp.where((lane < MOE_E) & ((lane // MOE_PER_GROUP) == gsel), lg, neg)
    v1 = jnp.max(elm, axis=-1, keepdims=True)
    i1 = first(elm == v1)
    elm2 = jnp.where(lane == i1, neg, elm)
    v2 = jnp.max(elm2, axis=-1, keepdims=True)
    i2 = first(elm2 == v2)
    e2 = jnp.exp(v2 - v1)
    w1 = p_group / (1.0 + e2)
    w2 = p_group * e2 / (1.0 + e2)
    gate_ref[...] = jnp.where(lane == i1, w1, 0.0) + jnp.where(lane == i2, w2, 0.0)


def _outproj_call(x, ada_t, ys5, ymla, yssd, yconf, w):
    t = x[0].shape[0] + x[1].shape[0]
    n_first = x[0].shape[0] // TM
    row = lambda n: pl.BlockSpec((TM, n), lambda i: (i, 0))
    full = lambda a: pl.BlockSpec(a.shape, lambda i: (0,) * a.ndim)
    w = tuple(w) + (_chunk_perm(),)
    pair = lambda n: _two_group_specs((TM, n), n_first)
    return pl.pallas_call(
        functools.partial(_outproj_kernel, n_first=n_first), grid=(t // TM,),
        in_specs=pair(D_MODEL) + [pl.BlockSpec((1, 8, D_MODEL), lambda i: (i, 0, 0))]
        + _two_group_specs((TM // S5_CHUNK, S5_W), n_first, lead=(S5_G,)) + pair(256) + pair(256) + pair(256)
        + [full(a) for a in w],
        out_specs=[row(D_MODEL), row(D_MODEL), row(LANE)],
        out_shape=[jax.ShapeDtypeStruct((t, D_MODEL), F32), jax.ShapeDtypeStruct((t, D_MODEL), BF16),
                   jax.ShapeDtypeStruct((t, LANE), F32)],
        compiler_params=_cparams("arbitrary"), name="outproj",
    )(*x, ada_t, *ys5, *ymla, *yssd, *yconf, *w)


def _moe_kernel(h_ref, gate_ref, x1_ref, ada_ref, wg_ref, wu_ref, wd_ref, npost_ref, op_ref, os_ref, acc_ref,
                *, n_first):
    i = pl.program_id(0)
    e = pl.program_id(1)

    @pl.when(e == 0)
    def _zero():
        acc_ref[...] = jnp.zeros_like(acc_ref)

    h = h_ref[...]
    lane = lax.broadcasted_iota(jnp.int32, gate_ref.shape, 1)
    ge = jnp.sum(jnp.where(lane == e, gate_ref[...], 0.0), axis=-1, keepdims=True)
    hid = (jax.nn.silu(jnp.dot(h, wg_ref[0], preferred_element_type=F32))
           * jnp.dot(h, wu_ref[0], preferred_element_type=F32) * ge)
    acc_ref[...] += jnp.dot(hid.astype(BF16), wd_ref[0], preferred_element_type=F32)

    def result():
        return x1_ref[...] + ada_ref[0][5:6] * _rms(acc_ref[...], npost_ref[...])

    @pl.when((e == MOE_E - 1) & (i < n_first))
    def _finish_first():
        op_ref[...] = result()

    @pl.when((e == MOE_E - 1) & (i >= n_first))
    def _finish_second():
        os_ref[...] = result()


def _moe_call(h2, gate, x1, ada_t, wg, wu, wd, npost, t_first):
    t = h2.shape[0]
    tm = MOE_TM
    n_first = t_first // tm
    row = lambda n: pl.BlockSpec((tm, n), lambda i, e: (i, 0))
    return pl.pallas_call(
        functools.partial(_moe_kernel, n_first=n_first), grid=(t // tm, MOE_E),
        in_specs=[row(D_MODEL), row(LANE), row(D_MODEL),
                  pl.BlockSpec((1, 8, D_MODEL), lambda i, e: (i * (tm // TM), 0, 0)),
                  pl.BlockSpec((1, D_MODEL, MOE_HID), lambda i, e: (e, 0, 0)),
                  pl.BlockSpec((1, D_MODEL, MOE_HID), lambda i, e: (e, 0, 0)),
                  pl.BlockSpec((1, MOE_HID, D_MODEL), lambda i, e: (e, 0, 0)),
                  pl.BlockSpec(npost.shape, lambda i, e: (0, 0))],
        out_specs=_two_group_specs((tm, D_MODEL), n_first),
        out_shape=[jax.ShapeDtypeStruct((t_first, D_MODEL), F32), jax.ShapeDtypeStruct((t - t_first, D_MODEL), F32)],
        scratch_shapes=[pltpu.VMEM((tm, D_MODEL), F32)],
        compiler_params=_cparams("arbitrary", "arbitrary"), name="moe",
    )(h2, gate, x1, ada_t, wg, wu, wd, npost)


def _pack_w_in(w):
    d = w.shape[0]
    z32, z64 = jnp.zeros((d, 32), F32), jnp.zeros((d, 64), F32)
    kr = w[:, OFF_MLA_KR:OFF_SSD_Z]
    dtw = jnp.pad(w[:, OFF_SSD_DT:OFF_CONF], ((0, 0), (0, LANE - SSD_H)))
    cols = [w[:, OFF_S5:OFF_MLA_Q], w[:, OFF_MLA_Q:OFF_MLA_KV], w[:, OFF_MLA_KV:OFF_MLA_KR],
            z64, kr, z32, z64, _swap8(kr), z32,
            w[:, OFF_SSD_Z:OFF_SSD_XBC], w[:, OFF_SSD_XBC:OFF_SSD_DT], dtw, w[:, OFF_CONF:IN_COLS]]
    return jnp.concatenate(cols, -1).astype(BF16)


def _row(v):
    return v.reshape(1, -1)


def kernel(x_prompt, x_sample, cache_mla_ckv, cache_mla_krope, state_s5, state_ssd, c, c_ctx, w_ada, b_ada, norm_pre1, norm_post1, norm_pre2, norm_post2, w_in, w_out, s5_lam_re, s5_lam_im, s5_log_dt, s5_b_re, s5_b_im, s5_c_re, s5_c_im, s5_d, s5_w_glu, s5_b_glu, mla_qnorm, mla_kvnorm, mla_wuq, mla_wuk, mla_wuv, ssd_conv_w, ssd_conv_b, ssd_dt_bias, ssd_a_log, ssd_d, ssd_norm_w, conf_dw_w, conf_dw_b, conf_ln_g, conf_ln_b, moe_wg, moe_bg, moe_we, moe_be, moe_w_gate, moe_w_up, moe_w_down):
    bp, lp, d = x_prompt.shape
    bs, ls, _ = x_sample.shape
    tp, ts = bp * lp, bs * ls
    x = (x_prompt.reshape(tp, d), x_sample.reshape(ts, d))

    crows = jnp.concatenate([c_ctx[None], c, jnp.zeros((8 - 1 - bs, d), F32)], 0)
    ada = _ada_call(crows, w_ada, b_ada)
    tile_row = np.concatenate([np.zeros(tp // TM, np.int32), 1 + np.repeat(np.arange(bs, dtype=np.int32), ls // TM)])

    rope_p = _rope_tables(lp, rotate=False)
    rope_s = _rope_tables(ls, rotate=True)
    zeros_s5 = tuple(jnp.zeros((bp, S5_G * 2 * S5_P), F32) for _ in range(4))
    zeros_ssd = jnp.zeros((bp, 2, SSD_H * SSD_P, SSD_N), F32)
    pad_kr = lambda kr: jnp.pad(kr, ((0, 0), (0, 0), (MLA_NOPE, LANE - MLA_NOPE - MLA_ROPE)))
    moe_w = [w.astype(BF16) for w in (moe_w_gate, moe_w_up, moe_w_down)]

    ckv_l, kr_l, s5_l, ssd_l = [], [], [], []
    for l in range(DEPTH):
        ada_t = jnp.pad(ada[l][tile_row].reshape(-1, 6, d), ((0, 0), (0, 2), (0, 0)))
        u, cq, ckv, krab, z, xbc, dt, v = _inproj_call(
            *x, ada_t, _row(norm_pre1[l]), _pack_w_in(w_in[l]), _row(mla_qnorm[l]), _row(mla_kvnorm[l]))

        tabs = _s5_tables(s5_lam_re[l], s5_lam_im[l], s5_log_dt[l], s5_b_re[l], s5_b_im[l], s5_c_re[l], s5_c_im[l],
                          s5_d[l])
        ncp = tp // S5_CHUNK
        yp, stf, stb = _s5_call(_s5_rows_in(u[:, :ncp], bp), tabs, zeros_s5, bp)
        ys, _, _ = _s5_call(_s5_rows_in(u[:, ncp:], bs), tabs, _s5_state_in(state_s5[:, l]), bs)
        y_s5 = (_s5_rows_out(yp, bp), _s5_rows_out(ys, bs))
        s5_l.append(_s5_state_out(stf, stb))

        mw = _mla_weights(mla_wuq[l], mla_wuk[l], mla_wuv[l])
        y_mla = (_mla_call(cq, ckv, krab, rope_p, None, mw, bp, lp, 0),
                 _mla_call(cq, ckv, krab, rope_s, (cache_mla_ckv[:, l], pad_kr(cache_mla_krope[:, l])), mw, bs, ls, tp))
        ckv_l.append(ckv[:tp].reshape(bp, lp, MLA_KVLORA))
        kr_l.append(krab[:tp, MLA_NOPE:MLA_NOPE + MLA_ROPE].reshape(bp, lp, MLA_ROPE))

        sw = _ssd_weights(ssd_conv_w[l], ssd_conv_b[l], ssd_dt_bias[l], ssd_a_log[l], ssd_d[l], ssd_norm_w[l])
        yssd_p, st_ssd = _ssd_call(xbc, z, dt, sw, zeros_ssd, bp, lp, 0)
        yssd_s, _ = _ssd_call(xbc, z, dt, sw, state_ssd[:, l].reshape(bs, 2, SSD_H * SSD_P, SSD_N), bs, ls, tp)
        ssd_l.append(st_ssd.reshape(bp, 2, SSD_H, SSD_P, SSD_N))

        cw = (jnp.concatenate([conf_dw_w[l], jnp.zeros((1, CONF_W), F32)], 0), _row(conf_dw_b[l]),
              _row(conf_ln_g[l]), _row(conf_ln_b[l]))
        y_conf = (_conf_call(v, cw, bp, lp, 0), _conf_call(v, cw, bs, ls, tp))

        wr = jnp.pad(jnp.concatenate([moe_we[l], moe_wg[l]], -1), ((0, 0), (0, LANE - MOE_E - MOE_GROUPS)))
        wrh, wrl = _split_bf16(wr)
        br = jnp.pad(jnp.concatenate([moe_be[l], moe_bg[l]]), (0, LANE - MOE_E - MOE_GROUPS)).reshape(1, LANE)
        ow = (s5_w_glu[l].astype(BF16), _row(s5_b_glu[l]), w_out[l].astype(BF16), _row(norm_post1[l]),
              _row(norm_pre2[l]), wrh, wrl, br)
        x1, h2, gate = _outproj_call(x, ada_t, y_s5, y_mla, (yssd_p, yssd_s), y_conf, ow)

        x = _moe_call(h2, gate, x1, ada_t, moe_w[0][l], moe_w[1][l], moe_w[2][l], _row(norm_post2[l]), tp)

    return (x[0].reshape(bp, lp, d), x[1].reshape(bs, ls, d),
            jnp.stack(ckv_l, 1), jnp.stack(kr_l, 1), jnp.stack(s5_l, 1), jnp.stack(ssd_l, 1))
```

```python
import functools
import math

import jax
import jax.numpy as jnp
import numpy as np
from jax import lax
from jax.experimental import pallas as pl
from jax.experimental.pallas import tpu as pltpu

F32 = jnp.float32
BF16 = jnp.bfloat16

D_MODEL = 1024
DEPTH = 2
GRID_W = 64
EPS = 1e-6

S5_W = 256
S5_GSIZE = 16
S5_G = 16
S5_P = 64
S5_CHUNK = 16

MLA_H = 4
MLA_NOPE = 64
MLA_ROPE = 32
MLA_V = 64
MLA_QLORA = 256
MLA_KVLORA = 128
ROPE_BASE = 10000.0
MLA_TQ = 256
MLA_KC = 256

SSD_W = 256
SSD_P = 64
SSD_H = 4
SSD_G = 2
SSD_N = 64
SSD_CHUNK = 128
SSD_XBC = 512

CONF_W = 256
CONF_K = 31
CONF_Q = 128
CONF_PAD = 16
CONF_SEG = 1024

MOE_GROUPS = 4
MOE_PER_GROUP = 4
MOE_E = 16
MOE_HID = 256
MOE_TM = 1024

OFF_S5 = 0
OFF_MLA_Q = 256
OFF_MLA_KV = 512
OFF_MLA_KR = 640
OFF_SSD_Z = 672
OFF_SSD_XBC = 928
OFF_SSD_DT = 1440
OFF_CONF = 1444
IN_COLS = 1956

LANE = 128
SUBLANE = 8
TM = 256
P_U, P_Q, P_KV, P_KRA, P_KRB, P_Z, P_XBC, P_DT, P_CA, P_CB, P_END = (
    0, 256, 512, 640, 768, 896, 1152, 1664, 1792, 2048, 2304)

VMEM_LIMIT = 56 * 1024 * 1024


def _bdot(a, b):
    return jnp.dot(a.astype(BF16), b.astype(BF16), preferred_element_type=F32)


def _bdot_nt(a, b):
    return lax.dot_general(a.astype(BF16), b.astype(BF16), (((1,), (1,)), ((), ())), preferred_element_type=F32)


def _rms(x, g):
    return x * lax.rsqrt(jnp.mean(x * x, axis=-1, keepdims=True) + EPS) * g


def _cparams(*sem):
    return pltpu.CompilerParams(dimension_semantics=sem if sem else None, vmem_limit_bytes=VMEM_LIMIT)


def _ada_kernel(c_ref, w_ref, b_ref, o_ref):
    o_ref[0] = _bdot(jax.nn.silu(c_ref[...]), w_ref[0]) + b_ref[0]


def _ada_call(crows, w_ada, b_ada):
    tn = 1024
    d6 = 6 * D_MODEL
    return pl.pallas_call(
        _ada_kernel, grid=(DEPTH, d6 // tn),
        in_specs=[pl.BlockSpec((8, D_MODEL), lambda l, j: (0, 0)),
                  pl.BlockSpec((1, D_MODEL, tn), lambda l, j: (l, 0, j)),
                  pl.BlockSpec((1, 1, tn), lambda l, j: (l, 0, j))],
        out_specs=pl.BlockSpec((1, 8, tn), lambda l, j: (l, 0, j)),
        out_shape=jax.ShapeDtypeStruct((DEPTH, 8, d6), F32),
        compiler_params=_cparams("arbitrary", "arbitrary"), name="ada",
    )(crows, w_ada, b_ada.reshape(DEPTH, 1, d6))


def _lane_block(shape):
    return lax.broadcasted_iota(jnp.int32, shape, 1) // S5_GSIZE


def _chunk_rows_to_group_rows(x, perm):
    r = jnp.dot(perm, x.astype(BF16), preferred_element_type=F32)
    blk = _lane_block((S5_CHUNK, S5_W))
    out = []
    for g in range(S5_G):
        acc = None
        for t in range(S5_CHUNK):
            src = r[t * S5_CHUNK:(t + 1) * S5_CHUNK, :]
            shift = ((t - g) * S5_GSIZE) % S5_W
            src = pltpu.roll(src, shift, 1) if shift else src
            acc = src if acc is None else jnp.where(blk == t, src, acc)
        out.append(acc)
    return out


def _group_rows_to_chunk_rows(ys, perm_t):
    blk = _lane_block((S5_CHUNK, S5_W))
    rows = []
    for t in range(S5_CHUNK):
        acc = None
        for g in range(S5_G):
            shift = ((g - t) * S5_GSIZE) % S5_W
            src = pltpu.roll(ys[g], shift, 1) if shift else ys[g]
            acc = src if acc is None else jnp.where(blk == g, src, acc)
        rows.append(acc)
    z = jnp.concatenate(rows, 0)
    return jnp.dot(perm_t, z.astype(BF16), preferred_element_type=F32)


def _pick(first, second, n_first):
    return jnp.where(pl.program_id(0) < n_first, first[...], second[...])


def _two_group_specs(block, n_first, lead=()):
    nl = len(lead)
    first = pl.BlockSpec(lead + block, lambda i, *_: (0,) * nl + (jnp.minimum(i, n_first - 1), 0))
    second = pl.BlockSpec(lead + block, lambda i, *_: (0,) * nl + (jnp.maximum(i - n_first, 0), 0))
    return [first, second]


def _inproj_kernel(xp_ref, xs_ref, ada_ref, npre_ref, w_ref, qn_ref, kvn_ref, perm_ref,
                   u_ref, cq_ref, ckv_ref, krab_ref, z_ref, xbc_ref, dt_ref, v_ref, *, n_first):
    ada = ada_ref[0]
    h = _rms(_pick(xp_ref, xs_ref, n_first), npre_ref[...]) * (1.0 + ada[1:2]) + ada[0:1]
    p = _bdot(h, w_ref[...])
    for g, ug in enumerate(_chunk_rows_to_group_rows(p[:, P_U:P_Q], perm_ref[...])):
        u_ref[g] = ug.astype(BF16)
    cq_ref[...] = _rms(p[:, P_Q:P_KV], qn_ref[...]).astype(BF16)
    ckv_ref[...] = _rms(p[:, P_KV:P_KRA], kvn_ref[...])
    krab_ref[...] = p[:, P_KRA:P_Z]
    z_ref[...] = p[:, P_Z:P_XBC]
    xbc_ref[...] = p[:, P_XBC:P_DT]
    dt_ref[...] = p[:, P_DT:P_CA]
    v_ref[...] = p[:, P_CA:P_CB] * jax.nn.sigmoid(p[:, P_CB:P_END])


def _chunk_perm():
    idx = np.arange(TM)
    src = (idx % S5_CHUNK) * S5_CHUNK + idx // S5_CHUNK
    return jnp.asarray(np.eye(TM, dtype=np.float32)[src], BF16)


def _inproj_call(xp, xs, ada_t, npre, w, qn, kvn):
    t = xp.shape[0] + xs.shape[0]
    n_first = xp.shape[0] // TM
    row = lambda n: pl.BlockSpec((TM, n), lambda i: (i, 0))
    full = lambda a: pl.BlockSpec(a.shape, lambda i: (0,) * a.ndim)
    widths = (256, 128, 256, 256, 512, 128, 256)
    dts = (BF16, F32, F32, F32, F32, F32, F32)
    perm = _chunk_perm()
    cpt = TM // S5_CHUNK
    return pl.pallas_call(
        functools.partial(_inproj_kernel, n_first=n_first), grid=(t // TM,),
        in_specs=_two_group_specs((TM, D_MODEL), n_first)
        + [pl.BlockSpec((1, 8, D_MODEL), lambda i: (i, 0, 0)), full(npre), full(w), full(qn), full(kvn), full(perm)],
        out_specs=[pl.BlockSpec((S5_G, cpt, S5_W), lambda i: (0, i, 0))] + [row(n) for n in widths],
        out_shape=[jax.ShapeDtypeStruct((S5_G, t // S5_CHUNK, S5_W), BF16)]
        + [jax.ShapeDtypeStruct((t, n), d) for n, d in zip(widths, dts)],
        compiler_params=_cparams("arbitrary"), name="inproj",
    )(xp, xs, ada_t, npre, w, qn, kvn, perm)


def _s5_kernel(u_ref, r_ref, vf_ref, vb_ref, a_ref, h0f_ref, h0fs_ref, h0b_ref, h0bs_ref,
               y_ref, stf_ref, stb_ref, sf, sfs, sb, sbs, *, nb, nc):
    gw = 2 * S5_P
    for g in range(S5_G):
        r = jnp.dot(u_ref[g], r_ref[g, :, S5_W:], preferred_element_type=F32)
        sl = slice(g * gw, (g + 1) * gw)
        sf[:, sl] = r[:, 0:128]
        sfs[:, sl] = r[:, 128:256]
        sb[:, sl] = r[:, 256:384]
        sbs[:, sl] = r[:, 384:512]
    arf, aif, aisf = a_ref[0:1], a_ref[1:2], a_ref[2:3]
    arb, aib, aisb = a_ref[3:4], a_ref[4:5], a_ref[5:6]
    cpt = 1 if nb % SUBLANE == 0 else SUBLANE // nb
    rows = cpt * nb
    nt = nc // cpt

    def body(i, carry):
        hf, hfs, hb, hbs = carry
        rf = pl.ds(pl.multiple_of(i * rows, rows), rows)
        rb = pl.ds(pl.multiple_of((nt - 1 - i) * rows, rows), rows)
        inc_f, inc_fs, inc_b, inc_bs = sf[rf, :], sfs[rf, :], sb[rb, :], sbs[rb, :]
        in_f, in_b = [None] * cpt, [None] * cpt
        for k in range(cpt):
            kf = slice(k * nb, (k + 1) * nb)
            kb = slice((cpt - 1 - k) * nb, (cpt - k) * nb)
            in_f[k] = hf
            in_b[cpt - 1 - k] = hb
            hf, hfs = arf * hf + aif * hfs + inc_f[kf], arf * hfs + aisf * hf + inc_fs[kf]
            hb, hbs = arb * hb + aib * hbs + inc_b[kb], arb * hbs + aisb * hb + inc_bs[kb]
        sf[rf, :] = in_f[0] if cpt == 1 else jnp.concatenate(in_f, 0)
        sb[rb, :] = in_b[0] if cpt == 1 else jnp.concatenate(in_b, 0)
        return hf, hfs, hb, hbs

    hf, _, hb, _ = lax.fori_loop(0, nt, body, (h0f_ref[...], h0fs_ref[...], h0b_ref[...], h0bs_ref[...]))
    stf_ref[...] = hf
    stb_ref[...] = hb
    for g in range(S5_G):
        sl = slice(g * gw, (g + 1) * gw)
        y = (jnp.dot(u_ref[g], r_ref[g, :, :S5_W], preferred_element_type=F32)
             + _bdot(sf[:, sl], vf_ref[g]) + _bdot(sb[:, sl], vb_ref[g]))
        y_ref[g] = y.astype(BF16)


def _s5_call(u, tabs, h0, nb):
    n = u.shape[1]
    r, vf, vb, avec = tabs
    sw = S5_G * 2 * S5_P
    st = jax.ShapeDtypeStruct((nb, sw), F32)
    return pl.pallas_call(
        functools.partial(_s5_kernel, nb=nb, nc=n // nb),
        out_shape=[jax.ShapeDtypeStruct(u.shape, BF16), st, st],
        scratch_shapes=[pltpu.VMEM((n, sw), F32) for _ in range(4)],
        compiler_params=_cparams(), name="s5",
    )(u, r, vf, vb, avec, *h0)


def _cpow(br, bi, e, nbits):
    pr = jnp.ones(e.shape, F32)
    pi = jnp.zeros(e.shape, F32)
    for k in range(nbits):
        bit = ((e >> k) & 1) == 1
        pr, pi = jnp.where(bit, pr * br - pi * bi, pr), jnp.where(bit, pr * bi + pi * br, pi)
        br, bi = br * br - bi * bi, 2.0 * br * bi
    return pr, pi


def _s5tab_kernel(prow_ref, pcol_ref, bt_ref, ct_ref, dsk_ref, etile_ref, r_ref, vf_ref, vb_ref, a_ref):
    q = S5_CHUNK
    hp = lax.Precision.HIGHEST

    def abar(lre, lim, ldt):
        dt = jnp.exp(ldt)
        mag = jnp.exp(lre * dt)
        return mag * jnp.cos(lim * dt), mag * jnp.sin(lim * dt)

    def per_step(x):
        return jnp.dot(x, etile_ref[...], precision=hp, preferred_element_type=F32)

    step_of_col = lax.broadcasted_iota(jnp.int32, (S5_P, S5_W), 1) // S5_GSIZE
    step_of_row = lax.broadcasted_iota(jnp.int32, (S5_W, S5_P), 0) // S5_GSIZE
    kcat, wtabs, arows = [], [], []
    for d in range(2):
        prow = prow_ref[d, 0]
        lre, lim = prow[0:1], prow[1:2]
        abr, abi = abar(lre, lim, prow[2:3])
        den = lre * lre + lim * lim
        fr = ((abr - 1.0) * lre + abi * lim) / den
        fi = (abi * lre - (abr - 1.0) * lim) / den
        btr, bti = bt_ref[d, 0, 0], bt_ref[d, 0, 1]
        bbr, bbi = fr * btr - fi * bti, fr * bti + fi * btr
        pcol = pcol_ref[d, 0]
        cbr, cbi = abar(pcol[:, 0:1], pcol[:, 1:2], pcol[:, 2:3])
        ctr, cti = per_step(ct_ref[d, 0, 0]), per_step(ct_ref[d, 0, 1])

        def c_times_power(e):
            pr, pi = _cpow(cbr, cbi, e, 5)
            return ctr * pr - cti * pi, ctr * pi + cti * pr

        vr, vi = c_times_power(step_of_col + 1 if d == 0 else q - step_of_col)
        (vf_ref if d == 0 else vb_ref)[0] = jnp.concatenate([vr, -vi], 0).astype(BF16)
        lr, li = c_times_power(step_of_col if d == 0 else q - 1 - step_of_col)
        kcat.append(jnp.dot(bbr, lr, precision=hp, preferred_element_type=F32)
                    - jnp.dot(bbi, li, precision=hp, preferred_element_type=F32))
        pr, pi = _cpow(abr, abi, q - 1 - step_of_row if d == 0 else step_of_row, 4)
        tbr, tbi = jnp.concatenate([bbr] * q, 0), jnp.concatenate([bbi] * q, 0)
        wr, wi = pr * tbr - pi * tbi, pr * tbi + pi * tbr
        wtabs += [wr, wi, wi, wr]
        ar, ai = _cpow(abr, abi, jnp.full((1, S5_P), q, jnp.int32), 5)
        arows += [jnp.concatenate([ar, ar], 1), jnp.concatenate([-ai, ai], 1), jnp.concatenate([ai, -ai], 1)]
    lane = lax.broadcasted_iota(jnp.int32, (S5_GSIZE, S5_W), 1)
    rows = []
    for s in range(q):
        shr = S5_GSIZE * s
        shl = S5_GSIZE * (q - 1 - s)
        f = jnp.where(lane >= shr, pltpu.roll(kcat[0], shr, 1), 0.0) if shr else kcat[0]
        b = jnp.where(lane < S5_W - shl, pltpu.roll(kcat[1], S5_W - shl, 1), 0.0) if shl else kcat[1]
        rows.append(f + b)
    ri = lax.broadcasted_iota(jnp.int32, (S5_W, S5_W), 0)
    ci = lax.broadcasted_iota(jnp.int32, (S5_W, S5_W), 1)
    tt = jnp.concatenate(rows, 0) + jnp.where(ri == ci, dsk_ref[0], 0.0)
    r_ref[0] = jnp.concatenate([tt] + wtabs, 1).astype(BF16)
    a_ref[...] = jnp.concatenate(arows + [jnp.zeros((2, 2 * S5_P), F32)], 0)


def _s5_tables(lam_re, lam_im, log_dt, b_re, b_im, c_re, c_im, dskip):
    zeros = jnp.zeros((2, S5_G, 5, S5_P), F32)
    prow = jnp.concatenate([lam_re[:, :, None], lam_im[:, :, None],
                            jnp.broadcast_to(log_dt[:, :, None, None], (2, S5_G, 1, S5_P)), zeros], 2)
    pcol = jnp.swapaxes(prow, 2, 3)
    bt = jnp.stack([jnp.swapaxes(b_re, 2, 3), jnp.swapaxes(b_im, 2, 3)], 2)
    ct = jnp.stack([jnp.swapaxes(c_re, 2, 3), jnp.swapaxes(c_im, 2, 3)], 2)
    dsk = jnp.tile(dskip.reshape(S5_G, 1, S5_GSIZE), (1, 1, S5_CHUNK))
    etile = jnp.asarray(np.tile(np.eye(S5_GSIZE, dtype=np.float32), (1, S5_CHUNK)))
    gblk = lambda a: pl.BlockSpec((2, 1) + a.shape[2:], lambda g: (0, g) + (0,) * (a.ndim - 2))
    sw = S5_G * 2 * S5_P
    return pl.pallas_call(
        _s5tab_kernel, grid=(S5_G,),
        in_specs=[gblk(prow), gblk(pcol), gblk(bt), gblk(ct), pl.BlockSpec((1, 1, S5_W), lambda g: (g, 0, 0)),
                  pl.BlockSpec(etile.shape, lambda g: (0, 0))],
        out_specs=[pl.BlockSpec((1, S5_W, 768), lambda g: (g, 0, 0)), pl.BlockSpec((1, 2 * S5_P, S5_W), lambda g: (g, 0, 0)),
                   pl.BlockSpec((1, 2 * S5_P, S5_W), lambda g: (g, 0, 0)), pl.BlockSpec((8, 2 * S5_P), lambda g: (0, g))],
        out_shape=[jax.ShapeDtypeStruct((S5_G, S5_W, 768), BF16), jax.ShapeDtypeStruct((S5_G, 2 * S5_P, S5_W), BF16),
                   jax.ShapeDtypeStruct((S5_G, 2 * S5_P, S5_W), BF16), jax.ShapeDtypeStruct((8, sw), F32)],
        compiler_params=_cparams("arbitrary"), name="s5tab",
    )(prow, pcol, bt, ct, dsk, etile)


def _s5_rows_in(u, nb):
    g, n, w = u.shape
    return jnp.swapaxes(u.reshape(g, nb, n // nb, w), 1, 2).reshape(g, n, w)


def _s5_rows_out(y, nb):
    g, n, w = y.shape
    return jnp.swapaxes(y.reshape(g, n // nb, nb, w), 1, 2).reshape(g, n, w)


def _s5_state_in(state):
    b = state.shape[0]
    sw = jnp.transpose(state, (0, 1, 2, 4, 3))
    plain = sw.reshape(b, 2, S5_G * 2 * S5_P)
    swapped = sw[:, :, :, ::-1, :].reshape(b, 2, S5_G * 2 * S5_P)
    return plain[:, 0], swapped[:, 0], plain[:, 1], swapped[:, 1]


def _s5_state_out(stf, stb):
    b = stf.shape[0]
    st = jnp.stack([stf, stb], 1).reshape(b, 2, S5_G, 2, S5_P)
    return jnp.transpose(st, (0, 1, 2, 4, 3))


def _mla_kernel(*refs, seq, lctx):
    if lctx:
        (cq_ref, ckv_ref, krab_ref, cosq_ref, sinq_ref, cosk_ref, sink_ref, cckv_ref, ckr_ref,
         wqa_ref, wqb_ref, wk_ref, wv_ref, o_ref, k_scr, v_scr, s_scr) = refs
    else:
        (cq_ref, ckv_ref, krab_ref, cosq_ref, sinq_ref, cosk_ref, sink_ref,
         wqa_ref, wqb_ref, wk_ref, wv_ref, o_ref, k_scr, v_scr, s_scr) = refs
    lk = lctx + seq
    rb = min(seq, 512)

    @pl.when(pl.program_id(1) == 0)
    def _build_keys():
        if lctx:
            ckv = cckv_ref[0]
            kn = _bdot(ckv, wk_ref[...])
            vn = _bdot(ckv, wv_ref[...])
            for h in range(MLA_H):
                k_scr[h, 0:lctx, :] = (kn[:, h * LANE:(h + 1) * LANE] + ckr_ref[0]).astype(BF16)
                v_scr[h, 0:lctx, :] = vn[:, h * LANE:(h + 1) * LANE].astype(BF16)

        def chunk(i, _):
            r0 = pl.multiple_of(i * rb, rb)
            rows = pl.ds(r0, rb)
            ckv = ckv_ref[rows, :]
            kn = _bdot(ckv, wk_ref[...])
            vn = _bdot(ckv, wv_ref[...])
            krab = krab_ref[rows, :]
            krx = krab[:, :LANE] * cosk_ref[rows, :] + krab[:, LANE:] * sink_ref[rows, :]
            dst = pl.ds(pl.multiple_of(lctx + r0, rb if lctx % rb == 0 else 256), rb)
            for h in range(MLA_H):
                k_scr[h, dst, :] = (kn[:, h * LANE:(h + 1) * LANE] + krx).astype(BF16)
                v_scr[h, dst, :] = vn[:, h * LANE:(h + 1) * LANE].astype(BF16)
            return 0

        lax.fori_loop(0, seq // rb, chunk, 0)

    cq = cq_ref[...]
    qa = _bdot(cq, wqa_ref[...])
    qb = _bdot(cq, wqb_ref[...])
    cosq, sinq = cosq_ref[...], sinq_ref[...]
    scale = 1.0 / math.sqrt(MLA_NOPE + MLA_ROPE)
    outs = []
    chunks = [slice(j * MLA_KC, (j + 1) * MLA_KC) for j in range(lk // MLA_KC)]
    for h in range(MLA_H):
        hs = slice(h * LANE, (h + 1) * LANE)
        qh = ((qa[:, hs] * cosq + qb[:, hs] * sinq) * scale).astype(BF16)
        mp = None
        for ks in chunks:
            s = _bdot_nt(qh, k_scr[h, ks, :])
            s_scr[:, ks] = s
            for t in range(MLA_KC // LANE):
                part = s[:, t * LANE:(t + 1) * LANE]
                mp = part if mp is None else jnp.maximum(mp, part)
        m = jnp.max(mp, axis=-1, keepdims=True)
        lp = jnp.zeros((MLA_TQ, LANE), F32)
        acc = jnp.zeros((MLA_TQ, LANE), F32)
        for ks in chunks:
            e = jnp.exp(s_scr[:, ks] - m)
            for t in range(MLA_KC // LANE):
                lp = lp + e[:, t * LANE:(t + 1) * LANE]
            acc = acc + _bdot(e, v_scr[h, ks, :])
        outs.append((acc / jnp.sum(lp, axis=-1, keepdims=True))[:, :MLA_V])
    o_ref[...] = jnp.concatenate(outs, axis=-1)


def _mla_call(cq, ckv, krab, rope, ctx, w, nb, seq, row0):
    lctx = 0 if ctx is None else ctx[0].shape[1]
    nq = seq // MLA_TQ
    q0, s0 = row0 // MLA_TQ, row0 // seq
    cosx, sinx = rope
    qrow = lambda n: pl.BlockSpec((MLA_TQ, n), lambda b, i: (q0 + b * nq + i, 0))
    srow = lambda n: pl.BlockSpec((seq, n), lambda b, i: (s0 + b, 0))
    full = lambda a: pl.BlockSpec(a.shape, lambda b, i: (0,) * a.ndim)
    in_specs = [qrow(256), srow(128), srow(256), pl.BlockSpec((MLA_TQ, LANE), lambda b, i: (i, 0)),
                pl.BlockSpec((MLA_TQ, LANE), lambda b, i: (i, 0)), full(cosx), full(sinx)]
    args = [cq, ckv, krab, cosx, sinx, cosx, sinx]
    if lctx:
        in_specs += [pl.BlockSpec((1, lctx, LANE), lambda b, i: (b, 0, 0))] * 2
        args += list(ctx)
    in_specs += [full(a) for a in w]
    args += list(w)
    lk = lctx + seq
    return pl.pallas_call(
        functools.partial(_mla_kernel, seq=seq, lctx=lctx), grid=(nb, nq),
        in_specs=in_specs,
        out_specs=pl.BlockSpec((MLA_TQ, 256), lambda b, i: (b * nq + i, 0)),
        out_shape=jax.ShapeDtypeStruct((nb * seq, 256), F32),
        scratch_shapes=[pltpu.VMEM((MLA_H, lk, LANE), BF16), pltpu.VMEM((MLA_H, lk, LANE), BF16),
                        pltpu.VMEM((MLA_TQ, lk), F32)],
        compiler_params=_cparams("arbitrary", "arbitrary"), name="mla",
    )(*args)


def _swap8(w):
    s = w.shape
    return w.reshape(s[:-1] + (s[-1] // 16, 2, 8))[..., ::-1, :].reshape(s)


def _mla_weights(wuq, wuk, wuv):
    z32 = jnp.zeros((MLA_QLORA, 32), F32)
    z64 = jnp.zeros((MLA_QLORA, 64), F32)
    k64 = jnp.zeros((MLA_KVLORA, 64), F32)
    qa, qb, wk, wv = [], [], [], []
    for h in range(MLA_H):
        rope_w = wuq[:, h, MLA_NOPE:]
        qa += [wuq[:, h, :MLA_NOPE], rope_w, z32]
        qb += [z64, _swap8(rope_w), z32]
        wk += [wuk[:, h], k64]
        wv += [wuv[:, h], k64]
    cat = lambda xs: jnp.concatenate(xs, -1).astype(BF16)
    return cat(qa), cat(qb), cat(wk), cat(wv)


def _rope_tables(seq, rotate):
    ones = jnp.ones((seq, MLA_NOPE), F32)
    z32 = jnp.zeros((seq, 32), F32)
    if not rotate:
        return (jnp.concatenate([ones, jnp.ones((seq, MLA_ROPE), F32), z32], -1), jnp.zeros((seq, LANE), F32))
    pos = jnp.arange(seq)
    row = (pos // GRID_W).astype(F32)
    col = (pos % GRID_W).astype(F32)
    half = MLA_ROPE // 2
    inv = ROPE_BASE ** (-jnp.arange(0, half, 2, dtype=F32) / half)
    ang = jnp.concatenate([row[:, None] * inv, col[:, None] * inv], -1)
    cos, sin = jnp.cos(ang), jnp.sin(ang)
    q = MLA_ROPE // 4
    cos32 = jnp.concatenate([cos[:, :q], cos[:, :q], cos[:, q:], cos[:, q:]], -1)
    sin32 = jnp.concatenate([-sin[:, :q], sin[:, :q], -sin[:, q:], sin[:, q:]], -1)
    return jnp.concatenate([ones, cos32, z32], -1), jnp.concatenate([jnp.zeros_like(ones), sin32, z32], -1)


def _cumsum_rows(x, reverse):
    n = x.shape[0]
    row = lax.broadcasted_iota(jnp.int32, x.shape, 0)
    s = 1
    while s < n:
        if reverse:
            x = x + jnp.where(row < n - s, pltpu.roll(x, n - s, 0), 0.0)
        else:
            x = x + jnp.where(row >= s, pltpu.roll(x, s, 0), 0.0)
        s *= 2
    return x


def _ssd_kernel(xbc_ref, z_ref, dt_ref, cw_ref, cb_ref, par_ref, dsk_ref, nw_ref, h0_ref,
                y_ref, st_ref, xc_scr, yf_scr, yb_scr, hf_scr, hb_scr, *, seq):
    q = SSD_CHUNK
    nc = seq // q
    row = lax.broadcasted_iota(jnp.int32, (q, 1), 0)

    def conv(c, _):
        t0 = pl.multiple_of(c * q, q)
        cur = xbc_ref[pl.ds(t0, q), :]
        prev8 = xbc_ref[pl.ds(pl.multiple_of(jnp.maximum(t0 - 8, 0), 8), 8), :]
        next8 = xbc_ref[pl.ds(pl.multiple_of(jnp.minimum(t0 + q, seq - 8), 8), 8), :]
        prow = jnp.where(c > 0, prev8[7:8], 0.0)
        nrow = jnp.where(c < nc - 1, next8[0:1], 0.0)
        up = jnp.where(row == 0, prow, pltpu.roll(cur, 1, 0))
        dn = jnp.where(row == q - 1, nrow, pltpu.roll(cur, q - 1, 0))
        acc = up * cw_ref[0:1] + cur * cw_ref[1:2] + dn * cw_ref[2:3] + cb_ref[...]
        xc_scr[pl.ds(t0, q), :] = jax.nn.silu(acc)
        return 0

    lax.fori_loop(0, nc, conv, 0)
    hf_scr[...] = h0_ref[0, 0]
    hb_scr[...] = h0_ref[0, 1]
    ii = lax.broadcasted_iota(jnp.int32, (q, q), 0)
    jj = lax.broadcasted_iota(jnp.int32, (q, q), 1)

    def one_chunk(c, d, h_scr, y_scr):
        t0 = pl.multiple_of(c * q, q)
        xc = xc_scr[pl.ds(t0, q), :]
        xs = xc[:, :SSD_W]
        xs_t = jnp.transpose(xs)
        dt = jax.nn.softplus(dt_ref[pl.ds(t0, q), :] + par_ref[d:d + 1])
        dta = dt * par_ref[2 + d:3 + d]
        cs = _cumsum_rows(dta, reverse=(d == 1))
        cs_t = jnp.transpose(cs)
        dt_t = jnp.transpose(dt)
        edge = cs[q - 1:q] if d == 0 else cs[0:1]
        wts = jnp.exp(edge - cs) * dt
        ecs = jnp.exp(cs)
        mask = (ii >= jj) if d == 0 else (ii <= jj)
        for g in range(SSD_G):
            bm = xc[:, SSD_W + g * SSD_N:SSD_W + (g + 1) * SSD_N]
            cm = xc[:, SSD_W + SSD_G * SSD_N + g * SSD_N:SSD_W + SSD_G * SSD_N + (g + 1) * SSD_N]
            cbm = _bdot_nt(cm, bm)
            for hh in range(SSD_H // SSD_G):
                h = g * (SSD_H // SSD_G) + hh
                ps = slice(h * SSD_P, (h + 1) * SSD_P)
                lmat = jnp.exp(jnp.where(mask, cs[:, h:h + 1] - cs_t[h:h + 1, :], -jnp.inf))
                y = _bdot(cbm * lmat * dt_t[h:h + 1, :], xs[:, ps])
                h_in = h_scr[ps, :]
                y = y + _bdot_nt(cm, h_in) * ecs[:, h:h + 1]
                y_scr[pl.ds(t0, q), ps] = y
                st = _bdot(xs_t[ps, :], bm * wts[:, h:h + 1])
                h_scr[ps, :] = jnp.exp(edge[:, h:h + 1]) * h_in + st

    def chunks(c, _):
        one_chunk(c, 0, hf_scr, yf_scr)
        one_chunk(nc - 1 - c, 1, hb_scr, yb_scr)
        return 0

    lax.fori_loop(0, nc, chunks, 0)
    st_ref[0, 0] = hf_scr[...]
    st_ref[0, 1] = hb_scr[...]

    def finish(c, _):
        rows = pl.ds(pl.multiple_of(c * q, q), q)
        y = yf_scr[rows, :] + yb_scr[rows, :] + dsk_ref[...] * xc_scr[rows, 0:SSD_W]
        y_ref[rows, :] = _rms(y * jax.nn.silu(z_ref[rows, :]), nw_ref[...])
        return 0

    lax.fori_loop(0, nc, finish, 0)


def _ssd_call(xbc, z, dt, w, h0, nb, seq, row0):
    s0 = row0 // seq
    srow = lambda n: pl.BlockSpec((seq, n), lambda b: (s0 + b, 0))
    full = lambda a: pl.BlockSpec(a.shape, lambda b: (0,) * a.ndim)
    hp = SSD_H * SSD_P
    st_spec = pl.BlockSpec((1, 2, hp, SSD_N), lambda b: (b, 0, 0, 0))
    return pl.pallas_call(
        functools.partial(_ssd_kernel, seq=seq), grid=(nb,),
        in_specs=[srow(SSD_XBC), srow(SSD_W), srow(LANE)] + [full(a) for a in w] + [st_spec],
        out_specs=[pl.BlockSpec((seq, SSD_W), lambda b: (b, 0)), st_spec],
        out_shape=[jax.ShapeDtypeStruct((nb * seq, SSD_W), F32), jax.ShapeDtypeStruct((nb, 2, hp, SSD_N), F32)],
        scratch_shapes=[pltpu.VMEM((seq, SSD_XBC), F32), pltpu.VMEM((seq, SSD_W), F32), pltpu.VMEM((seq, SSD_W), F32),
                        pltpu.VMEM((hp, SSD_N), F32), pltpu.VMEM((hp, SSD_N), F32)],
        compiler_params=_cparams("arbitrary"), name="ssd",
    )(xbc, z, dt, *w, h0)


def _ssd_weights(conv_w, conv_b, dt_bias, a_log, dskip, norm_w):
    pad = lambda x: jnp.pad(x, ((0, 0), (0, LANE - SSD_H)))
    par = jnp.concatenate([pad(dt_bias), pad(-jnp.exp(a_log)), jnp.zeros((4, LANE), F32)], 0)
    cw = jnp.concatenate([conv_w, jnp.zeros((5, SSD_XBC), F32)], 0)
    return (cw, conv_b.reshape(1, SSD_XBC), par, jnp.repeat(dskip, SSD_P).reshape(1, SSD_W),
            norm_w.reshape(1, SSD_W))


def _conf_kernel(v_ref, w_ref, b_ref, g_ref, bt_ref, o_ref, pad_scr, sh_scr, *, seq, seg):
    q = CONF_Q
    halo = 2 * CONF_PAD
    tail = pad_scr.shape[0] - seq - CONF_PAD
    pad_scr[0:CONF_PAD, :] = jnp.zeros((CONF_PAD, CONF_W), F32)
    pad_scr[seq + CONF_PAD:, :] = jnp.zeros((tail, CONF_W), F32)
    pad_scr[CONF_PAD:seq + CONF_PAD, :] = v_ref[...]
    lo = CONF_PAD - CONF_K // 2
    pieces = [(j, min(LANE, seg + halo - j)) for j in range(0, seg + halo, LANE)]

    def segment(s, _):
        s0 = pl.multiple_of(s * seg, seg)
        for j0, n in pieces:
            big = pad_scr[pl.ds(s0 + j0, n + SUBLANE), :]
            for r in range(SUBLANE):
                sh_scr[r, j0:j0 + n, :] = big[r:r + n]

        def chunk(c, _):
            t0 = pl.multiple_of(c * q, q)
            acc = jnp.zeros((q, CONF_W), F32) + b_ref[...]
            for k in range(CONF_K):
                r, off = (lo + k) % SUBLANE, (lo + k) // SUBLANE * SUBLANE
                wk = pltpu.repeat(w_ref[k * SUBLANE:(k + 1) * SUBLANE, :], q // SUBLANE, axis=0)
                acc = acc + sh_scr[r, pl.ds(pl.multiple_of(t0 + off, SUBLANE), q), :] * wk
            mu = jnp.mean(acc, axis=-1, keepdims=True)
            xc = acc - mu
            var = jnp.mean(xc * xc, axis=-1, keepdims=True)
            o_ref[pl.ds(s0 + t0, q), :] = jax.nn.silu(xc * lax.rsqrt(var + EPS) * g_ref[...] + bt_ref[...])
            return 0

        lax.fori_loop(0, seg // q, chunk, 0)
        return 0

    lax.fori_loop(0, seq // seg, segment, 0)


def _conf_call(v, w, nb, seq, row0):
    s0 = row0 // seq
    full = lambda a: pl.BlockSpec(a.shape, lambda b: (0,) * a.ndim)
    seg = min(seq, CONF_SEG)
    halo = 2 * CONF_PAD
    return pl.pallas_call(
        functools.partial(_conf_kernel, seq=seq, seg=seg), grid=(nb,),
        in_specs=[pl.BlockSpec((seq, CONF_W), lambda b: (s0 + b, 0))] + [full(a) for a in w],
        out_specs=pl.BlockSpec((seq, CONF_W), lambda b: (b, 0)),
        out_shape=jax.ShapeDtypeStruct((nb * seq, CONF_W), F32),
        scratch_shapes=[pltpu.VMEM((seq + halo + 2 * SUBLANE, CONF_W), F32),
                        pltpu.VMEM((SUBLANE, seg + halo, CONF_W), F32)],
        compiler_params=_cparams("arbitrary"), name="conf",
    )(v, *w)


def _split_bf16(x):
    hi = x.astype(BF16)
    return hi, (x - hi.astype(F32)).astype(BF16)


def _outproj_kernel(xp_ref, xs_ref, ada_ref, s5p_ref, s5s_ref, mlap_ref, mlas_ref, ssdp_ref, ssds_ref, confp_ref,
                    confs_ref, wglu_ref, bglu_ref, wo_ref, npost_ref, npre_ref, wrh_ref, wrl_ref, br_ref, perm_ref,
                    x1_ref, h2_ref, gate_ref, *, n_first):
    ada = ada_ref[0]
    ys5 = _pick(s5p_ref, s5s_ref, n_first)
    a = jax.nn.gelu(_group_rows_to_chunk_rows([ys5[g].astype(F32) for g in range(S5_G)], perm_ref[...]))
    s5 = a * jax.nn.sigmoid(_bdot(a, wglu_ref[...]) + bglu_ref[...])
    mix = (_bdot(s5, wo_ref[0:256, :]) + _bdot(_pick(mlap_ref, mlas_ref, n_first), wo_ref[256:512, :])
           + _bdot(_pick(ssdp_ref, ssds_ref, n_first), wo_ref[512:768, :])
           + _bdot(_pick(confp_ref, confs_ref, n_first), wo_ref[768:1024, :]))
    x1 = _pick(xp_ref, xs_ref, n_first) + ada[2:3] * _rms(mix, npost_ref[...])
    x1_ref[...] = x1
    h2 = _rms(x1, npre_ref[...]) * (1.0 + ada[4:5]) + ada[3:4]
    h2_ref[...] = h2.astype(BF16)
    hi, lo = _split_bf16(h2)
    lg = (jnp.dot(hi, wrh_ref[...], preferred_element_type=F32) + jnp.dot(lo, wrh_ref[...], preferred_element_type=F32)
          + jnp.dot(hi, wrl_ref[...], preferred_element_type=F32) + br_ref[...])
    lane = lax.broadcasted_iota(jnp.int32, lg.shape, 1)
    neg = -jnp.inf
    big = jnp.int32(1 << 20)
    first = lambda hit: jnp.min(jnp.where(hit, lane, big), axis=-1, keepdims=True)
    glm = jnp.where((lane >= MOE_E) & (lane < MOE_E + MOE_GROUPS), lg, neg)
    gmax = jnp.max(glm, axis=-1, keepdims=True)
    p_group = 1.0 / jnp.sum(jnp.exp(glm - gmax), axis=-1, keepdims=True)
    gsel = first(glm == gmax) - MOE_E
    elm = jnp.where((lane < MOE_E) & ((lane // MOE_PER_GROUP) == gsel), lg, neg)
    v1 = jnp.max(elm, axis=-1, keepdims=True)
    i1 = first(elm == v1)
    elm2 = jnp.where(lane == i1, neg, elm)
    v2 = jnp.max(elm2, axis=-1, keepdims=True)
    i2 = first(elm2 == v2)
    e2 = jnp.exp(v2 - v1)
    w1 = p_group / (1.0 + e2)
    w2 = p_group * e2 / (1.0 + e2)
    gate_ref[...] = jnp.where(lane == i1, w1, 0.0) + jnp.where(lane == i2, w2, 0.0)


def _outproj_call(x, ada_t, ys5, ymla, yssd, yconf, w):
    t = x[0].shape[0] + x[1].shape[0]
    n_first = x[0].shape[0] // TM
    row = lambda n: pl.BlockSpec((TM, n), lambda i: (i, 0))
    full = lambda a: pl.BlockSpec(a.shape, lambda i: (0,) * a.ndim)
    w = tuple(w) + (_chunk_perm(),)
    pair = lambda n: _two_group_specs((TM, n), n_first)
    return pl.pallas_call(
        functools.partial(_outproj_kernel, n_first=n_first), grid=(t // TM,),
        in_specs=pair(D_MODEL) + [pl.BlockSpec((1, 8, D_MODEL), lambda i: (i, 0, 0))]
        + _two_group_specs((TM // S5_CHUNK, S5_W), n_first, lead=(S5_G,)) + pair(256) + pair(256) + pair(256)
        + [full(a) for a in w],
        out_specs=[row(D_MODEL), row(D_MODEL), row(LANE)],
        out_shape=[jax.ShapeDtypeStruct((t, D_MODEL), F32), jax.ShapeDtypeStruct((t, D_MODEL), BF16),
                   jax.ShapeDtypeStruct((t, LANE), F32)],
        compiler_params=_cparams("arbitrary"), name="outproj",
    )(*x, ada_t, *ys5, *ymla, *yssd, *yconf, *w)


def _moe_kernel(h_ref, gate_ref, x1_ref, ada_ref, wg_ref, wu_ref, wd_ref, npost_ref, op_ref, os_ref, acc_ref,
                *, n_first):
    i = pl.program_id(0)
    e = pl.program_id(1)

    @pl.when(e == 0)
    def _zero():
        acc_ref[...] = jnp.zeros_like(acc_ref)

    h = h_ref[...]
    lane = lax.broadcasted_iota(jnp.int32, gate_ref.shape, 1)
    ge = jnp.sum(jnp.where(lane == e, gate_ref[...], 0.0), axis=-1, keepdims=True)
    hid = (jax.nn.silu(jnp.dot(h, wg_ref[0], preferred_element_type=F32))
           * jnp.dot(h, wu_ref[0], preferred_element_type=F32) * ge)
    acc_ref[...] += jnp.dot(hid.astype(BF16), wd_ref[0], preferred_element_type=F32)

    def result():
        return x1_ref[...] + ada_ref[0][5:6] * _rms(acc_ref[...], npost_ref[...])

    @pl.when((e == MOE_E - 1) & (i < n_first))
    def _finish_first():
        op_ref[...] = result()

    @pl.when((e == MOE_E - 1) & (i >= n_first))
    def _finish_second():
        os_ref[...] = result()


def _moe_call(h2, gate, x1, ada_t, wg, wu, wd, npost, t_first):
    t = h2.shape[0]
    tm = MOE_TM
    n_first = t_first // tm
    row = lambda n: pl.BlockSpec((tm, n), lambda i, e: (i, 0))
    return pl.pallas_call(
        functools.partial(_moe_kernel, n_first=n_first), grid=(t // tm, MOE_E),
        in_specs=[row(D_MODEL), row(LANE), row(D_MODEL),
                  pl.BlockSpec((1, 8, D_MODEL), lambda i, e: (i * (tm // TM), 0, 0)),
                  pl.BlockSpec((1, D_MODEL, MOE_HID), lambda i, e: (e, 0, 0)),
                  pl.BlockSpec((1, D_MODEL, MOE_HID), lambda i, e: (e, 0, 0)),
                  pl.BlockSpec((1, MOE_HID, D_MODEL), lambda i, e: (e, 0, 0)),
                  pl.BlockSpec(npost.shape, lambda i, e: (0, 0))],
        out_specs=_two_group_specs((tm, D_MODEL), n_first),
        out_shape=[jax.ShapeDtypeStruct((t_first, D_MODEL), F32), jax.ShapeDtypeStruct((t - t_first, D_MODEL), F32)],
        scratch_shapes=[pltpu.VMEM((tm, D_MODEL), F32)],
        compiler_params=_cparams("arbitrary", "arbitrary"), name="moe",
    )(h2, gate, x1, ada_t, wg, wu, wd, npost)


def _pack_w_in(w):
    d = w.shape[0]
    z32, z64 = jnp.zeros((d, 32), F32), jnp.zeros((d, 64), F32)
    kr = w[:, OFF_MLA_KR:OFF_SSD_Z]
    dtw = jnp.pad(w[:, OFF_SSD_DT:OFF_CONF], ((0, 0), (0, LANE - SSD_H)))
    cols = [w[:, OFF_S5:OFF_MLA_Q], w[:, OFF_MLA_Q:OFF_MLA_KV], w[:, OFF_MLA_KV:OFF_MLA_KR],
            z64, kr, z32, z64, _swap8(kr), z32,
            w[:, OFF_SSD_Z:OFF_SSD_XBC], w[:, OFF_SSD_XBC:OFF_SSD_DT], dtw, w[:, OFF_CONF:IN_COLS]]
    return jnp.concatenate(cols, -1).astype(BF16)


def _row(v):
    return v.reshape(1, -1)


def kernel(x_prompt, x_sample, cache_mla_ckv, cache_mla_krope, state_s5, state_ssd, c, c_ctx, w_ada, b_ada, norm_pre1, norm_post1, norm_pre2, norm_post2, w_in, w_out, s5_lam_re, s5_lam_im, s5_log_dt, s5_b_re, s5_b_im, s5_c_re, s5_c_im, s5_d, s5_w_glu, s5_b_glu, mla_qnorm, mla_kvnorm, mla_wuq, mla_wuk, mla_wuv, ssd_conv_w, ssd_conv_b, ssd_dt_bias, ssd_a_log, ssd_d, ssd_norm_w, conf_dw_w, conf_dw_b, conf_ln_g, conf_ln_b, moe_wg, moe_bg, moe_we, moe_be, moe_w_gate, moe_w_up, moe_w_down):
    bp, lp, d = x_prompt.shape
    bs, ls, _ = x_sample.shape
    tp, ts = bp * lp, bs * ls
    x = (x_prompt.reshape(tp, d), x_sample.reshape(ts, d))

    crows = jnp.concatenate([c_ctx[None], c, jnp.zeros((8 - 1 - bs, d), F32)], 0)
    ada = _ada_call(crows, w_ada, b_ada)
    tile_row = np.concatenate([np.zeros(tp // TM, np.int32), 1 + np.repeat(np.arange(bs, dtype=np.int32), ls // TM)])

    rope_p = _rope_tables(lp, rotate=False)
    rope_s = _rope_tables(ls, rotate=True)
    zeros_s5 = tuple(jnp.zeros((bp, S5_G * 2 * S5_P), F32) for _ in range(4))
    zeros_ssd = jnp.zeros((bp, 2, SSD_H * SSD_P, SSD_N), F32)
    pad_kr = lambda kr: jnp.pad(kr, ((0, 0), (0, 0), (MLA_NOPE, LANE - MLA_NOPE - MLA_ROPE)))
    moe_w = [w.astype(BF16) for w in (moe_w_gate, moe_w_up, moe_w_down)]

    ckv_l, kr_l, s5_l, ssd_l = [], [], [], []
    for l in range(DEPTH):
        ada_t = jnp.pad(ada[l][tile_row].reshape(-1, 6, d), ((0, 0), (0, 2), (0, 0)))
        u, cq, ckv, krab, z, xbc, dt, v = _inproj_call(
            *x, ada_t, _row(norm_pre1[l]), _pack_w_in(w_in[l]), _row(mla_qnorm[l]), _row(mla_kvnorm[l]))

        tabs = _s5_tables(s5_lam_re[l], s5_lam_im[l], s5_log_dt[l], s5_b_re[l], s5_b_im[l], s5_c_re[l], s5_c_im[l],
                          s5_d[l])
        ncp = tp // S5_CHUNK
        yp, stf, stb = _s5_call(_s5_rows_in(u[:, :ncp], bp), tabs, zeros_s5, bp)
        ys, _, _ = _s5_call(_s5_rows_in(u[:, ncp:], bs), tabs, _s5_state_in(state_s5[:, l]), bs)
        y_s5 = (_s5_rows_out(yp, bp), _s5_rows_out(ys, bs))
        s5_l.append(_s5_state_out(stf, stb))

        mw = _mla_weights(mla_wuq[l], mla_wuk[l], mla_wuv[l])
        y_mla = (_mla_call(cq, ckv, krab, rope_p, None, mw, bp, lp, 0),
                 _mla_call(cq, ckv, krab, rope_s, (cache_mla_ckv[:, l], pad_kr(cache_mla_krope[:, l])), mw, bs, ls, tp))
        ckv_l.append(ckv[:tp].reshape(bp, lp, MLA_KVLORA))
        kr_l.append(krab[:tp, MLA_NOPE:MLA_NOPE + MLA_ROPE].reshape(bp, lp, MLA_ROPE))

        sw = _ssd_weights(ssd_conv_w[l], ssd_conv_b[l], ssd_dt_bias[l], ssd_a_log[l], ssd_d[l], ssd_norm_w[l])
        yssd_p, st_ssd = _ssd_call(xbc, z, dt, sw, zeros_ssd, bp, lp, 0)
        yssd_s, _ = _ssd_call(xbc, z, dt, sw, state_ssd[:, l].reshape(bs, 2, SSD_H * SSD_P, SSD_N), bs, ls, tp)
        ssd_l.append(st_ssd.reshape(bp, 2, SSD_H, SSD_P, SSD_N))

        cw = (jnp.repeat(conf_dw_w[l], SUBLANE, axis=0), _row(conf_dw_b[l]),
              _row(conf_ln_g[l]), _row(conf_ln_b[l]))
        y_conf = (_conf_call(v, cw, bp, lp, 0), _conf_call(v, cw, bs, ls, tp))

        wr = jnp.pad(jnp.concatenate([moe_we[l], moe_wg[l]], -1), ((0, 0), (0, LANE - MOE_E - MOE_GROUPS)))
        wrh, wrl = _split_bf16(wr)
        br = jnp.pad(jnp.concatenate([moe_be[l], moe_bg[l]]), (0, LANE - MOE_E - MOE_GROUPS)).reshape(1, LANE)
        ow = (s5_w_glu[l].astype(BF16), _row(s5_b_glu[l]), w_out[l].astype(BF16), _row(norm_post1[l]),
              _row(norm_pre2[l]), wrh, wrl, br)
        x1, h2, gate = _outproj_call(x, ada_t, y_s5, y_mla, (yssd_p, yssd_s), y_conf, ow)

        x = _moe_call(h2, gate, x1, ada_t, moe_w[0][l], moe_w[1][l], moe_w[2][l], _row(norm_post2[l]), tp)

    return (x[0].reshape(bp, lp, d), x[1].reshape(bs, ls, d),
            jnp.stack(ckv_l, 1), jnp.stack(kr_l, 1), jnp.stack(s5_l, 1), jnp.stack(ssd_l, 1))
```

```python
import functools
import math

import jax
import jax.numpy as jnp
import numpy as np
from jax import lax
from jax.experimental import pallas as pl
from jax.experimental.pallas import tpu as pltpu

F32 = jnp.float32
BF16 = jnp.bfloat16

D_MODEL = 1024
DEPTH = 2
GRID_W = 64
EPS = 1e-6

S5_W = 256
S5_GSIZE = 16
S5_G = 16
S5_P = 64
S5_CHUNK = 16

MLA_H = 4
MLA_NOPE = 64
MLA_ROPE = 32
MLA_V = 64
MLA_QLORA = 256
MLA_KVLORA = 128
ROPE_BASE = 10000.0
MLA_TQ = 256
MLA_KC = 256

SSD_W = 256
SSD_P = 64
SSD_H = 4
SSD_G = 2
SSD_N = 64
SSD_CHUNK = 128
SSD_XBC = 512

CONF_W = 256
CONF_K = 31
CONF_Q = 128
CONF_PAD = 16
CONF_SEG = 1024

MOE_GROUPS = 4
MOE_PER_GROUP = 4
MOE_E = 16
MOE_HID = 256
MOE_TM = 1024

OFF_S5 = 0
OFF_MLA_Q = 256
OFF_MLA_KV = 512
OFF_MLA_KR = 640
OFF_SSD_Z = 672
OFF_SSD_XBC = 928
OFF_SSD_DT = 1440
OFF_CONF = 1444
IN_COLS = 1956

LANE = 128
SUBLANE = 8
TM = 256
P_U, P_Q, P_KV, P_KRA, P_KRB, P_Z, P_XBC, P_DT, P_CA, P_CB, P_END = (
    0, 256, 512, 640, 768, 896, 1152, 1664, 1792, 2048, 2304)

VMEM_LIMIT = 56 * 1024 * 1024


def _bdot(a, b):
    return jnp.dot(a.astype(BF16), b.astype(BF16), preferred_element_type=F32)


def _bdot_nt(a, b):
    return lax.dot_general(a.astype(BF16), b.astype(BF16), (((1,), (1,)), ((), ())), preferred_element_type=F32)


def _rms(x, g):
    return x * lax.rsqrt(jnp.mean(x * x, axis=-1, keepdims=True) + EPS) * g


def _cparams(*sem):
    return pltpu.CompilerParams(dimension_semantics=sem if sem else None, vmem_limit_bytes=VMEM_LIMIT)


def _ada_kernel(c_ref, w_ref, b_ref, o_ref):
    o_ref[0] = _bdot(jax.nn.silu(c_ref[...]), w_ref[0]) + b_ref[0]


def _ada_call(crows, w_ada, b_ada):
    tn = 1024
    d6 = 6 * D_MODEL
    return pl.pallas_call(
        _ada_kernel, grid=(DEPTH, d6 // tn),
        in_specs=[pl.BlockSpec((8, D_MODEL), lambda l, j: (0, 0)),
                  pl.BlockSpec((1, D_MODEL, tn), lambda l, j: (l, 0, j)),
                  pl.BlockSpec((1, 1, tn), lambda l, j: (l, 0, j))],
        out_specs=pl.BlockSpec((1, 8, tn), lambda l, j: (l, 0, j)),
        out_shape=jax.ShapeDtypeStruct((DEPTH, 8, d6), F32),
        compiler_params=_cparams("arbitrary", "arbitrary"), name="ada",
    )(crows, w_ada, b_ada.reshape(DEPTH, 1, d6))


def _lane_block(shape):
    return lax.broadcasted_iota(jnp.int32, shape, 1) // S5_GSIZE


def _chunk_rows_to_group_rows(x, perm):
    r = jnp.dot(perm, x.astype(BF16), preferred_element_type=F32)
    blk = _lane_block((S5_CHUNK, S5_W))
    out = []
    for g in range(S5_G):
        acc = None
        for t in range(S5_CHUNK):
            src = r[t * S5_CHUNK:(t + 1) * S5_CHUNK, :]
            shift = ((t - g) * S5_GSIZE) % S5_W
            src = pltpu.roll(src, shift, 1) if shift else src
            acc = src if acc is None else jnp.where(blk == t, src, acc)
        out.append(acc)
    return out


def _group_rows_to_chunk_rows(ys, perm_t):
    blk = _lane_block((S5_CHUNK, S5_W))
    rows = []
    for t in range(S5_CHUNK):
        acc = None
        for g in range(S5_G):
            shift = ((g - t) * S5_GSIZE) % S5_W
            src = pltpu.roll(ys[g], shift, 1) if shift else ys[g]
            acc = src if acc is None else jnp.where(blk == g, src, acc)
        rows.append(acc)
    z = jnp.concatenate(rows, 0)
    return jnp.dot(perm_t, z.astype(BF16), preferred_element_type=F32)


def _pick(first, second, n_first):
    return jnp.where(pl.program_id(0) < n_first, first[...], second[...])


def _two_group_specs(block, n_first, lead=()):
    nl = len(lead)
    first = pl.BlockSpec(lead + block, lambda i, *_: (0,) * nl + (jnp.minimum(i, n_first - 1), 0))
    second = pl.BlockSpec(lead + block, lambda i, *_: (0,) * nl + (jnp.maximum(i - n_first, 0), 0))
    return [first, second]


def _inproj_kernel(xp_ref, xs_ref, ada_ref, npre_ref, w_ref, qn_ref, kvn_ref, perm_ref,
                   u_ref, cq_ref, ckv_ref, krab_ref, z_ref, xbc_ref, dt_ref, v_ref, *, n_first):
    ada = ada_ref[0]
    h = _rms(_pick(xp_ref, xs_ref, n_first), npre_ref[...]) * (1.0 + ada[1:2]) + ada[0:1]
    p = _bdot(h, w_ref[...])
    for g, ug in enumerate(_chunk_rows_to_group_rows(p[:, P_U:P_Q], perm_ref[...])):
        u_ref[g] = ug.astype(BF16)
    cq_ref[...] = _rms(p[:, P_Q:P_KV], qn_ref[...]).astype(BF16)
    ckv_ref[...] = _rms(p[:, P_KV:P_KRA], kvn_ref[...])
    krab_ref[...] = p[:, P_KRA:P_Z]
    z_ref[...] = p[:, P_Z:P_XBC]
    xbc_ref[...] = p[:, P_XBC:P_DT]
    dt_ref[...] = p[:, P_DT:P_CA]
    v_ref[...] = p[:, P_CA:P_CB] * jax.nn.sigmoid(p[:, P_CB:P_END])


def _chunk_perm():
    idx = np.arange(TM)
    src = (idx % S5_CHUNK) * S5_CHUNK + idx // S5_CHUNK
    return jnp.asarray(np.eye(TM, dtype=np.float32)[src], BF16)


def _inproj_call(xp, xs, ada_t, npre, w, qn, kvn):
    t = xp.shape[0] + xs.shape[0]
    n_first = xp.shape[0] // TM
    row = lambda n: pl.BlockSpec((TM, n), lambda i: (i, 0))
    full = lambda a: pl.BlockSpec(a.shape, lambda i: (0,) * a.ndim)
    widths = (256, 128, 256, 256, 512, 128, 256)
    dts = (BF16, F32, F32, F32, F32, F32, F32)
    perm = _chunk_perm()
    cpt = TM // S5_CHUNK
    return pl.pallas_call(
        functools.partial(_inproj_kernel, n_first=n_first), grid=(t // TM,),
        in_specs=_two_group_specs((TM, D_MODEL), n_first)
        + [pl.BlockSpec((1, 8, D_MODEL), lambda i: (i, 0, 0)), full(npre), full(w), full(qn), full(kvn), full(perm)],
        out_specs=[pl.BlockSpec((S5_G, cpt, S5_W), lambda i: (0, i, 0))] + [row(n) for n in widths],
        out_shape=[jax.ShapeDtypeStruct((S5_G, t // S5_CHUNK, S5_W), BF16)]
        + [jax.ShapeDtypeStruct((t, n), d) for n, d in zip(widths, dts)],
        compiler_params=_cparams("arbitrary"), name="inproj",
    )(xp, xs, ada_t, npre, w, qn, kvn, perm)


def _s5_kernel(u_ref, r_ref, vf_ref, vb_ref, a_ref, h0f_ref, h0fs_ref, h0b_ref, h0bs_ref,
               y_ref, stf_ref, stb_ref, sf, sfs, sb, sbs, *, nb, nc):
    gw = 2 * S5_P
    for g in range(S5_G):
        r = jnp.dot(u_ref[g], r_ref[g, :, S5_W:], preferred_element_type=F32)
        sl = slice(g * gw, (g + 1) * gw)
        sf[:, sl] = r[:, 0:128]
        sfs[:, sl] = r[:, 128:256]
        sb[:, sl] = r[:, 256:384]
        sbs[:, sl] = r[:, 384:512]
    arf, aif, aisf = a_ref[0:1], a_ref[1:2], a_ref[2:3]
    arb, aib, aisb = a_ref[3:4], a_ref[4:5], a_ref[5:6]
    cpt = 1 if nb % SUBLANE == 0 else SUBLANE // nb
    rows = cpt * nb
    nt = nc // cpt

    def body(i, carry):
        hf, hfs, hb, hbs = carry
        rf = pl.ds(pl.multiple_of(i * rows, rows), rows)
        rb = pl.ds(pl.multiple_of((nt - 1 - i) * rows, rows), rows)
        inc_f, inc_fs, inc_b, inc_bs = sf[rf, :], sfs[rf, :], sb[rb, :], sbs[rb, :]
        in_f, in_b = [None] * cpt, [None] * cpt
        for k in range(cpt):
            kf = slice(k * nb, (k + 1) * nb)
            kb = slice((cpt - 1 - k) * nb, (cpt - k) * nb)
            in_f[k] = hf
            in_b[cpt - 1 - k] = hb
            hf, hfs = arf * hf + aif * hfs + inc_f[kf], arf * hfs + aisf * hf + inc_fs[kf]
            hb, hbs = arb * hb + aib * hbs + inc_b[kb], arb * hbs + aisb * hb + inc_bs[kb]
        sf[rf, :] = in_f[0] if cpt == 1 else jnp.concatenate(in_f, 0)
        sb[rb, :] = in_b[0] if cpt == 1 else jnp.concatenate(in_b, 0)
        return hf, hfs, hb, hbs

    hf, _, hb, _ = lax.fori_loop(0, nt, body, (h0f_ref[...], h0fs_ref[...], h0b_ref[...], h0bs_ref[...]))
    stf_ref[...] = hf
    stb_ref[...] = hb
    for g in range(S5_G):
        sl = slice(g * gw, (g + 1) * gw)
        y = (jnp.dot(u_ref[g], r_ref[g, :, :S5_W], preferred_element_type=F32)
             + _bdot(sf[:, sl], vf_ref[g]) + _bdot(sb[:, sl], vb_ref[g]))
        y_ref[g] = y.astype(BF16)


def _s5_call(u, tabs, h0, nb):
    n = u.shape[1]
    r, vf, vb, avec = tabs
    sw = S5_G * 2 * S5_P
    st = jax.ShapeDtypeStruct((nb, sw), F32)
    return pl.pallas_call(
        functools.partial(_s5_kernel, nb=nb, nc=n // nb),
        out_shape=[jax.ShapeDtypeStruct(u.shape, BF16), st, st],
        scratch_shapes=[pltpu.VMEM((n, sw), F32) for _ in range(4)],
        compiler_params=_cparams(), name="s5",
    )(u, r, vf, vb, avec, *h0)


def _cpow(br, bi, e, nbits):
    pr = jnp.ones(e.shape, F32)
    pi = jnp.zeros(e.shape, F32)
    for k in range(nbits):
        bit = ((e >> k) & 1) == 1
        pr, pi = jnp.where(bit, pr * br - pi * bi, pr), jnp.where(bit, pr * bi + pi * br, pi)
        br, bi = br * br - bi * bi, 2.0 * br * bi
    return pr, pi


def _s5tab_kernel(prow_ref, pcol_ref, bt_ref, ct_ref, dsk_ref, etile_ref, r_ref, vf_ref, vb_ref, a_ref):
    q = S5_CHUNK
    hp = lax.Precision.HIGHEST

    def abar(lre, lim, ldt):
        dt = jnp.exp(ldt)
        mag = jnp.exp(lre * dt)
        return mag * jnp.cos(lim * dt), mag * jnp.sin(lim * dt)

    def per_step(x):
        return jnp.dot(x, etile_ref[...], precision=hp, preferred_element_type=F32)

    step_of_col = lax.broadcasted_iota(jnp.int32, (S5_P, S5_W), 1) // S5_GSIZE
    step_of_row = lax.broadcasted_iota(jnp.int32, (S5_W, S5_P), 0) // S5_GSIZE
    kcat, wtabs, arows = [], [], []
    for d in range(2):
        prow = prow_ref[d, 0]
        lre, lim = prow[0:1], prow[1:2]
        abr, abi = abar(lre, lim, prow[2:3])
        den = lre * lre + lim * lim
        fr = ((abr - 1.0) * lre + abi * lim) / den
        fi = (abi * lre - (abr - 1.0) * lim) / den
        btr, bti = bt_ref[d, 0, 0], bt_ref[d, 0, 1]
        bbr, bbi = fr * btr - fi * bti, fr * bti + fi * btr
        pcol = pcol_ref[d, 0]
        cbr, cbi = abar(pcol[:, 0:1], pcol[:, 1:2], pcol[:, 2:3])
        ctr, cti = per_step(ct_ref[d, 0, 0]), per_step(ct_ref[d, 0, 1])

        def c_times_power(e):
            pr, pi = _cpow(cbr, cbi, e, 5)
            return ctr * pr - cti * pi, ctr * pi + cti * pr

        vr, vi = c_times_power(step_of_col + 1 if d == 0 else q - step_of_col)
        (vf_ref if d == 0 else vb_ref)[0] = jnp.concatenate([vr, -vi], 0).astype(BF16)
        lr, li = c_times_power(step_of_col if d == 0 else q - 1 - step_of_col)
        kcat.append(jnp.dot(bbr, lr, precision=hp, preferred_element_type=F32)
                    - jnp.dot(bbi, li, precision=hp, preferred_element_type=F32))
        pr, pi = _cpow(abr, abi, q - 1 - step_of_row if d == 0 else step_of_row, 4)
        tbr, tbi = jnp.concatenate([bbr] * q, 0), jnp.concatenate([bbi] * q, 0)
        wr, wi = pr * tbr - pi * tbi, pr * tbi + pi * tbr
        wtabs += [wr, wi, wi, wr]
        ar, ai = _cpow(abr, abi, jnp.full((1, S5_P), q, jnp.int32), 5)
        arows += [jnp.concatenate([ar, ar], 1), jnp.concatenate([-ai, ai], 1), jnp.concatenate([ai, -ai], 1)]
    lane = lax.broadcasted_iota(jnp.int32, (S5_GSIZE, S5_W), 1)
    rows = []
    for s in range(q):
        shr = S5_GSIZE * s
        shl = S5_GSIZE * (q - 1 - s)
        f = jnp.where(lane >= shr, pltpu.roll(kcat[0], shr, 1), 0.0) if shr else kcat[0]
        b = jnp.where(lane < S5_W - shl, pltpu.roll(kcat[1], S5_W - shl, 1), 0.0) if shl else kcat[1]
        rows.append(f + b)
    ri = lax.broadcasted_iota(jnp.int32, (S5_W, S5_W), 0)
    ci = lax.broadcasted_iota(jnp.int32, (S5_W, S5_W), 1)
    tt = jnp.concatenate(rows, 0) + jnp.where(ri == ci, dsk_ref[0], 0.0)
    r_ref[0] = jnp.concatenate([tt] + wtabs, 1).astype(BF16)
    a_ref[...] = jnp.concatenate(arows + [jnp.zeros((2, 2 * S5_P), F32)], 0)


def _s5_tables(lam_re, lam_im, log_dt, b_re, b_im, c_re, c_im, dskip):
    zeros = jnp.zeros((2, S5_G, 5, S5_P), F32)
    prow = jnp.concatenate([lam_re[:, :, None], lam_im[:, :, None],
                            jnp.broadcast_to(log_dt[:, :, None, None], (2, S5_G, 1, S5_P)), zeros], 2)
    pcol = jnp.swapaxes(prow, 2, 3)
    bt = jnp.stack([jnp.swapaxes(b_re, 2, 3), jnp.swapaxes(b_im, 2, 3)], 2)
    ct = jnp.stack([jnp.swapaxes(c_re, 2, 3), jnp.swapaxes(c_im, 2, 3)], 2)
    dsk = jnp.tile(dskip.reshape(S5_G, 1, S5_GSIZE), (1, 1, S5_CHUNK))
    etile = jnp.asarray(np.tile(np.eye(S5_GSIZE, dtype=np.float32), (1, S5_CHUNK)))
    gblk = lambda a: pl.BlockSpec((2, 1) + a.shape[2:], lambda g: (0, g) + (0,) * (a.ndim - 2))
    sw = S5_G * 2 * S5_P
    return pl.pallas_call(
        _s5tab_kernel, grid=(S5_G,),
        in_specs=[gblk(prow), gblk(pcol), gblk(bt), gblk(ct), pl.BlockSpec((1, 1, S5_W), lambda g: (g, 0, 0)),
                  pl.BlockSpec(etile.shape, lambda g: (0, 0))],
        out_specs=[pl.BlockSpec((1, S5_W, 768), lambda g: (g, 0, 0)), pl.BlockSpec((1, 2 * S5_P, S5_W), lambda g: (g, 0, 0)),
                   pl.BlockSpec((1, 2 * S5_P, S5_W), lambda g: (g, 0, 0)), pl.BlockSpec((8, 2 * S5_P), lambda g: (0, g))],
        out_shape=[jax.ShapeDtypeStruct((S5_G, S5_W, 768), BF16), jax.ShapeDtypeStruct((S5_G, 2 * S5_P, S5_W), BF16),
                   jax.ShapeDtypeStruct((S5_G, 2 * S5_P, S5_W), BF16), jax.ShapeDtypeStruct((8, sw), F32)],
        compiler_params=_cparams("arbitrary"), name="s5tab",
    )(prow, pcol, bt, ct, dsk, etile)


def _s5_rows_in(u, nb):
    g, n, w = u.shape
    return jnp.swapaxes(u.reshape(g, nb, n // nb, w), 1, 2).reshape(g, n, w)


def _s5_rows_out(y, nb):
    g, n, w = y.shape
    return jnp.swapaxes(y.reshape(g, n // nb, nb, w), 1, 2).reshape(g, n, w)


def _s5_state_in(state):
    b = state.shape[0]
    sw = jnp.transpose(state, (0, 1, 2, 4, 3))
    plain = sw.reshape(b, 2, S5_G * 2 * S5_P)
    swapped = sw[:, :, :, ::-1, :].reshape(b, 2, S5_G * 2 * S5_P)
    return plain[:, 0], swapped[:, 0], plain[:, 1], swapped[:, 1]


def _s5_state_out(stf, stb):
    b = stf.shape[0]
    st = jnp.stack([stf, stb], 1).reshape(b, 2, S5_G, 2, S5_P)
    return jnp.transpose(st, (0, 1, 2, 4, 3))


def _mla_kernel(*refs, seq, lctx):
    if lctx:
        (cq_ref, ckv_ref, krab_ref, cosq_ref, sinq_ref, cosk_ref, sink_ref, cckv_ref, ckr_ref,
         wqa_ref, wqb_ref, wk_ref, wv_ref, o_ref, k_scr, v_scr, s_scr) = refs
    else:
        (cq_ref, ckv_ref, krab_ref, cosq_ref, sinq_ref, cosk_ref, sink_ref,
         wqa_ref, wqb_ref, wk_ref, wv_ref, o_ref, k_scr, v_scr, s_scr) = refs
    lk = lctx + seq
    rb = min(seq, 512)

    @pl.when(pl.program_id(1) == 0)
    def _build_keys():
        if lctx:
            ckv = cckv_ref[0]
            kn = _bdot(ckv, wk_ref[...])
            vn = _bdot(ckv, wv_ref[...])
            for h in range(MLA_H):
                k_scr[h, 0:lctx, :] = (kn[:, h * LANE:(h + 1) * LANE] + ckr_ref[0]).astype(BF16)
                v_scr[h, 0:lctx, :] = vn[:, h * LANE:(h + 1) * LANE].astype(BF16)

        def chunk(i, _):
            r0 = pl.multiple_of(i * rb, rb)
            rows = pl.ds(r0, rb)
            ckv = ckv_ref[rows, :]
            kn = _bdot(ckv, wk_ref[...])
            vn = _bdot(ckv, wv_ref[...])
            krab = krab_ref[rows, :]
            krx = krab[:, :LANE] * cosk_ref[rows, :] + krab[:, LANE:] * sink_ref[rows, :]
            dst = pl.ds(pl.multiple_of(lctx + r0, rb if lctx % rb == 0 else 256), rb)
            for h in range(MLA_H):
                k_scr[h, dst, :] = (kn[:, h * LANE:(h + 1) * LANE] + krx).astype(BF16)
                v_scr[h, dst, :] = vn[:, h * LANE:(h + 1) * LANE].astype(BF16)
            return 0

        lax.fori_loop(0, seq // rb, chunk, 0)

    cq = cq_ref[...]
    qa = _bdot(cq, wqa_ref[...])
    qb = _bdot(cq, wqb_ref[...])
    cosq, sinq = cosq_ref[...], sinq_ref[...]
    scale = 1.0 / math.sqrt(MLA_NOPE + MLA_ROPE)
    outs = []
    chunks = [slice(j * MLA_KC, (j + 1) * MLA_KC) for j in range(lk // MLA_KC)]
    for h in range(MLA_H):
        hs = slice(h * LANE, (h + 1) * LANE)
        qh = ((qa[:, hs] * cosq + qb[:, hs] * sinq) * scale).astype(BF16)
        mp = None
        for ks in chunks:
            s = _bdot_nt(qh, k_scr[h, ks, :])
            s_scr[:, ks] = s
            for t in range(MLA_KC // LANE):
                part = s[:, t * LANE:(t + 1) * LANE]
                mp = part if mp is None else jnp.maximum(mp, part)
        m = jnp.max(mp, axis=-1, keepdims=True)
        lp = jnp.zeros((MLA_TQ, LANE), F32)
        acc = jnp.zeros((MLA_TQ, LANE), F32)
        for ks in chunks:
            e = jnp.exp(s_scr[:, ks] - m)
            for t in range(MLA_KC // LANE):
                lp = lp + e[:, t * LANE:(t + 1) * LANE]
            acc = acc + _bdot(e, v_scr[h, ks, :])
        outs.append((acc / jnp.sum(lp, axis=-1, keepdims=True))[:, :MLA_V])
    o_ref[...] = jnp.concatenate(outs, axis=-1)


def _mla_call(cq, ckv, krab, rope, ctx, w, nb, seq, row0):
    lctx = 0 if ctx is None else ctx[0].shape[1]
    nq = seq // MLA_TQ
    q0, s0 = row0 // MLA_TQ, row0 // seq
    cosx, sinx = rope
    qrow = lambda n: pl.BlockSpec((MLA_TQ, n), lambda b, i: (q0 + b * nq + i, 0))
    srow = lambda n: pl.BlockSpec((seq, n), lambda b, i: (s0 + b, 0))
    full = lambda a: pl.BlockSpec(a.shape, lambda b, i: (0,) * a.ndim)
    in_specs = [qrow(256), srow(128), srow(256), pl.BlockSpec((MLA_TQ, LANE), lambda b, i: (i, 0)),
                pl.BlockSpec((MLA_TQ, LANE), lambda b, i: (i, 0)), full(cosx), full(sinx)]
    args = [cq, ckv, krab, cosx, sinx, cosx, sinx]
    if lctx:
        in_specs += [pl.BlockSpec((1, lctx, LANE), lambda b, i: (b, 0, 0))] * 2
        args += list(ctx)
    in_specs += [full(a) for a in w]
    args += list(w)
    lk = lctx + seq
    return pl.pallas_call(
        functools.partial(_mla_kernel, seq=seq, lctx=lctx), grid=(nb, nq),
        in_specs=in_specs,
        out_specs=pl.BlockSpec((MLA_TQ, 256), lambda b, i: (b * nq + i, 0)),
        out_shape=jax.ShapeDtypeStruct((nb * seq, 256), F32),
        scratch_shapes=[pltpu.VMEM((MLA_H, lk, LANE), BF16), pltpu.VMEM((MLA_H, lk, LANE), BF16),
                        pltpu.VMEM((MLA_TQ, lk), F32)],
        compiler_params=_cparams("arbitrary", "arbitrary"), name="mla",
    )(*args)


def _swap8(w):
    q = MLA_ROPE // 4
    return jnp.concatenate([w[..., j * q:(j + 1) * q] for j in (1, 0, 3, 2)], -1)


def _mla_weights(wuq, wuk, wuv):
    z32 = jnp.zeros((MLA_QLORA, 32), F32)
    z64 = jnp.zeros((MLA_QLORA, 64), F32)
    k64 = jnp.zeros((MLA_KVLORA, 64), F32)
    qa, qb, wk, wv = [], [], [], []
    for h in range(MLA_H):
        rope_w = wuq[:, h, MLA_NOPE:]
        qa += [wuq[:, h, :MLA_NOPE], rope_w, z32]
        qb += [z64, _swap8(rope_w), z32]
        wk += [wuk[:, h], k64]
        wv += [wuv[:, h], k64]
    cat = lambda xs: jnp.concatenate(xs, -1).astype(BF16)
    return cat(qa), cat(qb), cat(wk), cat(wv)


def _rope_tables(seq, rotate):
    ones = jnp.ones((seq, MLA_NOPE), F32)
    z32 = jnp.zeros((seq, 32), F32)
    if not rotate:
        return (jnp.concatenate([ones, jnp.ones((seq, MLA_ROPE), F32), z32], -1), jnp.zeros((seq, LANE), F32))
    pos = jnp.arange(seq)
    row = (pos // GRID_W).astype(F32)
    col = (pos % GRID_W).astype(F32)
    half = MLA_ROPE // 2
    inv = ROPE_BASE ** (-jnp.arange(0, half, 2, dtype=F32) / half)
    ang = jnp.concatenate([row[:, None] * inv, col[:, None] * inv], -1)
    cos, sin = jnp.cos(ang), jnp.sin(ang)
    q = MLA_ROPE // 4
    cos32 = jnp.concatenate([cos[:, :q], cos[:, :q], cos[:, q:], cos[:, q:]], -1)
    sin32 = jnp.concatenate([-sin[:, :q], sin[:, :q], -sin[:, q:], sin[:, q:]], -1)
    return jnp.concatenate([ones, cos32, z32], -1), jnp.concatenate([jnp.zeros_like(ones), sin32, z32], -1)


def _cumsum_rows(x, reverse):
    n = x.shape[0]
    row = lax.broadcasted_iota(jnp.int32, x.shape, 0)
    s = 1
    while s < n:
        if reverse:
            x = x + jnp.where(row < n - s, pltpu.roll(x, n - s, 0), 0.0)
        else:
            x = x + jnp.where(row >= s, pltpu.roll(x, s, 0), 0.0)
        s *= 2
    return x


def _ssd_kernel(xbc_ref, z_ref, dt_ref, cw_ref, cb_ref, par_ref, dsk_ref, nw_ref, h0_ref,
                y_ref, st_ref, xc_scr, xt_scr, y_scr, hf_scr, hb_scr, *, seq):
    q = SSD_CHUNK
    nc = seq // q
    row = lax.broadcasted_iota(jnp.int32, (q, 1), 0)

    def conv(c, _):
        t0 = pl.multiple_of(c * q, q)
        cur = xbc_ref[pl.ds(t0, q), :]
        prev8 = xbc_ref[pl.ds(pl.multiple_of(jnp.maximum(t0 - 8, 0), 8), 8), :]
        next8 = xbc_ref[pl.ds(pl.multiple_of(jnp.minimum(t0 + q, seq - 8), 8), 8), :]
        prow = jnp.where(c > 0, prev8[7:8], 0.0)
        nrow = jnp.where(c < nc - 1, next8[0:1], 0.0)
        up = jnp.where(row == 0, prow, pltpu.roll(cur, 1, 0))
        dn = jnp.where(row == q - 1, nrow, pltpu.roll(cur, q - 1, 0))
        acc = up * cw_ref[0:1] + cur * cw_ref[1:2] + dn * cw_ref[2:3] + cb_ref[...]
        xc = jax.nn.silu(acc)
        xc_scr[pl.ds(t0, q), :] = xc
        xt_scr[c] = jnp.transpose(xc[:, :SSD_W])
        y_scr[pl.ds(t0, q), :] = jnp.zeros((q, SSD_W), F32)
        return 0

    lax.fori_loop(0, nc, conv, 0)

    pair_rows = (SSD_H // SSD_G) * SSD_P
    for d, h_scr in enumerate((hf_scr, hb_scr)):
        h_scr[...] = jnp.zeros_like(h_scr)
        for g in range(SSD_G):
            h_scr[g * pair_rows:(g + 1) * pair_rows, g * SSD_N:(g + 1) * SSD_N] = (
                h0_ref[0, d, g * pair_rows:(g + 1) * pair_rows, :])
    ii = lax.broadcasted_iota(jnp.int32, (q, q), 0)
    jj = lax.broadcasted_iota(jnp.int32, (q, q), 1)
    first_half = lax.broadcasted_iota(jnp.int32, (q, LANE), 1) < SSD_N

    def one_chunk(c, d, h_scr):
        t0 = pl.multiple_of(c * q, q)
        xc = xc_scr[pl.ds(t0, q), :]
        xs_t = xt_scr[c]
        dt = jax.nn.softplus(dt_ref[pl.ds(t0, q), :] + par_ref[d:d + 1])
        cs = _cumsum_rows(dt * par_ref[2 + d:3 + d], reverse=(d == 1))
        edge = cs[q - 1:q] if d == 0 else cs[0:1]
        cs_t = jnp.transpose(cs)
        dt_t = jnp.transpose(dt)
        wts_t = jnp.transpose(jnp.exp(edge - cs) * dt)
        decay = jnp.exp(edge)
        mask = (ii >= jj) if d == 0 else (ii <= jj)
        bm_pair = xc[:, SSD_W:SSD_W + LANE]
        cm_pair = xc[:, SSD_W + LANE:SSD_W + 2 * LANE]
        for g in range(SSD_G):
            in_g = first_half if g == 0 else jnp.logical_not(first_half)
            pair = slice(g * LANE, (g + 1) * LANE)
            heads = [g * (SSD_H // SSD_G) + hh for hh in range(SSD_H // SSD_G)]
            cm = jnp.where(in_g, cm_pair, 0.0)
            bm = jnp.where(in_g, bm_pair, 0.0)
            cbm = _bdot_nt(cm, bm_pair)
            cs_i = [jnp.transpose(jnp.broadcast_to(cs_t[h:h + 1, :], (q, q))) for h in heads]
            y_off = _bdot_nt(cm, h_scr[pair, :]) * jnp.exp(jnp.where(first_half, cs_i[0], cs_i[1]))
            ys = []
            for hh, h in enumerate(heads):
                lmat = jnp.exp(jnp.where(mask, cs_i[hh] - cs_t[h:h + 1, :], -jnp.inf))
                ys.append(_bdot(cbm * lmat * dt_t[h:h + 1, :], xc[:, pair]))
                ps = slice(h * SSD_P, (h + 1) * SSD_P)
                st = _bdot(xs_t[ps, :] * wts_t[h:h + 1, :], bm)
                h_scr[ps, :] = decay[:, h:h + 1] * h_scr[ps, :] + st
            y_scr[pl.ds(t0, q), pair] += jnp.where(first_half, ys[0], ys[1]) + y_off

    def chunks(c, _):
        one_chunk(c, 0, hf_scr)
        one_chunk(nc - 1 - c, 1, hb_scr)
        return 0

    lax.fori_loop(0, nc, chunks, 0, unroll=2)
    for d, h_scr in enumerate((hf_scr, hb_scr)):
        for g in range(SSD_G):
            st_ref[0, d, g * pair_rows:(g + 1) * pair_rows, :] = (
                h_scr[g * pair_rows:(g + 1) * pair_rows, g * SSD_N:(g + 1) * SSD_N])

    def finish(c, _):
        rows = pl.ds(pl.multiple_of(c * q, q), q)
        y = y_scr[rows, :] + dsk_ref[...] * xc_scr[rows, 0:SSD_W]
        y_ref[rows, :] = _rms(y * jax.nn.silu(z_ref[rows, :]), nw_ref[...])
        return 0

    lax.fori_loop(0, nc, finish, 0)


def _ssd_call(xbc, z, dt, w, h0, nb, seq, row0):
    s0 = row0 // seq
    srow = lambda n: pl.BlockSpec((seq, n), lambda b: (s0 + b, 0))
    full = lambda a: pl.BlockSpec(a.shape, lambda b: (0,) * a.ndim)
    hp = SSD_H * SSD_P
    st_spec = pl.BlockSpec((1, 2, hp, SSD_N), lambda b: (b, 0, 0, 0))
    return pl.pallas_call(
        functools.partial(_ssd_kernel, seq=seq), grid=(nb,),
        in_specs=[srow(SSD_XBC), srow(SSD_W), srow(LANE)] + [full(a) for a in w] + [st_spec],
        out_specs=[pl.BlockSpec((seq, SSD_W), lambda b: (b, 0)), st_spec],
        out_shape=[jax.ShapeDtypeStruct((nb * seq, SSD_W), F32), jax.ShapeDtypeStruct((nb, 2, hp, SSD_N), F32)],
        scratch_shapes=[pltpu.VMEM((seq, SSD_XBC), F32), pltpu.VMEM((seq // SSD_CHUNK, SSD_W, SSD_CHUNK), F32),
                        pltpu.VMEM((seq, SSD_W), F32), pltpu.VMEM((hp, LANE), F32), pltpu.VMEM((hp, LANE), F32)],
        compiler_params=_cparams("arbitrary"), name="ssd",
    )(xbc, z, dt, *w, h0)


def _ssd_weights(conv_w, conv_b, dt_bias, a_log, dskip, norm_w):
    pad = lambda x: jnp.pad(x, ((0, 0), (0, LANE - SSD_H)))
    par = jnp.concatenate([pad(dt_bias), pad(-jnp.exp(a_log)), jnp.zeros((4, LANE), F32)], 0)
    cw = jnp.concatenate([conv_w, jnp.zeros((5, SSD_XBC), F32)], 0)
    return (cw, conv_b.reshape(1, SSD_XBC), par, jnp.repeat(dskip, SSD_P).reshape(1, SSD_W),
            norm_w.reshape(1, SSD_W))


def _conf_kernel(v_ref, w_ref, b_ref, g_ref, bt_ref, o_ref, pad_scr, sh_scr, *, seq, seg):
    q = CONF_Q
    halo = 2 * CONF_PAD
    tail = pad_scr.shape[0] - seq - CONF_PAD
    pad_scr[0:CONF_PAD, :] = jnp.zeros((CONF_PAD, CONF_W), F32)
    pad_scr[seq + CONF_PAD:, :] = jnp.zeros((tail, CONF_W), F32)
    pad_scr[CONF_PAD:seq + CONF_PAD, :] = v_ref[...]
    lo = CONF_PAD - CONF_K // 2
    pieces = [(j, min(LANE, seg + halo - j)) for j in range(0, seg + halo, LANE)]

    def segment(s, _):
        s0 = pl.multiple_of(s * seg, seg)
        for j0, n in pieces:
            big = pad_scr[pl.ds(s0 + j0, n + SUBLANE), :]
            for r in range(SUBLANE):
                sh_scr[r, j0:j0 + n, :] = big[r:r + n]

        def chunk(c, _):
            t0 = pl.multiple_of(c * q, q)
            acc = jnp.zeros((q, CONF_W), F32) + b_ref[...]
            for k in range(CONF_K):
                r, off = (lo + k) % SUBLANE, (lo + k) // SUBLANE * SUBLANE
                wk = jnp.concatenate([w_ref[k * SUBLANE:(k + 1) * SUBLANE, :]] * (q // SUBLANE), 0)
                acc = acc + sh_scr[r, pl.ds(pl.multiple_of(t0 + off, SUBLANE), q), :] * wk
            mu = jnp.mean(acc, axis=-1, keepdims=True)
            xc = acc - mu
            var = jnp.mean(xc * xc, axis=-1, keepdims=True)
            o_ref[pl.ds(s0 + t0, q), :] = jax.nn.silu(xc * lax.rsqrt(var + EPS) * g_ref[...] + bt_ref[...])
            return 0

        lax.fori_loop(0, seg // q, chunk, 0)
        return 0

    lax.fori_loop(0, seq // seg, segment, 0)


def _conf_call(v, w, nb, seq, row0):
    s0 = row0 // seq
    full = lambda a: pl.BlockSpec(a.shape, lambda b: (0,) * a.ndim)
    seg = min(seq, CONF_SEG)
    halo = 2 * CONF_PAD
    return pl.pallas_call(
        functools.partial(_conf_kernel, seq=seq, seg=seg), grid=(nb,),
        in_specs=[pl.BlockSpec((seq, CONF_W), lambda b: (s0 + b, 0))] + [full(a) for a in w],
        out_specs=pl.BlockSpec((seq, CONF_W), lambda b: (b, 0)),
        out_shape=jax.ShapeDtypeStruct((nb * seq, CONF_W), F32),
        scratch_shapes=[pltpu.VMEM((seq + halo + 2 * SUBLANE, CONF_W), F32),
                        pltpu.VMEM((SUBLANE, seg + halo, CONF_W), F32)],
        compiler_params=_cparams("arbitrary"), name="conf",
    )(v, *w)


def _split_bf16(x):
    hi = x.astype(BF16)
    return hi, (x - hi.astype(F32)).astype(BF16)


def _outproj_kernel(xp_ref, xs_ref, ada_ref, s5p_ref, s5s_ref, mlap_ref, mlas_ref, ssdp_ref, ssds_ref, confp_ref,
                    confs_ref, wglu_ref, bglu_ref, wo_ref, npost_ref, npre_ref, wrh_ref, wrl_ref, br_ref, perm_ref,
                    x1_ref, h2_ref, gate_ref, *, n_first):
    ada = ada_ref[0]
    ys5 = _pick(s5p_ref, s5s_ref, n_first)
    a = jax.nn.gelu(_group_rows_to_chunk_rows([ys5[g].astype(F32) for g in range(S5_G)], perm_ref[...]))
    s5 = a * jax.nn.sigmoid(_bdot(a, wglu_ref[...]) + bglu_ref[...])
    mix = (_bdot(s5, wo_ref[0:256, :]) + _bdot(_pick(mlap_ref, mlas_ref, n_first), wo_ref[256:512, :])
           + _bdot(_pick(ssdp_ref, ssds_ref, n_first), wo_ref[512:768, :])
           + _bdot(_pick(confp_ref, confs_ref, n_first), wo_ref[768:1024, :]))
    x1 = _pick(xp_ref, xs_ref, n_first) + ada[2:3] * _rms(mix, npost_ref[...])
    x1_ref[...] = x1
    h2 = _rms(x1, npre_ref[...]) * (1.0 + ada[4:5]) + ada[3:4]
    h2_ref[...] = h2.astype(BF16)
    hi, lo = _split_bf16(h2)
    lg = (jnp.dot(hi, wrh_ref[...], preferred_element_type=F32) + jnp.dot(lo, wrh_ref[...], preferred_element_type=F32)
          + jnp.dot(hi, wrl_ref[...], preferred_element_type=F32) + br_ref[...])
    lane = lax.broadcasted_iota(jnp.int32, lg.shape, 1)
    neg = -jnp.inf
    big = jnp.int32(1 << 20)
    first = lambda hit: jnp.min(jnp.where(hit, lane, big), axis=-1, keepdims=True)
    glm = jnp.where((lane >= MOE_E) & (lane < MOE_E + MOE_GROUPS), lg, neg)
    gmax = jnp.max(glm, axis=-1, keepdims=True)
    p_group = 1.0 / jnp.sum(jnp.exp(glm - gmax), axis=-1, keepdims=True)
    gsel = first(glm == gmax) - MOE_E
    elm = jnp.where((lane < MOE_E) & ((lane // MOE_PER_GROUP) == gsel), lg, neg)
    v1 = jnp.max(elm, axis=-1, keepdims=True)
    i1 = first(elm == v1)
    elm2 = jnp.where(lane == i1, neg, elm)
    v2 = jnp.max(elm2, axis=-1, keepdims=True)
    i2 = first(elm2 == v2)
    e2 = jnp.exp(v2 - v1)
    w1 = p_group / (1.0 + e2)
    w2 = p_group * e2 / (1.0 + e2)
    gate_ref[...] = jnp.where(lane == i1, w1, 0.0) + jnp.where(lane == i2, w2, 0.0)


def _outproj_call(x, ada_t, ys5, ymla, yssd, yconf, w):
    t = x[0].shape[0] + x[1].shape[0]
    n_first = x[0].shape[0] // TM
    row = lambda n: pl.BlockSpec((TM, n), lambda i: (i, 0))
    full = lambda a: pl.BlockSpec(a.shape, lambda i: (0,) * a.ndim)
    w = tuple(w) + (_chunk_perm(),)
    pair = lambda n: _two_group_specs((TM, n), n_first)
    return pl.pallas_call(
        functools.partial(_outproj_kernel, n_first=n_first), grid=(t // TM,),
        in_specs=pair(D_MODEL) + [pl.BlockSpec((1, 8, D_MODEL), lambda i: (i, 0, 0))]
        + _two_group_specs((TM // S5_CHUNK, S5_W), n_first, lead=(S5_G,)) + pair(256) + pair(256) + pair(256)
        + [full(a) for a in w],
        out_specs=[row(D_MODEL), row(D_MODEL), row(LANE)],
        out_shape=[jax.ShapeDtypeStruct((t, D_MODEL), F32), jax.ShapeDtypeStruct((t, D_MODEL), BF16),
                   jax.ShapeDtypeStruct((t, LANE), F32)],
        compiler_params=_cparams("arbitrary"), name="outproj",
    )(*x, ada_t, *ys5, *ymla, *yssd, *yconf, *w)


def _moe_kernel(h_ref, gate_ref, x1_ref, ada_ref, wg_ref, wu_ref, wd_ref, npost_ref, op_ref, os_ref, acc_ref,
                *, n_first):
    i = pl.program_id(0)
    e = pl.program_id(1)

    @pl.when(e == 0)
    def _zero():
        acc_ref[...] = jnp.zeros_like(acc_ref)

    h = h_ref[...]
    lane = lax.broadcasted_iota(jnp.int32, gate_ref.shape, 1)
    ge = jnp.sum(jnp.where(lane == e, gate_ref[...], 0.0), axis=-1, keepdims=True)
    hid = (jax.nn.silu(jnp.dot(h, wg_ref[0], preferred_element_type=F32))
           * jnp.dot(h, wu_ref[0], preferred_element_type=F32) * ge)
    acc_ref[...] += jnp.dot(hid.astype(BF16), wd_ref[0], preferred_element_type=F32)

    def result():
        return x1_ref[...] + ada_ref[0][5:6] * _rms(acc_ref[...], npost_ref[...])

    @pl.when((e == MOE_E - 1) & (i < n_first))
    def _finish_first():
        op_ref[...] = result()

    @pl.when((e == MOE_E - 1) & (i >= n_first))
    def _finish_second():
        os_ref[...] = result()


def _moe_call(h2, gate, x1, ada_t, wg, wu, wd, npost, t_first):
    t = h2.shape[0]
    tm = MOE_TM
    n_first = t_first // tm
    row = lambda n: pl.BlockSpec((tm, n), lambda i, e: (i, 0))
    return pl.pallas_call(
        functools.partial(_moe_kernel, n_first=n_first), grid=(t // tm, MOE_E),
        in_specs=[row(D_MODEL), row(LANE), row(D_MODEL),
                  pl.BlockSpec((1, 8, D_MODEL), lambda i, e: (i * (tm // TM), 0, 0)),
                  pl.BlockSpec((1, D_MODEL, MOE_HID), lambda i, e: (e, 0, 0)),
                  pl.BlockSpec((1, D_MODEL, MOE_HID), lambda i, e: (e, 0, 0)),
                  pl.BlockSpec((1, MOE_HID, D_MODEL), lambda i, e: (e, 0, 0)),
                  pl.BlockSpec(npost.shape, lambda i, e: (0, 0))],
        out_specs=_two_group_specs((tm, D_MODEL), n_first),
        out_shape=[jax.ShapeDtypeStruct((t_first, D_MODEL), F32), jax.ShapeDtypeStruct((t - t_first, D_MODEL), F32)],
        scratch_shapes=[pltpu.VMEM((tm, D_MODEL), F32)],
        compiler_params=_cparams("arbitrary", "arbitrary"), name="moe",
    )(h2, gate, x1, ada_t, wg, wu, wd, npost)


def _pack_w_in(w):
    d = w.shape[0]
    z32, z64 = jnp.zeros((d, 32), F32), jnp.zeros((d, 64), F32)
    kr = w[:, OFF_MLA_KR:OFF_SSD_Z]
    dtw = jnp.pad(w[:, OFF_SSD_DT:OFF_CONF], ((0, 0), (0, LANE - SSD_H)))
    cols = [w[:, OFF_S5:OFF_MLA_Q], w[:, OFF_MLA_Q:OFF_MLA_KV], w[:, OFF_MLA_KV:OFF_MLA_KR],
            z64, kr, z32, z64, _swap8(kr), z32,
            w[:, OFF_SSD_Z:OFF_SSD_XBC], w[:, OFF_SSD_XBC:OFF_SSD_DT], dtw, w[:, OFF_CONF:IN_COLS]]
    return jnp.concatenate(cols, -1).astype(BF16)


def _row(v):
    return v.reshape(1, -1)


def kernel(x_prompt, x_sample, cache_mla_ckv, cache_mla_krope, state_s5, state_ssd, c, c_ctx, w_ada, b_ada, norm_pre1, norm_post1, norm_pre2, norm_post2, w_in, w_out, s5_lam_re, s5_lam_im, s5_log_dt, s5_b_re, s5_b_im, s5_c_re, s5_c_im, s5_d, s5_w_glu, s5_b_glu, mla_qnorm, mla_kvnorm, mla_wuq, mla_wuk, mla_wuv, ssd_conv_w, ssd_conv_b, ssd_dt_bias, ssd_a_log, ssd_d, ssd_norm_w, conf_dw_w, conf_dw_b, conf_ln_g, conf_ln_b, moe_wg, moe_bg, moe_we, moe_be, moe_w_gate, moe_w_up, moe_w_down):
    bp, lp, d = x_prompt.shape
    bs, ls, _ = x_sample.shape
    tp, ts = bp * lp, bs * ls
    x = (x_prompt.reshape(tp, d), x_sample.reshape(ts, d))

    crows = jnp.concatenate([c_ctx[None], c, jnp.zeros((8 - 1 - bs, d), F32)], 0)
    ada = _ada_call(crows, w_ada, b_ada)
    tile_row = np.concatenate([np.zeros(tp // TM, np.int32), 1 + np.repeat(np.arange(bs, dtype=np.int32), ls // TM)])

    rope_p = _rope_tables(lp, rotate=False)
    rope_s = _rope_tables(ls, rotate=True)
    zeros_s5 = tuple(jnp.zeros((bp, S5_G * 2 * S5_P), F32) for _ in range(4))
    zeros_ssd = jnp.zeros((bp, 2, SSD_H * SSD_P, SSD_N), F32)
    pad_kr = lambda kr: jnp.pad(kr, ((0, 0), (0, 0), (MLA_NOPE, LANE - MLA_NOPE - MLA_ROPE)))
    moe_w = [w.astype(BF16) for w in (moe_w_gate, moe_w_up, moe_w_down)]

    ckv_l, kr_l, s5_l, ssd_l = [], [], [], []
    for l in range(DEPTH):
        ada_t = jnp.pad(ada[l][tile_row].reshape(-1, 6, d), ((0, 0), (0, 2), (0, 0)))
        u, cq, ckv, krab, z, xbc, dt, v = _inproj_call(
            *x, ada_t, _row(norm_pre1[l]), _pack_w_in(w_in[l]), _row(mla_qnorm[l]), _row(mla_kvnorm[l]))

        tabs = _s5_tables(s5_lam_re[l], s5_lam_im[l], s5_log_dt[l], s5_b_re[l], s5_b_im[l], s5_c_re[l], s5_c_im[l],
                          s5_d[l])
        ncp = tp // S5_CHUNK
        yp, stf, stb = _s5_call(_s5_rows_in(u[:, :ncp], bp), tabs, zeros_s5, bp)
        ys, _, _ = _s5_call(_s5_rows_in(u[:, ncp:], bs), tabs, _s5_state_in(state_s5[:, l]), bs)
        y_s5 = (_s5_rows_out(yp, bp), _s5_rows_out(ys, bs))
        s5_l.append(_s5_state_out(stf, stb))

        mw = _mla_weights(mla_wuq[l], mla_wuk[l], mla_wuv[l])
        y_mla = (_mla_call(cq, ckv, krab, rope_p, None, mw, bp, lp, 0),
                 _mla_call(cq, ckv, krab, rope_s, (cache_mla_ckv[:, l], pad_kr(cache_mla_krope[:, l])), mw, bs, ls, tp))
        ckv_l.append(ckv[:tp].reshape(bp, lp, MLA_KVLORA))
        kr_l.append(krab[:tp, MLA_NOPE:MLA_NOPE + MLA_ROPE].reshape(bp, lp, MLA_ROPE))

        sw = _ssd_weights(ssd_conv_w[l], ssd_conv_b[l], ssd_dt_bias[l], ssd_a_log[l], ssd_d[l], ssd_norm_w[l])
        yssd_p, st_ssd = _ssd_call(xbc, z, dt, sw, zeros_ssd, bp, lp, 0)
        yssd_s, _ = _ssd_call(xbc, z, dt, sw, state_ssd[:, l].reshape(bs, 2, SSD_H * SSD_P, SSD_N), bs, ls, tp)
        ssd_l.append(st_ssd.reshape(bp, 2, SSD_H, SSD_P, SSD_N))

        cw = (jnp.repeat(conf_dw_w[l], SUBLANE, axis=0), _row(conf_dw_b[l]),
              _row(conf_ln_g[l]), _row(conf_ln_b[l]))
        y_conf = (_conf_call(v, cw, bp, lp, 0), _conf_call(v, cw, bs, ls, tp))

        wr = jnp.pad(jnp.concatenate([moe_we[l], moe_wg[l]], -1), ((0, 0), (0, LANE - MOE_E - MOE_GROUPS)))
        wrh, wrl = _split_bf16(wr)
        br = jnp.pad(jnp.concatenate([moe_be[l], moe_bg[l]]), (0, LANE - MOE_E - MOE_GROUPS)).reshape(1, LANE)
        ow = (s5_w_glu[l].astype(BF16), _row(s5_b_glu[l]), w_out[l].astype(BF16), _row(norm_post1[l]),
              _row(norm_pre2[l]), wrh, wrl, br)
        x1, h2, gate = _outproj_call(x, ada_t, y_s5, y_mla, (yssd_p, yssd_s), y_conf, ow)

        x = _moe_call(h2, gate, x1, ada_t, moe_w[0][l], moe_w[1][l], moe_w[2][l], _row(norm_post2[l]), tp)

    return (x[0].reshape(bp, lp, d), x[1].reshape(bs, ls, d),
            jnp.stack(ckv_l, 1), jnp.stack(kr_l, 1), jnp.stack(s5_l, 1), jnp.stack(ssd_l, 1))
```

```python
import functools
import math

import jax
import jax.numpy as jnp
import numpy as np
from jax import lax
from jax.experimental import pallas as pl
from jax.experimental.pallas import tpu as pltpu

F32 = jnp.float32
BF16 = jnp.bfloat16

D_MODEL = 1024
DEPTH = 2
GRID_W = 64
EPS = 1e-6

S5_W = 256
S5_GSIZE = 16
S5_G = 16
S5_P = 64
S5_CHUNK = 16

MLA_H = 4
MLA_NOPE = 64
MLA_ROPE = 32
MLA_V = 64
MLA_QLORA = 256
MLA_KVLORA = 128
ROPE_BASE = 10000.0
MLA_TQ = 256
MLA_KC = 256

SSD_W = 256
SSD_P = 64
SSD_H = 4
SSD_G = 2
SSD_N = 64
SSD_CHUNK = 128
SSD_XBC = 512

CONF_W = 256
CONF_K = 31
CONF_Q = 128
CONF_PAD = 16
CONF_SEG = 1024

MOE_GROUPS = 4
MOE_PER_GROUP = 4
MOE_E = 16
MOE_HID = 256
MOE_TM = 512

OFF_S5 = 0
OFF_MLA_Q = 256
OFF_MLA_KV = 512
OFF_MLA_KR = 640
OFF_SSD_Z = 672
OFF_SSD_XBC = 928
OFF_SSD_DT = 1440
OFF_CONF = 1444
IN_COLS = 1956

LANE = 128
SUBLANE = 8
TM = 256
P_U, P_Q, P_KV, P_KRA, P_KRB, P_Z, P_XBC, P_DT, P_CA, P_CB, P_END = (
    0, 256, 512, 640, 768, 896, 1152, 1664, 1792, 2048, 2304)

VMEM_LIMIT = 56 * 1024 * 1024


def _bdot(a, b):
    return jnp.dot(a.astype(BF16), b.astype(BF16), preferred_element_type=F32)


def _bdot_nt(a, b):
    return lax.dot_general(a.astype(BF16), b.astype(BF16), (((1,), (1,)), ((), ())), preferred_element_type=F32)


def _rms(x, g):
    return x * lax.rsqrt(jnp.mean(x * x, axis=-1, keepdims=True) + EPS) * g


def _cparams(*sem):
    return pltpu.CompilerParams(dimension_semantics=sem if sem else None, vmem_limit_bytes=VMEM_LIMIT)


def _ada_kernel(c_ref, w_ref, b_ref, o_ref):
    o_ref[0] = _bdot(jax.nn.silu(c_ref[...]), w_ref[0]) + b_ref[0]


def _ada_call(crows, w_ada, b_ada):
    tn = 1024
    d6 = 6 * D_MODEL
    return pl.pallas_call(
        _ada_kernel, grid=(DEPTH, d6 // tn),
        in_specs=[pl.BlockSpec((8, D_MODEL), lambda l, j: (0, 0)),
                  pl.BlockSpec((1, D_MODEL, tn), lambda l, j: (l, 0, j)),
                  pl.BlockSpec((1, 1, tn), lambda l, j: (l, 0, j))],
        out_specs=pl.BlockSpec((1, 8, tn), lambda l, j: (l, 0, j)),
        out_shape=jax.ShapeDtypeStruct((DEPTH, 8, d6), F32),
        compiler_params=_cparams("arbitrary", "arbitrary"), name="ada",
    )(crows, w_ada, b_ada.reshape(DEPTH, 1, d6))


def _lane_block(shape):
    return lax.broadcasted_iota(jnp.int32, shape, 1) // S5_GSIZE


def _chunk_rows_to_group_rows(x, perm):
    r = jnp.dot(perm, x.astype(BF16), preferred_element_type=F32)
    blk = _lane_block((S5_CHUNK, S5_W))
    out = []
    for g in range(S5_G):
        acc = None
        for t in range(S5_CHUNK):
            src = r[t * S5_CHUNK:(t + 1) * S5_CHUNK, :]
            shift = ((t - g) * S5_GSIZE) % S5_W
            src = pltpu.roll(src, shift, 1) if shift else src
            acc = src if acc is None else jnp.where(blk == t, src, acc)
        out.append(acc)
    return out


def _group_rows_to_chunk_rows(ys, perm_t):
    blk = _lane_block((S5_CHUNK, S5_W))
    rows = []
    for t in range(S5_CHUNK):
        acc = None
        for g in range(S5_G):
            shift = ((g - t) * S5_GSIZE) % S5_W
            src = pltpu.roll(ys[g], shift, 1) if shift else ys[g]
            acc = src if acc is None else jnp.where(blk == g, src, acc)
        rows.append(acc)
    z = jnp.concatenate(rows, 0)
    return jnp.dot(perm_t, z.astype(BF16), preferred_element_type=F32)


def _pick(first, second, n_first):
    return jnp.where(pl.program_id(0) < n_first, first[...], second[...])


def _two_group_specs(block, n_first, lead=()):
    nl = len(lead)
    first = pl.BlockSpec(lead + block, lambda i, *_: (0,) * nl + (jnp.minimum(i, n_first - 1), 0))
    second = pl.BlockSpec(lead + block, lambda i, *_: (0,) * nl + (jnp.maximum(i - n_first, 0), 0))
    return [first, second]


def _inproj_kernel(xp_ref, xs_ref, ada_ref, npre_ref, w_ref, qn_ref, kvn_ref, perm_ref,
                   u_ref, cq_ref, ckv_ref, krab_ref, z_ref, xbc_ref, dt_ref, v_ref, *, n_first):
    ada = ada_ref[0]
    h = _rms(_pick(xp_ref, xs_ref, n_first), npre_ref[...]) * (1.0 + ada[1:2]) + ada[0:1]
    p = _bdot(h, w_ref[...])
    for g, ug in enumerate(_chunk_rows_to_group_rows(p[:, P_U:P_Q], perm_ref[...])):
        u_ref[g] = ug.astype(BF16)
    cq_ref[...] = _rms(p[:, P_Q:P_KV], qn_ref[...]).astype(BF16)
    ckv_ref[...] = _rms(p[:, P_KV:P_KRA], kvn_ref[...])
    krab_ref[...] = p[:, P_KRA:P_Z]
    z_ref[...] = p[:, P_Z:P_XBC]
    xbc_ref[...] = p[:, P_XBC:P_DT]
    dt_ref[...] = p[:, P_DT:P_CA]
    v_ref[...] = p[:, P_CA:P_CB] * jax.nn.sigmoid(p[:, P_CB:P_END])


def _chunk_perm():
    idx = np.arange(TM)
    src = (idx % S5_CHUNK) * S5_CHUNK + idx // S5_CHUNK
    return jnp.asarray(np.eye(TM, dtype=np.float32)[src], BF16)


def _inproj_call(xp, xs, ada_t, npre, w, qn, kvn):
    t = xp.shape[0] + xs.shape[0]
    n_first = xp.shape[0] // TM
    row = lambda n: pl.BlockSpec((TM, n), lambda i: (i, 0))
    full = lambda a: pl.BlockSpec(a.shape, lambda i: (0,) * a.ndim)
    widths = (256, 128, 256, 256, 512, 128, 256)
    dts = (BF16, F32, F32, F32, F32, F32, F32)
    perm = _chunk_perm()
    cpt = TM // S5_CHUNK
    return pl.pallas_call(
        functools.partial(_inproj_kernel, n_first=n_first), grid=(t // TM,),
        in_specs=_two_group_specs((TM, D_MODEL), n_first)
        + [pl.BlockSpec((1, 8, D_MODEL), lambda i: (i, 0, 0)), full(npre), full(w), full(qn), full(kvn), full(perm)],
        out_specs=[pl.BlockSpec((S5_G, cpt, S5_W), lambda i: (0, i, 0))] + [row(n) for n in widths],
        out_shape=[jax.ShapeDtypeStruct((S5_G, t // S5_CHUNK, S5_W), BF16)]
        + [jax.ShapeDtypeStruct((t, n), d) for n, d in zip(widths, dts)],
        compiler_params=_cparams("arbitrary"), name="inproj",
    )(xp, xs, ada_t, npre, w, qn, kvn, perm)


def _s5_kernel(u_ref, r_ref, vf_ref, vb_ref, a_ref, h0f_ref, h0fs_ref, h0b_ref, h0bs_ref,
               y_ref, stf_ref, stb_ref, sf, sfs, sb, sbs, *, nb, nc):
    gw = 2 * S5_P
    for g in range(S5_G):
        r = jnp.dot(u_ref[g], r_ref[g, :, S5_W:], preferred_element_type=F32)
        sl = slice(g * gw, (g + 1) * gw)
        sf[:, sl] = r[:, 0:128]
        sfs[:, sl] = r[:, 128:256]
        sb[:, sl] = r[:, 256:384]
        sbs[:, sl] = r[:, 384:512]
    arf, aif, aisf = a_ref[0:1], a_ref[1:2], a_ref[2:3]
    arb, aib, aisb = a_ref[3:4], a_ref[4:5], a_ref[5:6]
    cpt = 1 if nb % SUBLANE == 0 else SUBLANE // nb
    rows = cpt * nb
    nt = nc // cpt

    def body(i, carry):
        hf, hfs, hb, hbs = carry
        rf = pl.ds(pl.multiple_of(i * rows, rows), rows)
        rb = pl.ds(pl.multiple_of((nt - 1 - i) * rows, rows), rows)
        inc_f, inc_fs, inc_b, inc_bs = sf[rf, :], sfs[rf, :], sb[rb, :], sbs[rb, :]
        in_f, in_b = [None] * cpt, [None] * cpt
        for k in range(cpt):
            kf = slice(k * nb, (k + 1) * nb)
            kb = slice((cpt - 1 - k) * nb, (cpt - k) * nb)
            in_f[k] = hf
            in_b[cpt - 1 - k] = hb
            hf, hfs = arf * hf + aif * hfs + inc_f[kf], arf * hfs + aisf * hf + inc_fs[kf]
            hb, hbs = arb * hb + aib * hbs + inc_b[kb], arb * hbs + aisb * hb + inc_bs[kb]
        sf[rf, :] = in_f[0] if cpt == 1 else jnp.concatenate(in_f, 0)
        sb[rb, :] = in_b[0] if cpt == 1 else jnp.concatenate(in_b, 0)
        return hf, hfs, hb, hbs

    hf, _, hb, _ = lax.fori_loop(0, nt, body, (h0f_ref[...], h0fs_ref[...], h0b_ref[...], h0bs_ref[...]))
    stf_ref[...] = hf
    stb_ref[...] = hb
    for g in range(S5_G):
        sl = slice(g * gw, (g + 1) * gw)
        y = (jnp.dot(u_ref[g], r_ref[g, :, :S5_W], preferred_element_type=F32)
             + _bdot(sf[:, sl], vf_ref[g]) + _bdot(sb[:, sl], vb_ref[g]))
        y_ref[g] = y.astype(BF16)


def _s5_call(u, tabs, h0, nb):
    n = u.shape[1]
    r, vf, vb, avec = tabs
    sw = S5_G * 2 * S5_P
    st = jax.ShapeDtypeStruct((nb, sw), F32)
    return pl.pallas_call(
        functools.partial(_s5_kernel, nb=nb, nc=n // nb),
        out_shape=[jax.ShapeDtypeStruct(u.shape, BF16), st, st],
        scratch_shapes=[pltpu.VMEM((n, sw), F32) for _ in range(4)],
        compiler_params=_cparams(), name="s5",
    )(u, r, vf, vb, avec, *h0)


def _cpow(br, bi, e, nbits):
    pr = jnp.ones(e.shape, F32)
    pi = jnp.zeros(e.shape, F32)
    for k in range(nbits):
        bit = ((e >> k) & 1) == 1
        pr, pi = jnp.where(bit, pr * br - pi * bi, pr), jnp.where(bit, pr * bi + pi * br, pi)
        br, bi = br * br - bi * bi, 2.0 * br * bi
    return pr, pi


def _s5tab_kernel(prow_ref, pcol_ref, bt_ref, ct_ref, dsk_ref, etile_ref, r_ref, vf_ref, vb_ref, a_ref):
    q = S5_CHUNK
    hp = lax.Precision.HIGHEST

    def abar(lre, lim, ldt):
        dt = jnp.exp(ldt)
        mag = jnp.exp(lre * dt)
        return mag * jnp.cos(lim * dt), mag * jnp.sin(lim * dt)

    def per_step(x):
        return jnp.dot(x, etile_ref[...], precision=hp, preferred_element_type=F32)

    step_of_col = lax.broadcasted_iota(jnp.int32, (S5_P, S5_W), 1) // S5_GSIZE
    step_of_row = lax.broadcasted_iota(jnp.int32, (S5_W, S5_P), 0) // S5_GSIZE
    kcat, wtabs, arows = [], [], []
    for d in range(2):
        prow = prow_ref[d, 0]
        lre, lim = prow[0:1], prow[1:2]
        abr, abi = abar(lre, lim, prow[2:3])
        den = lre * lre + lim * lim
        fr = ((abr - 1.0) * lre + abi * lim) / den
        fi = (abi * lre - (abr - 1.0) * lim) / den
        btr, bti = bt_ref[d, 0, 0], bt_ref[d, 0, 1]
        bbr, bbi = fr * btr - fi * bti, fr * bti + fi * btr
        pcol = pcol_ref[d, 0]
        cbr, cbi = abar(pcol[:, 0:1], pcol[:, 1:2], pcol[:, 2:3])
        ctr, cti = per_step(ct_ref[d, 0, 0]), per_step(ct_ref[d, 0, 1])

        def c_times_power(e):
            pr, pi = _cpow(cbr, cbi, e, 5)
            return ctr * pr - cti * pi, ctr * pi + cti * pr

        vr, vi = c_times_power(step_of_col + 1 if d == 0 else q - step_of_col)
        (vf_ref if d == 0 else vb_ref)[0] = jnp.concatenate([vr, -vi], 0).astype(BF16)
        lr, li = c_times_power(step_of_col if d == 0 else q - 1 - step_of_col)
        kcat.append(jnp.dot(bbr, lr, precision=hp, preferred_element_type=F32)
                    - jnp.dot(bbi, li, precision=hp, preferred_element_type=F32))
        pr, pi = _cpow(abr, abi, q - 1 - step_of_row if d == 0 else step_of_row, 4)
        tbr, tbi = jnp.concatenate([bbr] * q, 0), jnp.concatenate([bbi] * q, 0)
        wr, wi = pr * tbr - pi * tbi, pr * tbi + pi * tbr
        wtabs += [wr, wi, wi, wr]
        ar, ai = _cpow(abr, abi, jnp.full((1, S5_P), q, jnp.int32), 5)
        arows += [jnp.concatenate([ar, ar], 1), jnp.concatenate([-ai, ai], 1), jnp.concatenate([ai, -ai], 1)]
    lane = lax.broadcasted_iota(jnp.int32, (S5_GSIZE, S5_W), 1)
    rows = []
    for s in range(q):
        shr = S5_GSIZE * s
        shl = S5_GSIZE * (q - 1 - s)
        f = jnp.where(lane >= shr, pltpu.roll(kcat[0], shr, 1), 0.0) if shr else kcat[0]
        b = jnp.where(lane < S5_W - shl, pltpu.roll(kcat[1], S5_W - shl, 1), 0.0) if shl else kcat[1]
        rows.append(f + b)
    ri = lax.broadcasted_iota(jnp.int32, (S5_W, S5_W), 0)
    ci = lax.broadcasted_iota(jnp.int32, (S5_W, S5_W), 1)
    tt = jnp.concatenate(rows, 0) + jnp.where(ri == ci, dsk_ref[0], 0.0)
    r_ref[0] = jnp.concatenate([tt] + wtabs, 1).astype(BF16)
    a_ref[...] = jnp.concatenate(arows + [jnp.zeros((2, 2 * S5_P), F32)], 0)


def _s5_tables(lam_re, lam_im, log_dt, b_re, b_im, c_re, c_im, dskip):
    zeros = jnp.zeros((2, S5_G, 5, S5_P), F32)
    prow = jnp.concatenate([lam_re[:, :, None], lam_im[:, :, None],
                            jnp.broadcast_to(log_dt[:, :, None, None], (2, S5_G, 1, S5_P)), zeros], 2)
    pcol = jnp.swapaxes(prow, 2, 3)
    bt = jnp.stack([jnp.swapaxes(b_re, 2, 3), jnp.swapaxes(b_im, 2, 3)], 2)
    ct = jnp.stack([jnp.swapaxes(c_re, 2, 3), jnp.swapaxes(c_im, 2, 3)], 2)
    dsk = jnp.tile(dskip.reshape(S5_G, 1, S5_GSIZE), (1, 1, S5_CHUNK))
    etile = jnp.asarray(np.tile(np.eye(S5_GSIZE, dtype=np.float32), (1, S5_CHUNK)))
    gblk = lambda a: pl.BlockSpec((2, 1) + a.shape[2:], lambda g: (0, g) + (0,) * (a.ndim - 2))
    sw = S5_G * 2 * S5_P
    return pl.pallas_call(
        _s5tab_kernel, grid=(S5_G,),
        in_specs=[gblk(prow), gblk(pcol), gblk(bt), gblk(ct), pl.BlockSpec((1, 1, S5_W), lambda g: (g, 0, 0)),
                  pl.BlockSpec(etile.shape, lambda g: (0, 0))],
        out_specs=[pl.BlockSpec((1, S5_W, 768), lambda g: (g, 0, 0)), pl.BlockSpec((1, 2 * S5_P, S5_W), lambda g: (g, 0, 0)),
                   pl.BlockSpec((1, 2 * S5_P, S5_W), lambda g: (g, 0, 0)), pl.BlockSpec((8, 2 * S5_P), lambda g: (0, g))],
        out_shape=[jax.ShapeDtypeStruct((S5_G, S5_W, 768), BF16), jax.ShapeDtypeStruct((S5_G, 2 * S5_P, S5_W), BF16),
                   jax.ShapeDtypeStruct((S5_G, 2 * S5_P, S5_W), BF16), jax.ShapeDtypeStruct((8, sw), F32)],
        compiler_params=_cparams("arbitrary"), name="s5tab",
    )(prow, pcol, bt, ct, dsk, etile)


def _s5_rows_in(u, nb):
    g, n, w = u.shape
    return jnp.swapaxes(u.reshape(g, nb, n // nb, w), 1, 2).reshape(g, n, w)


def _s5_rows_out(y, nb):
    g, n, w = y.shape
    return jnp.swapaxes(y.reshape(g, n // nb, nb, w), 1, 2).reshape(g, n, w)


def _s5_state_in(state):
    b = state.shape[0]
    sw = jnp.transpose(state, (0, 1, 2, 4, 3))
    plain = sw.reshape(b, 2, S5_G * 2 * S5_P)
    swapped = sw[:, :, :, ::-1, :].reshape(b, 2, S5_G * 2 * S5_P)
    return plain[:, 0], swapped[:, 0], plain[:, 1], swapped[:, 1]


def _s5_state_out(stf, stb):
    b = stf.shape[0]
    st = jnp.stack([stf, stb], 1).reshape(b, 2, S5_G, 2, S5_P)
    return jnp.transpose(st, (0, 1, 2, 4, 3))


def _mla_kernel(*refs, seq, lctx):
    if lctx:
        (cq_ref, ckv_ref, krab_ref, cosq_ref, sinq_ref, cosk_ref, sink_ref, cckv_ref, ckr_ref,
         wqa_ref, wqb_ref, wk_ref, wv_ref, o_ref, k_scr, v_scr, s_scr) = refs
    else:
        (cq_ref, ckv_ref, krab_ref, cosq_ref, sinq_ref, cosk_ref, sink_ref,
         wqa_ref, wqb_ref, wk_ref, wv_ref, o_ref, k_scr, v_scr, s_scr) = refs
    lk = lctx + seq
    rb = min(seq, 512)

    @pl.when(pl.program_id(1) == 0)
    def _build_keys():
        if lctx:
            ckv = cckv_ref[0]
            kn = _bdot(ckv, wk_ref[...])
            vn = _bdot(ckv, wv_ref[...])
            for h in range(MLA_H):
                k_scr[h, 0:lctx, :] = (kn[:, h * LANE:(h + 1) * LANE] + ckr_ref[0]).astype(BF16)
                v_scr[h, 0:lctx, :] = vn[:, h * LANE:(h + 1) * LANE].astype(BF16)

        def chunk(i, _):
            r0 = pl.multiple_of(i * rb, rb)
            rows = pl.ds(r0, rb)
            ckv = ckv_ref[rows, :]
            kn = _bdot(ckv, wk_ref[...])
            vn = _bdot(ckv, wv_ref[...])
            krab = krab_ref[rows, :]
            krx = krab[:, :LANE] * cosk_ref[rows, :] + krab[:, LANE:] * sink_ref[rows, :]
            dst = pl.ds(pl.multiple_of(lctx + r0, rb if lctx % rb == 0 else 256), rb)
            for h in range(MLA_H):
                k_scr[h, dst, :] = (kn[:, h * LANE:(h + 1) * LANE] + krx).astype(BF16)
                v_scr[h, dst, :] = vn[:, h * LANE:(h + 1) * LANE].astype(BF16)
            return 0

        lax.fori_loop(0, seq // rb, chunk, 0)

    cq = cq_ref[...]
    qa = _bdot(cq, wqa_ref[...])
    qb = _bdot(cq, wqb_ref[...])
    cosq, sinq = cosq_ref[...], sinq_ref[...]
    scale = 1.0 / math.sqrt(MLA_NOPE + MLA_ROPE)
    outs = []
    chunks = [slice(j * MLA_KC, (j + 1) * MLA_KC) for j in range(lk // MLA_KC)]
    for h in range(MLA_H):
        hs = slice(h * LANE, (h + 1) * LANE)
        qh = ((qa[:, hs] * cosq + qb[:, hs] * sinq) * scale).astype(BF16)
        mp = None
        for ks in chunks:
            s = _bdot_nt(qh, k_scr[h, ks, :])
            s_scr[:, ks] = s
            for t in range(MLA_KC // LANE):
                part = s[:, t * LANE:(t + 1) * LANE]
                mp = part if mp is None else jnp.maximum(mp, part)
        m = jnp.max(mp, axis=-1, keepdims=True)
        lp = jnp.zeros((MLA_TQ, LANE), F32)
        acc = jnp.zeros((MLA_TQ, LANE), F32)
        for ks in chunks:
            e = jnp.exp(s_scr[:, ks] - m)
            for t in range(MLA_KC // LANE):
                lp = lp + e[:, t * LANE:(t + 1) * LANE]
            acc = acc + _bdot(e, v_scr[h, ks, :])
        outs.append((acc / jnp.sum(lp, axis=-1, keepdims=True))[:, :MLA_V])
    o_ref[...] = jnp.concatenate(outs, axis=-1)


def _mla_call(cq, ckv, krab, rope, ctx, w, nb, seq, row0):
    lctx = 0 if ctx is None else ctx[0].shape[1]
    nq = seq // MLA_TQ
    q0, s0 = row0 // MLA_TQ, row0 // seq
    cosx, sinx = rope
    qrow = lambda n: pl.BlockSpec((MLA_TQ, n), lambda b, i: (q0 + b * nq + i, 0))
    srow = lambda n: pl.BlockSpec((seq, n), lambda b, i: (s0 + b, 0))
    full = lambda a: pl.BlockSpec(a.shape, lambda b, i: (0,) * a.ndim)
    in_specs = [qrow(256), srow(128), srow(256), pl.BlockSpec((MLA_TQ, LANE), lambda b, i: (i, 0)),
                pl.BlockSpec((MLA_TQ, LANE), lambda b, i: (i, 0)), full(cosx), full(sinx)]
    args = [cq, ckv, krab, cosx, sinx, cosx, sinx]
    if lctx:
        in_specs += [pl.BlockSpec((1, lctx, LANE), lambda b, i: (b, 0, 0))] * 2
        args += list(ctx)
    in_specs += [full(a) for a in w]
    args += list(w)
    lk = lctx + seq
    return pl.pallas_call(
        functools.partial(_mla_kernel, seq=seq, lctx=lctx), grid=(nb, nq),
        in_specs=in_specs,
        out_specs=pl.BlockSpec((MLA_TQ, 256), lambda b, i: (b * nq + i, 0)),
        out_shape=jax.ShapeDtypeStruct((nb * seq, 256), F32),
        scratch_shapes=[pltpu.VMEM((MLA_H, lk, LANE), BF16), pltpu.VMEM((MLA_H, lk, LANE), BF16),
                        pltpu.VMEM((MLA_TQ, lk), F32)],
        compiler_params=_cparams("arbitrary", "arbitrary"), name="mla",
    )(*args)


def _swap8(w):
    q = MLA_ROPE // 4
    return jnp.concatenate([w[..., j * q:(j + 1) * q] for j in (1, 0, 3, 2)], -1)


def _mla_weights(wuq, wuk, wuv):
    z32 = jnp.zeros((MLA_QLORA, 32), F32)
    z64 = jnp.zeros((MLA_QLORA, 64), F32)
    k64 = jnp.zeros((MLA_KVLORA, 64), F32)
    qa, qb, wk, wv = [], [], [], []
    for h in range(MLA_H):
        rope_w = wuq[:, h, MLA_NOPE:]
        qa += [wuq[:, h, :MLA_NOPE], rope_w, z32]
        qb += [z64, _swap8(rope_w), z32]
        wk += [wuk[:, h], k64]
        wv += [wuv[:, h], k64]
    cat = lambda xs: jnp.concatenate(xs, -1).astype(BF16)
    return cat(qa), cat(qb), cat(wk), cat(wv)


def _rope_tables(seq, rotate):
    ones = jnp.ones((seq, MLA_NOPE), F32)
    z32 = jnp.zeros((seq, 32), F32)
    if not rotate:
        return (jnp.concatenate([ones, jnp.ones((seq, MLA_ROPE), F32), z32], -1), jnp.zeros((seq, LANE), F32))
    pos = jnp.arange(seq)
    row = (pos // GRID_W).astype(F32)
    col = (pos % GRID_W).astype(F32)
    half = MLA_ROPE // 2
    inv = ROPE_BASE ** (-jnp.arange(0, half, 2, dtype=F32) / half)
    ang = jnp.concatenate([row[:, None] * inv, col[:, None] * inv], -1)
    cos, sin = jnp.cos(ang), jnp.sin(ang)
    q = MLA_ROPE // 4
    cos32 = jnp.concatenate([cos[:, :q], cos[:, :q], cos[:, q:], cos[:, q:]], -1)
    sin32 = jnp.concatenate([-sin[:, :q], sin[:, :q], -sin[:, q:], sin[:, q:]], -1)
    return jnp.concatenate([ones, cos32, z32], -1), jnp.concatenate([jnp.zeros_like(ones), sin32, z32], -1)


def _cumsum_rows(x, reverse):
    n = x.shape[0]
    row = lax.broadcasted_iota(jnp.int32, x.shape, 0)
    s = 1
    while s < n:
        if reverse:
            x = x + jnp.where(row < n - s, pltpu.roll(x, n - s, 0), 0.0)
        else:
            x = x + jnp.where(row >= s, pltpu.roll(x, s, 0), 0.0)
        s *= 2
    return x


def _ssd_kernel(xbc_ref, z_ref, dt_ref, cw_ref, cb_ref, par_ref, dsk_ref, nw_ref, h0_ref,
                y_ref, st_ref, xc_scr, xt_scr, y_scr, hf_scr, hb_scr, *, seq):
    q = SSD_CHUNK
    nc = seq // q
    row = lax.broadcasted_iota(jnp.int32, (q, 1), 0)

    def conv(c, _):
        t0 = pl.multiple_of(c * q, q)
        cur = xbc_ref[pl.ds(t0, q), :]
        prev8 = xbc_ref[pl.ds(pl.multiple_of(jnp.maximum(t0 - 8, 0), 8), 8), :]
        next8 = xbc_ref[pl.ds(pl.multiple_of(jnp.minimum(t0 + q, seq - 8), 8), 8), :]
        prow = jnp.where(c > 0, prev8[7:8], 0.0)
        nrow = jnp.where(c < nc - 1, next8[0:1], 0.0)
        up = jnp.where(row == 0, prow, pltpu.roll(cur, 1, 0))
        dn = jnp.where(row == q - 1, nrow, pltpu.roll(cur, q - 1, 0))
        acc = up * cw_ref[0:1] + cur * cw_ref[1:2] + dn * cw_ref[2:3] + cb_ref[...]
        xc = jax.nn.silu(acc)
        xc_scr[pl.ds(t0, q), :] = xc
        xt_scr[c] = jnp.transpose(xc[:, :SSD_W])
        y_scr[pl.ds(t0, q), :] = jnp.zeros((q, SSD_W), F32)
        return 0

    lax.fori_loop(0, nc, conv, 0)

    pair_rows = (SSD_H // SSD_G) * SSD_P
    for d, h_scr in enumerate((hf_scr, hb_scr)):
        h_scr[...] = jnp.zeros_like(h_scr)
        for g in range(SSD_G):
            h_scr[g * pair_rows:(g + 1) * pair_rows, g * SSD_N:(g + 1) * SSD_N] = (
                h0_ref[0, d, g * pair_rows:(g + 1) * pair_rows, :])
    ii = lax.broadcasted_iota(jnp.int32, (q, q), 0)
    jj = lax.broadcasted_iota(jnp.int32, (q, q), 1)
    first_half = lax.broadcasted_iota(jnp.int32, (q, LANE), 1) < SSD_N

    def one_chunk(c, d, h_scr):
        t0 = pl.multiple_of(c * q, q)
        xc = xc_scr[pl.ds(t0, q), :]
        xs_t = xt_scr[c]
        dt = jax.nn.softplus(dt_ref[pl.ds(t0, q), :] + par_ref[d:d + 1])
        cs = _cumsum_rows(dt * par_ref[2 + d:3 + d], reverse=(d == 1))
        edge = cs[q - 1:q] if d == 0 else cs[0:1]
        cs_t = jnp.transpose(cs)
        dt_t = jnp.transpose(dt)
        wts_t = jnp.transpose(jnp.exp(edge - cs) * dt)
        decay = jnp.exp(edge)
        mask = (ii >= jj) if d == 0 else (ii <= jj)
        bm_pair = xc[:, SSD_W:SSD_W + LANE]
        cm_pair = xc[:, SSD_W + LANE:SSD_W + 2 * LANE]
        for g in range(SSD_G):
            in_g = first_half if g == 0 else jnp.logical_not(first_half)
            pair = slice(g * LANE, (g + 1) * LANE)
            heads = [g * (SSD_H // SSD_G) + hh for hh in range(SSD_H // SSD_G)]
            cm = jnp.where(in_g, cm_pair, 0.0)
            bm = jnp.where(in_g, bm_pair, 0.0)
            cbm = _bdot_nt(cm, bm_pair)
            cs_i = [jnp.transpose(jnp.broadcast_to(cs_t[h:h + 1, :], (q, q))) for h in heads]
            y_off = _bdot_nt(cm, h_scr[pair, :]) * jnp.exp(jnp.where(first_half, cs_i[0], cs_i[1]))
            ys = []
            for hh, h in enumerate(heads):
                lmat = jnp.exp(jnp.where(mask, cs_i[hh] - cs_t[h:h + 1, :], -jnp.inf))
                ys.append(_bdot(cbm * lmat * dt_t[h:h + 1, :], xc[:, pair]))
                ps = slice(h * SSD_P, (h + 1) * SSD_P)
                st = _bdot(xs_t[ps, :] * wts_t[h:h + 1, :], bm)
                h_scr[ps, :] = decay[:, h:h + 1] * h_scr[ps, :] + st
            y_scr[pl.ds(t0, q), pair] += jnp.where(first_half, ys[0], ys[1]) + y_off

    def chunks(c, _):
        one_chunk(c, 0, hf_scr)
        one_chunk(nc - 1 - c, 1, hb_scr)
        return 0

    lax.fori_loop(0, nc, chunks, 0, unroll=2)
    for d, h_scr in enumerate((hf_scr, hb_scr)):
        for g in range(SSD_G):
            st_ref[0, d, g * pair_rows:(g + 1) * pair_rows, :] = (
                h_scr[g * pair_rows:(g + 1) * pair_rows, g * SSD_N:(g + 1) * SSD_N])

    def finish(c, _):
        rows = pl.ds(pl.multiple_of(c * q, q), q)
        y = y_scr[rows, :] + dsk_ref[...] * xc_scr[rows, 0:SSD_W]
        y_ref[rows, :] = _rms(y * jax.nn.silu(z_ref[rows, :]), nw_ref[...])
        return 0

    lax.fori_loop(0, nc, finish, 0)


def _ssd_call(xbc, z, dt, w, h0, nb, seq, row0):
    s0 = row0 // seq
    srow = lambda n: pl.BlockSpec((seq, n), lambda b: (s0 + b, 0))
    full = lambda a: pl.BlockSpec(a.shape, lambda b: (0,) * a.ndim)
    hp = SSD_H * SSD_P
    st_spec = pl.BlockSpec((1, 2, hp, SSD_N), lambda b: (b, 0, 0, 0))
    return pl.pallas_call(
        functools.partial(_ssd_kernel, seq=seq), grid=(nb,),
        in_specs=[srow(SSD_XBC), srow(SSD_W), srow(LANE)] + [full(a) for a in w] + [st_spec],
        out_specs=[pl.BlockSpec((seq, SSD_W), lambda b: (b, 0)), st_spec],
        out_shape=[jax.ShapeDtypeStruct((nb * seq, SSD_W), F32), jax.ShapeDtypeStruct((nb, 2, hp, SSD_N), F32)],
        scratch_shapes=[pltpu.VMEM((seq, SSD_XBC), F32), pltpu.VMEM((seq // SSD_CHUNK, SSD_W, SSD_CHUNK), F32),
                        pltpu.VMEM((seq, SSD_W), F32), pltpu.VMEM((hp, LANE), F32), pltpu.VMEM((hp, LANE), F32)],
        compiler_params=_cparams("arbitrary"), name="ssd",
    )(xbc, z, dt, *w, h0)


def _ssd_weights(conv_w, conv_b, dt_bias, a_log, dskip, norm_w):
    pad = lambda x: jnp.pad(x, ((0, 0), (0, LANE - SSD_H)))
    par = jnp.concatenate([pad(dt_bias), pad(-jnp.exp(a_log)), jnp.zeros((4, LANE), F32)], 0)
    cw = jnp.concatenate([conv_w, jnp.zeros((5, SSD_XBC), F32)], 0)
    return (cw, conv_b.reshape(1, SSD_XBC), par, jnp.repeat(dskip, SSD_P).reshape(1, SSD_W),
            norm_w.reshape(1, SSD_W))


def _conf_kernel(v_ref, w_ref, b_ref, g_ref, bt_ref, o_ref, pad_scr, sh_scr, *, seq, seg):
    q = CONF_Q
    halo = 2 * CONF_PAD
    tail = pad_scr.shape[0] - seq - CONF_PAD
    pad_scr[0:CONF_PAD, :] = jnp.zeros((CONF_PAD, CONF_W), F32)
    pad_scr[seq + CONF_PAD:, :] = jnp.zeros((tail, CONF_W), F32)
    pad_scr[CONF_PAD:seq + CONF_PAD, :] = v_ref[...]
    lo = CONF_PAD - CONF_K // 2
    pieces = [(j, min(LANE, seg + halo - j)) for j in range(0, seg + halo, LANE)]

    def segment(s, _):
        s0 = pl.multiple_of(s * seg, seg)
        for j0, n in pieces:
            big = pad_scr[pl.ds(s0 + j0, n + SUBLANE), :]
            for r in range(SUBLANE):
                sh_scr[r, j0:j0 + n, :] = big[r:r + n]

        def chunk(c, _):
            t0 = pl.multiple_of(c * q, q)
            acc = jnp.zeros((q, CONF_W), F32) + b_ref[...]
            for k in range(CONF_K):
                r, off = (lo + k) % SUBLANE, (lo + k) // SUBLANE * SUBLANE
                wk = jnp.concatenate([w_ref[k * SUBLANE:(k + 1) * SUBLANE, :]] * (q // SUBLANE), 0)
                acc = acc + sh_scr[r, pl.ds(pl.multiple_of(t0 + off, SUBLANE), q), :] * wk
            mu = jnp.mean(acc, axis=-1, keepdims=True)
            xc = acc - mu
            var = jnp.mean(xc * xc, axis=-1, keepdims=True)
            o_ref[pl.ds(s0 + t0, q), :] = jax.nn.silu(xc * lax.rsqrt(var + EPS) * g_ref[...] + bt_ref[...])
            return 0

        lax.fori_loop(0, seg // q, chunk, 0)
        return 0

    lax.fori_loop(0, seq // seg, segment, 0)


def _conf_call(v, w, nb, seq, row0):
    s0 = row0 // seq
    full = lambda a: pl.BlockSpec(a.shape, lambda b: (0,) * a.ndim)
    seg = min(seq, CONF_SEG)
    halo = 2 * CONF_PAD
    return pl.pallas_call(
        functools.partial(_conf_kernel, seq=seq, seg=seg), grid=(nb,),
        in_specs=[pl.BlockSpec((seq, CONF_W), lambda b: (s0 + b, 0))] + [full(a) for a in w],
        out_specs=pl.BlockSpec((seq, CONF_W), lambda b: (b, 0)),
        out_shape=jax.ShapeDtypeStruct((nb * seq, CONF_W), F32),
        scratch_shapes=[pltpu.VMEM((seq + halo + 2 * SUBLANE, CONF_W), F32),
                        pltpu.VMEM((SUBLANE, seg + halo, CONF_W), F32)],
        compiler_params=_cparams("arbitrary"), name="conf",
    )(v, *w)


def _split_bf16(x):
    hi = x.astype(BF16)
    return hi, (x - hi.astype(F32)).astype(BF16)


def _outproj_kernel(xp_ref, xs_ref, ada_ref, s5p_ref, s5s_ref, mlap_ref, mlas_ref, ssdp_ref, ssds_ref, confp_ref,
                    confs_ref, wglu_ref, bglu_ref, wo_ref, npost_ref, npre_ref, wrh_ref, wrl_ref, br_ref, perm_ref,
                    x1_ref, h2_ref, gate_ref, *, n_first):
    ada = ada_ref[0]
    ys5 = _pick(s5p_ref, s5s_ref, n_first)
    a = jax.nn.gelu(_group_rows_to_chunk_rows([ys5[g].astype(F32) for g in range(S5_G)], perm_ref[...]))
    s5 = a * jax.nn.sigmoid(_bdot(a, wglu_ref[...]) + bglu_ref[...])
    mix = (_bdot(s5, wo_ref[0:256, :]) + _bdot(_pick(mlap_ref, mlas_ref, n_first), wo_ref[256:512, :])
           + _bdot(_pick(ssdp_ref, ssds_ref, n_first), wo_ref[512:768, :])
           + _bdot(_pick(confp_ref, confs_ref, n_first), wo_ref[768:1024, :]))
    x1 = _pick(xp_ref, xs_ref, n_first) + ada[2:3] * _rms(mix, npost_ref[...])
    x1_ref[...] = x1
    h2 = _rms(x1, npre_ref[...]) * (1.0 + ada[4:5]) + ada[3:4]
    h2_ref[...] = h2.astype(BF16)
    hi, lo = _split_bf16(h2)
    lg = (jnp.dot(hi, wrh_ref[...], preferred_element_type=F32) + jnp.dot(lo, wrh_ref[...], preferred_element_type=F32)
          + jnp.dot(hi, wrl_ref[...], preferred_element_type=F32) + br_ref[...])
    lane = lax.broadcasted_iota(jnp.int32, lg.shape, 1)
    neg = -jnp.inf
    big = jnp.int32(1 << 20)
    first = lambda hit: jnp.min(jnp.where(hit, lane, big), axis=-1, keepdims=True)
    glm = jnp.where((lane >= MOE_E) & (lane < MOE_E + MOE_GROUPS), lg, neg)
    gmax = jnp.max(glm, axis=-1, keepdims=True)
    p_group = 1.0 / jnp.sum(jnp.exp(glm - gmax), axis=-1, keepdims=True)
    gsel = first(glm == gmax) - MOE_E
    elm = jnp.where((lane < MOE_E) & ((lane // MOE_PER_GROUP) == gsel), lg, neg)
    v1 = jnp.max(elm, axis=-1, keepdims=True)
    i1 = first(elm == v1)
    elm2 = jnp.where(lane == i1, neg, elm)
    v2 = jnp.max(elm2, axis=-1, keepdims=True)
    i2 = first(elm2 == v2)
    e2 = jnp.exp(v2 - v1)
    w1 = p_group / (1.0 + e2)
    w2 = p_group * e2 / (1.0 + e2)
    gate_ref[...] = jnp.where(lane == i1, w1, 0.0) + jnp.where(lane == i2, w2, 0.0)


def _outproj_call(x, ada_t, ys5, ymla, yssd, yconf, w):
    t = x[0].shape[0] + x[1].shape[0]
    n_first = x[0].shape[0] // TM
    row = lambda n: pl.BlockSpec((TM, n), lambda i: (i, 0))
    full = lambda a: pl.BlockSpec(a.shape, lambda i: (0,) * a.ndim)
    w = tuple(w) + (_chunk_perm(),)
    pair = lambda n: _two_group_specs((TM, n), n_first)
    return pl.pallas_call(
        functools.partial(_outproj_kernel, n_first=n_first), grid=(t // TM,),
        in_specs=pair(D_MODEL) + [pl.BlockSpec((1, 8, D_MODEL), lambda i: (i, 0, 0))]
        + _two_group_specs((TM // S5_CHUNK, S5_W), n_first, lead=(S5_G,)) + pair(256) + pair(256) + pair(256)
        + [full(a) for a in w],
        out_specs=[row(D_MODEL), row(D_MODEL), row(LANE)],
        out_shape=[jax.ShapeDtypeStruct((t, D_MODEL), F32), jax.ShapeDtypeStruct((t, D_MODEL), BF16),
                   jax.ShapeDtypeStruct((t, LANE), F32)],
        compiler_params=_cparams("arbitrary"), name="outproj",
    )(*x, ada_t, *ys5, *ymla, *yssd, *yconf, *w)


def _moe_kernel(h_ref, gate_ref, x1_ref, ada_ref, wg_ref, wu_ref, wd_ref, npost_ref, op_ref, os_ref, hid_scr,
                *, n_first):
    i = pl.program_id(0)
    h = h_ref[...]
    gate = gate_ref[...]
    lane = lax.broadcasted_iota(jnp.int32, gate.shape, 1)
    for e in range(MOE_E):
        ge = jnp.sum(jnp.where(lane == e, gate, 0.0), axis=-1, keepdims=True)
        hid = (jax.nn.silu(jnp.dot(h, wg_ref[e], preferred_element_type=F32))
               * jnp.dot(h, wu_ref[e], preferred_element_type=F32) * ge)
        hid_scr[:, e * MOE_HID:(e + 1) * MOE_HID] = hid.astype(BF16)
    out = jnp.dot(hid_scr[...], wd_ref[...], preferred_element_type=F32)
    res = x1_ref[...] + ada_ref[0][5:6] * _rms(out, npost_ref[...])

    @pl.when(i < n_first)
    def _store_first():
        op_ref[...] = res

    @pl.when(i >= n_first)
    def _store_second():
        os_ref[...] = res


def _moe_call(h2, gate, x1, ada_t, wg, wu, wd, npost, t_first):
    t = h2.shape[0]
    tm = MOE_TM
    n_first = t_first // tm
    row = lambda n: pl.BlockSpec((tm, n), lambda i: (i, 0))
    resident = lambda a: pl.BlockSpec(a.shape, lambda i: (0,) * a.ndim, pipeline_mode=pl.Buffered(1))
    wd = wd.reshape(MOE_E * MOE_HID, D_MODEL)
    return pl.pallas_call(
        functools.partial(_moe_kernel, n_first=n_first), grid=(t // tm,),
        in_specs=[row(D_MODEL), row(LANE), row(D_MODEL),
                  pl.BlockSpec((1, 8, D_MODEL), lambda i: (i * (tm // TM), 0, 0)),
                  resident(wg), resident(wu), resident(wd), pl.BlockSpec(npost.shape, lambda i: (0, 0))],
        out_specs=_two_group_specs((tm, D_MODEL), n_first),
        out_shape=[jax.ShapeDtypeStruct((t_first, D_MODEL), F32), jax.ShapeDtypeStruct((t - t_first, D_MODEL), F32)],
        scratch_shapes=[pltpu.VMEM((tm, MOE_E * MOE_HID), BF16)],
        compiler_params=_cparams("arbitrary"), name="moe",
    )(h2, gate, x1, ada_t, wg, wu, wd, npost)


def _pack_w_in(w):
    d = w.shape[0]
    z32, z64 = jnp.zeros((d, 32), F32), jnp.zeros((d, 64), F32)
    kr = w[:, OFF_MLA_KR:OFF_SSD_Z]
    dtw = jnp.pad(w[:, OFF_SSD_DT:OFF_CONF], ((0, 0), (0, LANE - SSD_H)))
    cols = [w[:, OFF_S5:OFF_MLA_Q], w[:, OFF_MLA_Q:OFF_MLA_KV], w[:, OFF_MLA_KV:OFF_MLA_KR],
            z64, kr, z32, z64, _swap8(kr), z32,
            w[:, OFF_SSD_Z:OFF_SSD_XBC], w[:, OFF_SSD_XBC:OFF_SSD_DT], dtw, w[:, OFF_CONF:IN_COLS]]
    return jnp.concatenate(cols, -1).astype(BF16)


def _row(v):
    return v.reshape(1, -1)


def kernel(x_prompt, x_sample, cache_mla_ckv, cache_mla_krope, state_s5, state_ssd, c, c_ctx, w_ada, b_ada, norm_pre1, norm_post1, norm_pre2, norm_post2, w_in, w_out, s5_lam_re, s5_lam_im, s5_log_dt, s5_b_re, s5_b_im, s5_c_re, s5_c_im, s5_d, s5_w_glu, s5_b_glu, mla_qnorm, mla_kvnorm, mla_wuq, mla_wuk, mla_wuv, ssd_conv_w, ssd_conv_b, ssd_dt_bias, ssd_a_log, ssd_d, ssd_norm_w, conf_dw_w, conf_dw_b, conf_ln_g, conf_ln_b, moe_wg, moe_bg, moe_we, moe_be, moe_w_gate, moe_w_up, moe_w_down):
    bp, lp, d = x_prompt.shape
    bs, ls, _ = x_sample.shape
    tp, ts = bp * lp, bs * ls
    x = (x_prompt.reshape(tp, d), x_sample.reshape(ts, d))

    crows = jnp.concatenate([c_ctx[None], c, jnp.zeros((8 - 1 - bs, d), F32)], 0)
    ada = _ada_call(crows, w_ada, b_ada)
    tile_row = np.concatenate([np.zeros(tp // TM, np.int32), 1 + np.repeat(np.arange(bs, dtype=np.int32), ls // TM)])

    rope_p = _rope_tables(lp, rotate=False)
    rope_s = _rope_tables(ls, rotate=True)
    zeros_s5 = tuple(jnp.zeros((bp, S5_G * 2 * S5_P), F32) for _ in range(4))
    zeros_ssd = jnp.zeros((bp, 2, SSD_H * SSD_P, SSD_N), F32)
    pad_kr = lambda kr: jnp.pad(kr, ((0, 0), (0, 0), (MLA_NOPE, LANE - MLA_NOPE - MLA_ROPE)))
    moe_w = [w.astype(BF16) for w in (moe_w_gate, moe_w_up, moe_w_down)]

    ckv_l, kr_l, s5_l, ssd_l = [], [], [], []
    for l in range(DEPTH):
        ada_t = jnp.pad(ada[l][tile_row].reshape(-1, 6, d), ((0, 0), (0, 2), (0, 0)))
        u, cq, ckv, krab, z, xbc, dt, v = _inproj_call(
            *x, ada_t, _row(norm_pre1[l]), _pack_w_in(w_in[l]), _row(mla_qnorm[l]), _row(mla_kvnorm[l]))

        tabs = _s5_tables(s5_lam_re[l], s5_lam_im[l], s5_log_dt[l], s5_b_re[l], s5_b_im[l], s5_c_re[l], s5_c_im[l],
                          s5_d[l])
        ncp = tp // S5_CHUNK
        yp, stf, stb = _s5_call(_s5_rows_in(u[:, :ncp], bp), tabs, zeros_s5, bp)
        ys, _, _ = _s5_call(_s5_rows_in(u[:, ncp:], bs), tabs, _s5_state_in(state_s5[:, l]), bs)
        y_s5 = (_s5_rows_out(yp, bp), _s5_rows_out(ys, bs))
        s5_l.append(_s5_state_out(stf, stb))

        mw = _mla_weights(mla_wuq[l], mla_wuk[l], mla_wuv[l])
        y_mla = (_mla_call(cq, ckv, krab, rope_p, None, mw, bp, lp, 0),
                 _mla_call(cq, ckv, krab, rope_s, (cache_mla_ckv[:, l], pad_kr(cache_mla_krope[:, l])), mw, bs, ls, tp))
        ckv_l.append(ckv[:tp].reshape(bp, lp, MLA_KVLORA))
        kr_l.append(krab[:tp, MLA_NOPE:MLA_NOPE + MLA_ROPE].reshape(bp, lp, MLA_ROPE))

        sw = _ssd_weights(ssd_conv_w[l], ssd_conv_b[l], ssd_dt_bias[l], ssd_a_log[l], ssd_d[l], ssd_norm_w[l])
        yssd_p, st_ssd = _ssd_call(xbc, z, dt, sw, zeros_ssd, bp, lp, 0)
        yssd_s, _ = _ssd_call(xbc, z, dt, sw, state_ssd[:, l].reshape(bs, 2, SSD_H * SSD_P, SSD_N), bs, ls, tp)
        ssd_l.append(st_ssd.reshape(bp, 2, SSD_H, SSD_P, SSD_N))

        cw = (jnp.repeat(conf_dw_w[l], SUBLANE, axis=0), _row(conf_dw_b[l]),
              _row(conf_ln_g[l]), _row(conf_ln_b[l]))
        y_conf = (_conf_call(v, cw, bp, lp, 0), _conf_call(v, cw, bs, ls, tp))

        wr = jnp.pad(jnp.concatenate([moe_we[l], moe_wg[l]], -1), ((0, 0), (0, LANE - MOE_E - MOE_GROUPS)))
        wrh, wrl = _split_bf16(wr)
        br = jnp.pad(jnp.concatenate([moe_be[l], moe_bg[l]]), (0, LANE - MOE_E - MOE_GROUPS)).reshape(1, LANE)
        ow = (s5_w_glu[l].astype(BF16), _row(s5_b_glu[l]), w_out[l].astype(BF16), _row(norm_post1[l]),
              _row(norm_pre2[l]), wrh, wrl, br)
        x1, h2, gate = _outproj_call(x, ada_t, y_s5, y_mla, (yssd_p, yssd_s), y_conf, ow)

        x = _moe_call(h2, gate, x1, ada_t, moe_w[0][l], moe_w[1][l], moe_w[2][l], _row(norm_post2[l]), tp)

    return (x[0].reshape(bp, lp, d), x[1].reshape(bs, ls, d),
            jnp.stack(ckv_l, 1), jnp.stack(kr_l, 1), jnp.stack(s5_l, 1), jnp.stack(ssd_l, 1))
```

```python
import functools
import math

import jax
import jax.numpy as jnp
import numpy as np
from jax import lax
from jax.experimental import pallas as pl
from jax.experimental.pallas import tpu as pltpu

F32 = jnp.float32
BF16 = jnp.bfloat16

D_MODEL = 1024
DEPTH = 2
GRID_W = 64
EPS = 1e-6

S5_W = 256
S5_GSIZE = 16
S5_G = 16
S5_P = 64
S5_CHUNK = 16

MLA_H = 4
MLA_NOPE = 64
MLA_ROPE = 32
MLA_V = 64
MLA_QLORA = 256
MLA_KVLORA = 128
ROPE_BASE = 10000.0
MLA_TQ = 256
MLA_KC = 256

SSD_W = 256
SSD_P = 64
SSD_H = 4
SSD_G = 2
SSD_N = 64
SSD_CHUNK = 128
SSD_XBC = 512

CONF_W = 256
CONF_K = 31
CONF_Q = 128
CONF_PAD = 16
CONF_SEG = 1024

MOE_GROUPS = 4
MOE_PER_GROUP = 4
MOE_E = 16
MOE_HID = 256
MOE_TM = 512

OFF_S5 = 0
OFF_MLA_Q = 256
OFF_MLA_KV = 512
OFF_MLA_KR = 640
OFF_SSD_Z = 672
OFF_SSD_XBC = 928
OFF_SSD_DT = 1440
OFF_CONF = 1444
IN_COLS = 1956

LANE = 128
SUBLANE = 8
TM = 256
TOKEN_SUB = 4
P_U, P_Q, P_KV, P_KRA, P_KRB, P_Z, P_XBC, P_DT, P_CA, P_CB, P_END = (
    0, 256, 512, 640, 768, 896, 1152, 1664, 1792, 2048, 2304)

VMEM_LIMIT = 56 * 1024 * 1024


def _bdot(a, b):
    return jnp.dot(a.astype(BF16), b.astype(BF16), preferred_element_type=F32)


def _bdot_nt(a, b):
    return lax.dot_general(a.astype(BF16), b.astype(BF16), (((1,), (1,)), ((), ())), preferred_element_type=F32)


def _rms(x, g):
    return x * lax.rsqrt(jnp.mean(x * x, axis=-1, keepdims=True) + EPS) * g


def _cparams(*sem):
    return pltpu.CompilerParams(dimension_semantics=sem if sem else None, vmem_limit_bytes=VMEM_LIMIT)


def _ada_kernel(c_ref, w_ref, b_ref, o_ref):
    o_ref[0] = _bdot(jax.nn.silu(c_ref[...]), w_ref[0]) + b_ref[0]


def _ada_call(crows, w_ada, b_ada):
    tn = 1024
    d6 = 6 * D_MODEL
    return pl.pallas_call(
        _ada_kernel, grid=(DEPTH, d6 // tn),
        in_specs=[pl.BlockSpec((8, D_MODEL), lambda l, j: (0, 0)),
                  pl.BlockSpec((1, D_MODEL, tn), lambda l, j: (l, 0, j)),
                  pl.BlockSpec((1, 1, tn), lambda l, j: (l, 0, j))],
        out_specs=pl.BlockSpec((1, 8, tn), lambda l, j: (l, 0, j)),
        out_shape=jax.ShapeDtypeStruct((DEPTH, 8, d6), F32),
        compiler_params=_cparams("arbitrary", "arbitrary"), name="ada",
    )(crows, w_ada, b_ada.reshape(DEPTH, 1, d6))


def _lane_block(shape):
    return lax.broadcasted_iota(jnp.int32, shape, 1) // S5_GSIZE


def _chunk_rows_to_group_rows(x, perm):
    r = jnp.dot(perm, x.astype(BF16), preferred_element_type=F32)
    blk = _lane_block((S5_CHUNK, S5_W))
    out = []
    for g in range(S5_G):
        acc = None
        for t in range(S5_CHUNK):
            src = r[t * S5_CHUNK:(t + 1) * S5_CHUNK, :]
            shift = ((t - g) * S5_GSIZE) % S5_W
            src = pltpu.roll(src, shift, 1) if shift else src
            acc = src if acc is None else jnp.where(blk == t, src, acc)
        out.append(acc)
    return out


def _group_rows_to_chunk_rows(ys, perm_t):
    blk = _lane_block((S5_CHUNK, S5_W))
    rows = []
    for t in range(S5_CHUNK):
        acc = None
        for g in range(S5_G):
            shift = ((g - t) * S5_GSIZE) % S5_W
            src = pltpu.roll(ys[g], shift, 1) if shift else ys[g]
            acc = src if acc is None else jnp.where(blk == g, src, acc)
        rows.append(acc)
    z = jnp.concatenate(rows, 0)
    return jnp.dot(perm_t, z.astype(BF16), preferred_element_type=F32)


def _two_group_specs(block, n_first, lead=()):
    nl = len(lead)
    first = pl.BlockSpec(lead + block, lambda i, *_: (0,) * nl + (jnp.minimum(i, n_first - 1), 0))
    second = pl.BlockSpec(lead + block, lambda i, *_: (0,) * nl + (jnp.maximum(i - n_first, 0), 0))
    return [first, second]


def _inproj_kernel(xp_ref, xs_ref, ada_ref, npre_ref, w_ref, qn_ref, kvn_ref, perm_ref,
                   u_ref, cq_ref, ckv_ref, krab_ref, z_ref, xbc_ref, dt_ref, v_ref, *, n_first):
    cpt = TM // S5_CHUNK
    for s in range(TOKEN_SUB):
        rows = slice(s * TM, (s + 1) * TM)
        ada = ada_ref[s]
        x = jnp.where(pl.program_id(0) < n_first, xp_ref[rows, :], xs_ref[rows, :])
        h = _rms(x, npre_ref[...]) * (1.0 + ada[1:2]) + ada[0:1]
        p = _bdot(h, w_ref[...])
        for g, ug in enumerate(_chunk_rows_to_group_rows(p[:, P_U:P_Q], perm_ref[...])):
            u_ref[g, s * cpt:(s + 1) * cpt, :] = ug.astype(BF16)
        cq_ref[rows, :] = _rms(p[:, P_Q:P_KV], qn_ref[...]).astype(BF16)
        ckv_ref[rows, :] = _rms(p[:, P_KV:P_KRA], kvn_ref[...])
        krab_ref[rows, :] = p[:, P_KRA:P_Z]
        z_ref[rows, :] = p[:, P_Z:P_XBC]
        xbc_ref[rows, :] = p[:, P_XBC:P_DT]
        dt_ref[rows, :] = p[:, P_DT:P_CA]
        v_ref[rows, :] = p[:, P_CA:P_CB] * jax.nn.sigmoid(p[:, P_CB:P_END])


def _chunk_perm():
    idx = np.arange(TM)
    src = (idx % S5_CHUNK) * S5_CHUNK + idx // S5_CHUNK
    return jnp.asarray(np.eye(TM, dtype=np.float32)[src], BF16)


def _inproj_call(xp, xs, ada_t, npre, w, qn, kvn):
    t = xp.shape[0] + xs.shape[0]
    tm = TOKEN_SUB * TM
    n_first = xp.shape[0] // tm
    row = lambda n: pl.BlockSpec((tm, n), lambda i: (i, 0))
    full = lambda a: pl.BlockSpec(a.shape, lambda i: (0,) * a.ndim)
    widths = (256, 128, 256, 256, 512, 128, 256)
    dts = (BF16, F32, F32, F32, F32, F32, F32)
    perm = _chunk_perm()
    cpt = tm // S5_CHUNK
    return pl.pallas_call(
        functools.partial(_inproj_kernel, n_first=n_first), grid=(t // tm,),
        in_specs=_two_group_specs((tm, D_MODEL), n_first)
        + [pl.BlockSpec((TOKEN_SUB, 8, D_MODEL), lambda i: (i, 0, 0)), full(npre), full(w), full(qn), full(kvn),
           full(perm)],
        out_specs=[pl.BlockSpec((S5_G, cpt, S5_W), lambda i: (0, i, 0))] + [row(n) for n in widths],
        out_shape=[jax.ShapeDtypeStruct((S5_G, t // S5_CHUNK, S5_W), BF16)]
        + [jax.ShapeDtypeStruct((t, n), d) for n, d in zip(widths, dts)],
        compiler_params=_cparams("arbitrary"), name="inproj",
    )(xp, xs, ada_t, npre, w, qn, kvn, perm)


def _s5_kernel(u_ref, r_ref, vf_ref, vb_ref, a_ref, h0f_ref, h0fs_ref, h0b_ref, h0bs_ref,
               y_ref, stf_ref, stb_ref, sf, sfs, sb, sbs, *, nb, nc):
    gw = 2 * S5_P
    for g in range(S5_G):
        r = jnp.dot(u_ref[g], r_ref[g, :, S5_W:], preferred_element_type=F32)
        sl = slice(g * gw, (g + 1) * gw)
        sf[:, sl] = r[:, 0:128]
        sfs[:, sl] = r[:, 128:256]
        sb[:, sl] = r[:, 256:384]
        sbs[:, sl] = r[:, 384:512]
    arf, aif, aisf = a_ref[0:1], a_ref[1:2], a_ref[2:3]
    arb, aib, aisb = a_ref[3:4], a_ref[4:5], a_ref[5:6]
    cpt = 1 if nb % SUBLANE == 0 else SUBLANE // nb
    rows = cpt * nb
    nt = nc // cpt

    def body(i, carry):
        hf, hfs, hb, hbs = carry
        rf = pl.ds(pl.multiple_of(i * rows, rows), rows)
        rb = pl.ds(pl.multiple_of((nt - 1 - i) * rows, rows), rows)
        inc_f, inc_fs, inc_b, inc_bs = sf[rf, :], sfs[rf, :], sb[rb, :], sbs[rb, :]
        in_f, in_b = [None] * cpt, [None] * cpt
        for k in range(cpt):
            kf = slice(k * nb, (k + 1) * nb)
            kb = slice((cpt - 1 - k) * nb, (cpt - k) * nb)
            in_f[k] = hf
            in_b[cpt - 1 - k] = hb
            hf, hfs = arf * hf + aif * hfs + inc_f[kf], arf * hfs + aisf * hf + inc_fs[kf]
            hb, hbs = arb * hb + aib * hbs + inc_b[kb], arb * hbs + aisb * hb + inc_bs[kb]
        sf[rf, :] = in_f[0] if cpt == 1 else jnp.concatenate(in_f, 0)
        sb[rb, :] = in_b[0] if cpt == 1 else jnp.concatenate(in_b, 0)
        return hf, hfs, hb, hbs

    hf, _, hb, _ = lax.fori_loop(0, nt, body, (h0f_ref[...], h0fs_ref[...], h0b_ref[...], h0bs_ref[...]))
    stf_ref[...] = hf
    stb_ref[...] = hb
    for g in range(S5_G):
        sl = slice(g * gw, (g + 1) * gw)
        y = (jnp.dot(u_ref[g], r_ref[g, :, :S5_W], preferred_element_type=F32)
             + _bdot(sf[:, sl], vf_ref[g]) + _bdot(sb[:, sl], vb_ref[g]))
        y_ref[g] = y.astype(BF16)


def _s5_call(u, tabs, h0, nb):
    n = u.shape[1]
    r, vf, vb, avec = tabs
    sw = S5_G * 2 * S5_P
    st = jax.ShapeDtypeStruct((nb, sw), F32)
    return pl.pallas_call(
        functools.partial(_s5_kernel, nb=nb, nc=n // nb),
        out_shape=[jax.ShapeDtypeStruct(u.shape, BF16), st, st],
        scratch_shapes=[pltpu.VMEM((n, sw), F32) for _ in range(4)],
        compiler_params=_cparams(), name="s5",
    )(u, r, vf, vb, avec, *h0)


def _cpow(br, bi, e, nbits):
    pr = jnp.ones(e.shape, F32)
    pi = jnp.zeros(e.shape, F32)
    for k in range(nbits):
        bit = ((e >> k) & 1) == 1
        pr, pi = jnp.where(bit, pr * br - pi * bi, pr), jnp.where(bit, pr * bi + pi * br, pi)
        br, bi = br * br - bi * bi, 2.0 * br * bi
    return pr, pi


def _s5tab_kernel(prow_ref, pcol_ref, bt_ref, ct_ref, dsk_ref, etile_ref, r_ref, vf_ref, vb_ref, a_ref):
    q = S5_CHUNK
    hp = lax.Precision.HIGHEST

    def abar(lre, lim, ldt):
        dt = jnp.exp(ldt)
        mag = jnp.exp(lre * dt)
        return mag * jnp.cos(lim * dt), mag * jnp.sin(lim * dt)

    def per_step(x):
        return jnp.dot(x, etile_ref[...], precision=hp, preferred_element_type=F32)

    step_of_col = lax.broadcasted_iota(jnp.int32, (S5_P, S5_W), 1) // S5_GSIZE
    step_of_row = lax.broadcasted_iota(jnp.int32, (S5_W, S5_P), 0) // S5_GSIZE
    kcat, wtabs, arows = [], [], []
    for d in range(2):
        prow = prow_ref[d, 0]
        lre, lim = prow[0:1], prow[1:2]
        abr, abi = abar(lre, lim, prow[2:3])
        den = lre * lre + lim * lim
        fr = ((abr - 1.0) * lre + abi * lim) / den
        fi = (abi * lre - (abr - 1.0) * lim) / den
        btr, bti = bt_ref[d, 0, 0], bt_ref[d, 0, 1]
        bbr, bbi = fr * btr - fi * bti, fr * bti + fi * btr
        pcol = pcol_ref[d, 0]
        cbr, cbi = abar(pcol[:, 0:1], pcol[:, 1:2], pcol[:, 2:3])
        ctr, cti = per_step(ct_ref[d, 0, 0]), per_step(ct_ref[d, 0, 1])

        def c_times_power(e):
            pr, pi = _cpow(cbr, cbi, e, 5)
            return ctr * pr - cti * pi, ctr * pi + cti * pr

        vr, vi = c_times_power(step_of_col + 1 if d == 0 else q - step_of_col)
        (vf_ref if d == 0 else vb_ref)[0] = jnp.concatenate([vr, -vi], 0).astype(BF16)
        lr, li = c_times_power(step_of_col if d == 0 else q - 1 - step_of_col)
        kcat.append(jnp.dot(bbr, lr, precision=hp, preferred_element_type=F32)
                    - jnp.dot(bbi, li, precision=hp, preferred_element_type=F32))
        pr, pi = _cpow(abr, abi, q - 1 - step_of_row if d == 0 else step_of_row, 4)
        tbr, tbi = jnp.concatenate([bbr] * q, 0), jnp.concatenate([bbi] * q, 0)
        wr, wi = pr * tbr - pi * tbi, pr * tbi + pi * tbr
        wtabs += [wr, wi, wi, wr]
        ar, ai = _cpow(abr, abi, jnp.full((1, S5_P), q, jnp.int32), 5)
        arows += [jnp.concatenate([ar, ar], 1), jnp.concatenate([-ai, ai], 1), jnp.concatenate([ai, -ai], 1)]
    lane = lax.broadcasted_iota(jnp.int32, (S5_GSIZE, S5_W), 1)
    rows = []
    for s in range(q):
        shr = S5_GSIZE * s
        shl = S5_GSIZE * (q - 1 - s)
        f = jnp.where(lane >= shr, pltpu.roll(kcat[0], shr, 1), 0.0) if shr else kcat[0]
        b = jnp.where(lane < S5_W - shl, pltpu.roll(kcat[1], S5_W - shl, 1), 0.0) if shl else kcat[1]
        rows.append(f + b)
    ri = lax.broadcasted_iota(jnp.int32, (S5_W, S5_W), 0)
    ci = lax.broadcasted_iota(jnp.int32, (S5_W, S5_W), 1)
    tt = jnp.concatenate(rows, 0) + jnp.where(ri == ci, dsk_ref[0], 0.0)
    r_ref[0] = jnp.concatenate([tt] + wtabs, 1).astype(BF16)
    a_ref[...] = jnp.concatenate(arows + [jnp.zeros((2, 2 * S5_P), F32)], 0)


def _s5_tables(lam_re, lam_im, log_dt, b_re, b_im, c_re, c_im, dskip):
    zeros = jnp.zeros((2, S5_G, 5, S5_P), F32)
    prow = jnp.concatenate([lam_re[:, :, None], lam_im[:, :, None],
                            jnp.broadcast_to(log_dt[:, :, None, None], (2, S5_G, 1, S5_P)), zeros], 2)
    pcol = jnp.swapaxes(prow, 2, 3)
    bt = jnp.stack([jnp.swapaxes(b_re, 2, 3), jnp.swapaxes(b_im, 2, 3)], 2)
    ct = jnp.stack([jnp.swapaxes(c_re, 2, 3), jnp.swapaxes(c_im, 2, 3)], 2)
    dsk = jnp.tile(dskip.reshape(S5_G, 1, S5_GSIZE), (1, 1, S5_CHUNK))
    etile = jnp.asarray(np.tile(np.eye(S5_GSIZE, dtype=np.float32), (1, S5_CHUNK)))
    gblk = lambda a: pl.BlockSpec((2, 1) + a.shape[2:], lambda g: (0, g) + (0,) * (a.ndim - 2))
    sw = S5_G * 2 * S5_P
    return pl.pallas_call(
        _s5tab_kernel, grid=(S5_G,),
        in_specs=[gblk(prow), gblk(pcol), gblk(bt), gblk(ct), pl.BlockSpec((1, 1, S5_W), lambda g: (g, 0, 0)),
                  pl.BlockSpec(etile.shape, lambda g: (0, 0))],
        out_specs=[pl.BlockSpec((1, S5_W, 768), lambda g: (g, 0, 0)), pl.BlockSpec((1, 2 * S5_P, S5_W), lambda g: (g, 0, 0)),
                   pl.BlockSpec((1, 2 * S5_P, S5_W), lambda g: (g, 0, 0)), pl.BlockSpec((8, 2 * S5_P), lambda g: (0, g))],
        out_shape=[jax.ShapeDtypeStruct((S5_G, S5_W, 768), BF16), jax.ShapeDtypeStruct((S5_G, 2 * S5_P, S5_W), BF16),
                   jax.ShapeDtypeStruct((S5_G, 2 * S5_P, S5_W), BF16), jax.ShapeDtypeStruct((8, sw), F32)],
        compiler_params=_cparams("arbitrary"), name="s5tab",
    )(prow, pcol, bt, ct, dsk, etile)


def _s5_rows_in(u, nb):
    g, n, w = u.shape
    return jnp.swapaxes(u.reshape(g, nb, n // nb, w), 1, 2).reshape(g, n, w)


def _s5_rows_out(y, nb):
    g, n, w = y.shape
    return jnp.swapaxes(y.reshape(g, n // nb, nb, w), 1, 2).reshape(g, n, w)


def _s5_state_in(state):
    b = state.shape[0]
    sw = jnp.transpose(state, (0, 1, 2, 4, 3))
    plain = sw.reshape(b, 2, S5_G * 2 * S5_P)
    swapped = sw[:, :, :, ::-1, :].reshape(b, 2, S5_G * 2 * S5_P)
    return plain[:, 0], swapped[:, 0], plain[:, 1], swapped[:, 1]


def _s5_state_out(stf, stb):
    b = stf.shape[0]
    st = jnp.stack([stf, stb], 1).reshape(b, 2, S5_G, 2, S5_P)
    return jnp.transpose(st, (0, 1, 2, 4, 3))


def _mla_kernel(*refs, seq, lctx):
    if lctx:
        (cq_ref, ckv_ref, krab_ref, cosq_ref, sinq_ref, cosk_ref, sink_ref, cckv_ref, ckr_ref,
         wqa_ref, wqb_ref, wk_ref, wv_ref, o_ref, k_scr, v_scr, s_scr) = refs
    else:
        (cq_ref, ckv_ref, krab_ref, cosq_ref, sinq_ref, cosk_ref, sink_ref,
         wqa_ref, wqb_ref, wk_ref, wv_ref, o_ref, k_scr, v_scr, s_scr) = refs
    lk = lctx + seq
    rb = min(seq, 512)

    @pl.when(pl.program_id(1) == 0)
    def _build_keys():
        if lctx:
            ckv = cckv_ref[0]
            kn = _bdot(ckv, wk_ref[...])
            vn = _bdot(ckv, wv_ref[...])
            for h in range(MLA_H):
                k_scr[h, 0:lctx, :] = (kn[:, h * LANE:(h + 1) * LANE] + ckr_ref[0]).astype(BF16)
                v_scr[h, 0:lctx, :] = vn[:, h * LANE:(h + 1) * LANE].astype(BF16)

        def chunk(i, _):
            r0 = pl.multiple_of(i * rb, rb)
            rows = pl.ds(r0, rb)
            ckv = ckv_ref[rows, :]
            kn = _bdot(ckv, wk_ref[...])
            vn = _bdot(ckv, wv_ref[...])
            krab = krab_ref[rows, :]
            krx = krab[:, :LANE] * cosk_ref[rows, :] + krab[:, LANE:] * sink_ref[rows, :]
            dst = pl.ds(pl.multiple_of(lctx + r0, rb if lctx % rb == 0 else 256), rb)
            for h in range(MLA_H):
                k_scr[h, dst, :] = (kn[:, h * LANE:(h + 1) * LANE] + krx).astype(BF16)
                v_scr[h, dst, :] = vn[:, h * LANE:(h + 1) * LANE].astype(BF16)
            return 0

        lax.fori_loop(0, seq // rb, chunk, 0)

    cq = cq_ref[...]
    qa = _bdot(cq, wqa_ref[...])
    qb = _bdot(cq, wqb_ref[...])
    cosq, sinq = cosq_ref[...], sinq_ref[...]
    scale = 1.0 / math.sqrt(MLA_NOPE + MLA_ROPE)
    outs = []
    chunks = [slice(j * MLA_KC, (j + 1) * MLA_KC) for j in range(lk // MLA_KC)]
    for h in range(MLA_H):
        hs = slice(h * LANE, (h + 1) * LANE)
        qh = ((qa[:, hs] * cosq + qb[:, hs] * sinq) * scale).astype(BF16)
        mp = None
        for ks in chunks:
            s = _bdot_nt(qh, k_scr[h, ks, :])
            s_scr[:, ks] = s
            for t in range(MLA_KC // LANE):
                part = s[:, t * LANE:(t + 1) * LANE]
                mp = part if mp is None else jnp.maximum(mp, part)
        m = jnp.max(mp, axis=-1, keepdims=True)
        lp = jnp.zeros((MLA_TQ, LANE), F32)
        acc = jnp.zeros((MLA_TQ, LANE), F32)
        for ks in chunks:
            e = jnp.exp(s_scr[:, ks] - m)
            for t in range(MLA_KC // LANE):
                lp = lp + e[:, t * LANE:(t + 1) * LANE]
            acc = acc + _bdot(e, v_scr[h, ks, :])
        outs.append((acc / jnp.sum(lp, axis=-1, keepdims=True))[:, :MLA_V])
    o_ref[...] = jnp.concatenate(outs, axis=-1)


def _mla_call(cq, ckv, krab, rope, ctx, w, nb, seq, row0):
    lctx = 0 if ctx is None else ctx[0].shape[1]
    nq = seq // MLA_TQ
    q0, s0 = row0 // MLA_TQ, row0 // seq
    cosx, sinx = rope
    qrow = lambda n: pl.BlockSpec((MLA_TQ, n), lambda b, i: (q0 + b * nq + i, 0))
    srow = lambda n: pl.BlockSpec((seq, n), lambda b, i: (s0 + b, 0))
    full = lambda a: pl.BlockSpec(a.shape, lambda b, i: (0,) * a.ndim)
    in_specs = [qrow(256), srow(128), srow(256), pl.BlockSpec((MLA_TQ, LANE), lambda b, i: (i, 0)),
                pl.BlockSpec((MLA_TQ, LANE), lambda b, i: (i, 0)), full(cosx), full(sinx)]
    args = [cq, ckv, krab, cosx, sinx, cosx, sinx]
    if lctx:
        in_specs += [pl.BlockSpec((1, lctx, LANE), lambda b, i: (b, 0, 0))] * 2
        args += list(ctx)
    in_specs += [full(a) for a in w]
    args += list(w)
    lk = lctx + seq
    return pl.pallas_call(
        functools.partial(_mla_kernel, seq=seq, lctx=lctx), grid=(nb, nq),
        in_specs=in_specs,
        out_specs=pl.BlockSpec((MLA_TQ, 256), lambda b, i: (b * nq + i, 0)),
        out_shape=jax.ShapeDtypeStruct((nb * seq, 256), F32),
        scratch_shapes=[pltpu.VMEM((MLA_H, lk, LANE), BF16), pltpu.VMEM((MLA_H, lk, LANE), BF16),
                        pltpu.VMEM((MLA_TQ, lk), F32)],
        compiler_params=_cparams("arbitrary", "arbitrary"), name="mla",
    )(*args)


def _swap8(w):
    q = MLA_ROPE // 4
    return jnp.concatenate([w[..., j * q:(j + 1) * q] for j in (1, 0, 3, 2)], -1)


def _mla_weights(wuq, wuk, wuv):
    z32 = jnp.zeros((MLA_QLORA, 32), F32)
    z64 = jnp.zeros((MLA_QLORA, 64), F32)
    k64 = jnp.zeros((MLA_KVLORA, 64), F32)
    qa, qb, wk, wv = [], [], [], []
    for h in range(MLA_H):
        rope_w = wuq[:, h, MLA_NOPE:]
        qa += [wuq[:, h, :MLA_NOPE], rope_w, z32]
        qb += [z64, _swap8(rope_w), z32]
        wk += [wuk[:, h], k64]
        wv += [wuv[:, h], k64]
    cat = lambda xs: jnp.concatenate(xs, -1).astype(BF16)
    return cat(qa), cat(qb), cat(wk), cat(wv)


def _rope_tables(seq, rotate):
    ones = jnp.ones((seq, MLA_NOPE), F32)
    z32 = jnp.zeros((seq, 32), F32)
    if not rotate:
        return (jnp.concatenate([ones, jnp.ones((seq, MLA_ROPE), F32), z32], -1), jnp.zeros((seq, LANE), F32))
    pos = jnp.arange(seq)
    row = (pos // GRID_W).astype(F32)
    col = (pos % GRID_W).astype(F32)
    half = MLA_ROPE // 2
    inv = ROPE_BASE ** (-jnp.arange(0, half, 2, dtype=F32) / half)
    ang = jnp.concatenate([row[:, None] * inv, col[:, None] * inv], -1)
    cos, sin = jnp.cos(ang), jnp.sin(ang)
    q = MLA_ROPE // 4
    cos32 = jnp.concatenate([cos[:, :q], cos[:, :q], cos[:, q:], cos[:, q:]], -1)
    sin32 = jnp.concatenate([-sin[:, :q], sin[:, :q], -sin[:, q:], sin[:, q:]], -1)
    return jnp.concatenate([ones, cos32, z32], -1), jnp.concatenate([jnp.zeros_like(ones), sin32, z32], -1)


def _cumsum_rows(x, reverse):
    n = x.shape[0]
    row = lax.broadcasted_iota(jnp.int32, x.shape, 0)
    s = 1
    while s < n:
        if reverse:
            x = x + jnp.where(row < n - s, pltpu.roll(x, n - s, 0), 0.0)
        else:
            x = x + jnp.where(row >= s, pltpu.roll(x, s, 0), 0.0)
        s *= 2
    return x


def _ssd_kernel(xbc_ref, z_ref, dt_ref, cw_ref, cb_ref, par_ref, dsk_ref, nw_ref, h0_ref,
                y_ref, st_ref, xc_scr, xt_scr, y_scr, hf_scr, hb_scr, *, seq):
    q = SSD_CHUNK
    nc = seq // q
    row = lax.broadcasted_iota(jnp.int32, (q, 1), 0)

    def conv(c, _):
        t0 = pl.multiple_of(c * q, q)
        cur = xbc_ref[pl.ds(t0, q), :]
        prev8 = xbc_ref[pl.ds(pl.multiple_of(jnp.maximum(t0 - 8, 0), 8), 8), :]
        next8 = xbc_ref[pl.ds(pl.multiple_of(jnp.minimum(t0 + q, seq - 8), 8), 8), :]
        prow = jnp.where(c > 0, prev8[7:8], 0.0)
        nrow = jnp.where(c < nc - 1, next8[0:1], 0.0)
        up = jnp.where(row == 0, prow, pltpu.roll(cur, 1, 0))
        dn = jnp.where(row == q - 1, nrow, pltpu.roll(cur, q - 1, 0))
        acc = up * cw_ref[0:1] + cur * cw_ref[1:2] + dn * cw_ref[2:3] + cb_ref[...]
        xc = jax.nn.silu(acc)
        xc_scr[pl.ds(t0, q), :] = xc
        xt_scr[c] = jnp.transpose(xc[:, :SSD_W])
        y_scr[pl.ds(t0, q), :] = jnp.zeros((q, SSD_W), F32)
        return 0

    lax.fori_loop(0, nc, conv, 0)

    pair_rows = (SSD_H // SSD_G) * SSD_P
    for d, h_scr in enumerate((hf_scr, hb_scr)):
        h_scr[...] = jnp.zeros_like(h_scr)
        for g in range(SSD_G):
            h_scr[g * pair_rows:(g + 1) * pair_rows, g * SSD_N:(g + 1) * SSD_N] = (
                h0_ref[0, d, g * pair_rows:(g + 1) * pair_rows, :])
    ii = lax.broadcasted_iota(jnp.int32, (q, q), 0)
    jj = lax.broadcasted_iota(jnp.int32, (q, q), 1)
    first_half = lax.broadcasted_iota(jnp.int32, (q, LANE), 1) < SSD_N

    def one_chunk(c, d, h_scr):
        t0 = pl.multiple_of(c * q, q)
        xc = xc_scr[pl.ds(t0, q), :]
        xs_t = xt_scr[c]
        dt = jax.nn.softplus(dt_ref[pl.ds(t0, q), :] + par_ref[d:d + 1])
        cs = _cumsum_rows(dt * par_ref[2 + d:3 + d], reverse=(d == 1))
        edge = cs[q - 1:q] if d == 0 else cs[0:1]
        cs_t = jnp.transpose(cs)
        dt_t = jnp.transpose(dt)
        wts_t = jnp.transpose(jnp.exp(edge - cs) * dt)
        decay = jnp.exp(edge)
        mask = (ii >= jj) if d == 0 else (ii <= jj)
        bm_pair = xc[:, SSD_W:SSD_W + LANE]
        cm_pair = xc[:, SSD_W + LANE:SSD_W + 2 * LANE]
        for g in range(SSD_G):
            in_g = first_half if g == 0 else jnp.logical_not(first_half)
            pair = slice(g * LANE, (g + 1) * LANE)
            heads = [g * (SSD_H // SSD_G) + hh for hh in range(SSD_H // SSD_G)]
            cm = jnp.where(in_g, cm_pair, 0.0)
            bm = jnp.where(in_g, bm_pair, 0.0)
            cbm = _bdot_nt(cm, bm_pair)
            cs_i = [jnp.transpose(jnp.broadcast_to(cs_t[h:h + 1, :], (q, q))) for h in heads]
            y_off = _bdot_nt(cm, h_scr[pair, :]) * jnp.exp(jnp.where(first_half, cs_i[0], cs_i[1]))
            ys = []
            for hh, h in enumerate(heads):
                lmat = jnp.exp(jnp.where(mask, cs_i[hh] - cs_t[h:h + 1, :], -jnp.inf))
                ys.append(_bdot(cbm * lmat * dt_t[h:h + 1, :], xc[:, pair]))
                ps = slice(h * SSD_P, (h + 1) * SSD_P)
                st = _bdot(xs_t[ps, :] * wts_t[h:h + 1, :], bm)
                h_scr[ps, :] = decay[:, h:h + 1] * h_scr[ps, :] + st
            y_scr[pl.ds(t0, q), pair] += jnp.where(first_half, ys[0], ys[1]) + y_off

    def chunks(c, _):
        one_chunk(c, 0, hf_scr)
        one_chunk(nc - 1 - c, 1, hb_scr)
        return 0

    lax.fori_loop(0, nc, chunks, 0, unroll=2)
    for d, h_scr in enumerate((hf_scr, hb_scr)):
        for g in range(SSD_G):
            st_ref[0, d, g * pair_rows:(g + 1) * pair_rows, :] = (
                h_scr[g * pair_rows:(g + 1) * pair_rows, g * SSD_N:(g + 1) * SSD_N])

    def finish(c, _):
        rows = pl.ds(pl.multiple_of(c * q, q), q)
        y = y_scr[rows, :] + dsk_ref[...] * xc_scr[rows, 0:SSD_W]
        y_ref[rows, :] = _rms(y * jax.nn.silu(z_ref[rows, :]), nw_ref[...])
        return 0

    lax.fori_loop(0, nc, finish, 0)


def _ssd_call(xbc, z, dt, w, h0, nb, seq, row0):
    s0 = row0 // seq
    srow = lambda n: pl.BlockSpec((seq, n), lambda b: (s0 + b, 0))
    full = lambda a: pl.BlockSpec(a.shape, lambda b: (0,) * a.ndim)
    hp = SSD_H * SSD_P
    st_spec = pl.BlockSpec((1, 2, hp, SSD_N), lambda b: (b, 0, 0, 0))
    return pl.pallas_call(
        functools.partial(_ssd_kernel, seq=seq), grid=(nb,),
        in_specs=[srow(SSD_XBC), srow(SSD_W), srow(LANE)] + [full(a) for a in w] + [st_spec],
        out_specs=[pl.BlockSpec((seq, SSD_W), lambda b: (b, 0)), st_spec],
        out_shape=[jax.ShapeDtypeStruct((nb * seq, SSD_W), F32), jax.ShapeDtypeStruct((nb, 2, hp, SSD_N), F32)],
        scratch_shapes=[pltpu.VMEM((seq, SSD_XBC), F32), pltpu.VMEM((seq // SSD_CHUNK, SSD_W, SSD_CHUNK), F32),
                        pltpu.VMEM((seq, SSD_W), F32), pltpu.VMEM((hp, LANE), F32), pltpu.VMEM((hp, LANE), F32)],
        compiler_params=_cparams("arbitrary"), name="ssd",
    )(xbc, z, dt, *w, h0)


def _ssd_weights(conv_w, conv_b, dt_bias, a_log, dskip, norm_w):
    pad = lambda x: jnp.pad(x, ((0, 0), (0, LANE - SSD_H)))
    par = jnp.concatenate([pad(dt_bias), pad(-jnp.exp(a_log)), jnp.zeros((4, LANE), F32)], 0)
    cw = jnp.concatenate([conv_w, jnp.zeros((5, SSD_XBC), F32)], 0)
    return (cw, conv_b.reshape(1, SSD_XBC), par, jnp.repeat(dskip, SSD_P).reshape(1, SSD_W),
            norm_w.reshape(1, SSD_W))


def _conf_kernel(v_ref, w_ref, b_ref, g_ref, bt_ref, o_ref, pad_scr, sh_scr, *, seq, seg):
    q = CONF_Q
    halo = 2 * CONF_PAD
    tail = pad_scr.shape[0] - seq - CONF_PAD
    pad_scr[0:CONF_PAD, :] = jnp.zeros((CONF_PAD, CONF_W), F32)
    pad_scr[seq + CONF_PAD:, :] = jnp.zeros((tail, CONF_W), F32)
    pad_scr[CONF_PAD:seq + CONF_PAD, :] = v_ref[...]
    lo = CONF_PAD - CONF_K // 2
    pieces = [(j, min(LANE, seg + halo - j)) for j in range(0, seg + halo, LANE)]

    def segment(s, _):
        s0 = pl.multiple_of(s * seg, seg)
        for j0, n in pieces:
            big = pad_scr[pl.ds(s0 + j0, n + SUBLANE), :]
            for r in range(SUBLANE):
                sh_scr[r, j0:j0 + n, :] = big[r:r + n]

        def chunk(c, _):
            t0 = pl.multiple_of(c * q, q)
            acc = jnp.zeros((q, CONF_W), F32) + b_ref[...]
            for k in range(CONF_K):
                r, off = (lo + k) % SUBLANE, (lo + k) // SUBLANE * SUBLANE
                wk = jnp.concatenate([w_ref[k * SUBLANE:(k + 1) * SUBLANE, :]] * (q // SUBLANE), 0)
                acc = acc + sh_scr[r, pl.ds(pl.multiple_of(t0 + off, SUBLANE), q), :] * wk
            mu = jnp.mean(acc, axis=-1, keepdims=True)
            xc = acc - mu
            var = jnp.mean(xc * xc, axis=-1, keepdims=True)
            o_ref[pl.ds(s0 + t0, q), :] = jax.nn.silu(xc * lax.rsqrt(var + EPS) * g_ref[...] + bt_ref[...])
            return 0

        lax.fori_loop(0, seg // q, chunk, 0, unroll=2)
        return 0

    lax.fori_loop(0, seq // seg, segment, 0)


def _conf_call(v, w, nb, seq, row0):
    s0 = row0 // seq
    full = lambda a: pl.BlockSpec(a.shape, lambda b: (0,) * a.ndim)
    seg = min(seq, CONF_SEG)
    halo = 2 * CONF_PAD
    return pl.pallas_call(
        functools.partial(_conf_kernel, seq=seq, seg=seg), grid=(nb,),
        in_specs=[pl.BlockSpec((seq, CONF_W), lambda b: (s0 + b, 0))] + [full(a) for a in w],
        out_specs=pl.BlockSpec((seq, CONF_W), lambda b: (b, 0)),
        out_shape=jax.ShapeDtypeStruct((nb * seq, CONF_W), F32),
        scratch_shapes=[pltpu.VMEM((seq + halo + 2 * SUBLANE, CONF_W), F32),
                        pltpu.VMEM((SUBLANE, seg + halo, CONF_W), F32)],
        compiler_params=_cparams("arbitrary"), name="conf",
    )(v, *w)


def _split_bf16(x):
    hi = x.astype(BF16)
    return hi, (x - hi.astype(F32)).astype(BF16)


def _outproj_kernel(xp_ref, xs_ref, ada_ref, s5p_ref, s5s_ref, mlap_ref, mlas_ref, ssdp_ref, ssds_ref, confp_ref,
                    confs_ref, wglu_ref, bglu_ref, wo_ref, npost_ref, npre_ref, wrh_ref, wrl_ref, br_ref, perm_ref,
                    x1_ref, h2_ref, gate_ref, *, n_first):
    for s in range(TOKEN_SUB):
        _outproj_tile(s, xp_ref, xs_ref, ada_ref, s5p_ref, s5s_ref, mlap_ref, mlas_ref, ssdp_ref, ssds_ref, confp_ref,
                      confs_ref, wglu_ref, bglu_ref, wo_ref, npost_ref, npre_ref, wrh_ref, wrl_ref, br_ref, perm_ref,
                      x1_ref, h2_ref, gate_ref, n_first)


def _outproj_tile(s, xp_ref, xs_ref, ada_ref, s5p_ref, s5s_ref, mlap_ref, mlas_ref, ssdp_ref, ssds_ref, confp_ref,
                  confs_ref, wglu_ref, bglu_ref, wo_ref, npost_ref, npre_ref, wrh_ref, wrl_ref, br_ref, perm_ref,
                  x1_ref, h2_ref, gate_ref, n_first):
    rows = slice(s * TM, (s + 1) * TM)
    cpt = TM // S5_CHUNK
    is_first = pl.program_id(0) < n_first
    pick = lambda first, second: jnp.where(is_first, first[rows, :], second[rows, :])
    ada = ada_ref[s]
    ys5 = jnp.where(is_first, s5p_ref[:, s * cpt:(s + 1) * cpt, :], s5s_ref[:, s * cpt:(s + 1) * cpt, :])
    a = jax.nn.gelu(_group_rows_to_chunk_rows([ys5[g].astype(F32) for g in range(S5_G)], perm_ref[...]))
    s5 = a * jax.nn.sigmoid(_bdot(a, wglu_ref[...]) + bglu_ref[...])
    mix = (_bdot(s5, wo_ref[0:256, :]) + _bdot(pick(mlap_ref, mlas_ref), wo_ref[256:512, :])
           + _bdot(pick(ssdp_ref, ssds_ref), wo_ref[512:768, :]) + _bdot(pick(confp_ref, confs_ref), wo_ref[768:1024, :]))
    x1 = pick(xp_ref, xs_ref) + ada[2:3] * _rms(mix, npost_ref[...])
    x1_ref[rows, :] = x1
    h2 = _rms(x1, npre_ref[...]) * (1.0 + ada[4:5]) + ada[3:4]
    h2_ref[rows, :] = h2.astype(BF16)
    hi, lo = _split_bf16(h2)
    lg = (jnp.dot(hi, wrh_ref[...], preferred_element_type=F32) + jnp.dot(lo, wrh_ref[...], preferred_element_type=F32)
          + jnp.dot(hi, wrl_ref[...], preferred_element_type=F32) + br_ref[...])
    lane = lax.broadcasted_iota(jnp.int32, lg.shape, 1)
    neg = -jnp.inf
    big = jnp.int32(1 << 20)
    first = lambda hit: jnp.min(jnp.where(hit, lane, big), axis=-1, keepdims=True)
    glm = jnp.where((lane >= MOE_E) & (lane < MOE_E + MOE_GROUPS), lg, neg)
    gmax = jnp.max(glm, axis=-1, keepdims=True)
    p_group = 1.0 / jnp.sum(jnp.exp(glm - gmax), axis=-1, keepdims=True)
    gsel = first(glm == gmax) - MOE_E
    elm = jnp.where((lane < MOE_E) & ((lane // MOE_PER_GROUP) == gsel), lg, neg)
    v1 = jnp.max(elm, axis=-1, keepdims=True)
    i1 = first(elm == v1)
    elm2 = jnp.where(lane == i1, neg, elm)
    v2 = jnp.max(elm2, axis=-1, keepdims=True)
    i2 = first(elm2 == v2)
    e2 = jnp.exp(v2 - v1)
    w1 = p_group / (1.0 + e2)
    w2 = p_group * e2 / (1.0 + e2)
    gate_ref[rows, :] = jnp.where(lane == i1, w1, 0.0) + jnp.where(lane == i2, w2, 0.0)


def _outproj_call(x, ada_t, ys5, ymla, yssd, yconf, w):
    t = x[0].shape[0] + x[1].shape[0]
    tm = TOKEN_SUB * TM
    n_first = x[0].shape[0] // tm
    row = lambda n: pl.BlockSpec((tm, n), lambda i: (i, 0))
    full = lambda a: pl.BlockSpec(a.shape, lambda i: (0,) * a.ndim)
    w = tuple(w) + (_chunk_perm(),)
    pair = lambda n: _two_group_specs((tm, n), n_first)
    return pl.pallas_call(
        functools.partial(_outproj_kernel, n_first=n_first), grid=(t // tm,),
        in_specs=pair(D_MODEL) + [pl.BlockSpec((TOKEN_SUB, 8, D_MODEL), lambda i: (i, 0, 0))]
        + _two_group_specs((tm // S5_CHUNK, S5_W), n_first, lead=(S5_G,)) + pair(256) + pair(256) + pair(256)
        + [full(a) for a in w],
        out_specs=[row(D_MODEL), row(D_MODEL), row(LANE)],
        out_shape=[jax.ShapeDtypeStruct((t, D_MODEL), F32), jax.ShapeDtypeStruct((t, D_MODEL), BF16),
                   jax.ShapeDtypeStruct((t, LANE), F32)],
        compiler_params=_cparams("arbitrary"), name="outproj",
    )(*x, ada_t, *ys5, *ymla, *yssd, *yconf, *w)


def _moe_kernel(h_ref, gate_ref, x1_ref, ada_ref, wg_ref, wu_ref, wd_ref, npost_ref, op_ref, os_ref, hid_scr,
                *, n_first):
    i = pl.program_id(0)
    h = h_ref[...]
    gate = gate_ref[...]
    lane = lax.broadcasted_iota(jnp.int32, gate.shape, 1)
    for e in range(MOE_E):
        ge = jnp.sum(jnp.where(lane == e, gate, 0.0), axis=-1, keepdims=True)
        hid = (jax.nn.silu(jnp.dot(h, wg_ref[e], preferred_element_type=F32))
               * jnp.dot(h, wu_ref[e], preferred_element_type=F32) * ge)
        hid_scr[:, e * MOE_HID:(e + 1) * MOE_HID] = hid.astype(BF16)
    out = jnp.dot(hid_scr[...], wd_ref[...], preferred_element_type=F32)
    res = x1_ref[...] + ada_ref[0][5:6] * _rms(out, npost_ref[...])

    @pl.when(i < n_first)
    def _store_first():
        op_ref[...] = res

    @pl.when(i >= n_first)
    def _store_second():
        os_ref[...] = res


def _moe_call(h2, gate, x1, ada_t, wg, wu, wd, npost, t_first):
    t = h2.shape[0]
    tm = MOE_TM
    n_first = t_first // tm
    row = lambda n: pl.BlockSpec((tm, n), lambda i: (i, 0))
    resident = lambda a: pl.BlockSpec(a.shape, lambda i: (0,) * a.ndim, pipeline_mode=pl.Buffered(1))
    wd = wd.reshape(MOE_E * MOE_HID, D_MODEL)
    return pl.pallas_call(
        functools.partial(_moe_kernel, n_first=n_first), grid=(t // tm,),
        in_specs=[row(D_MODEL), row(LANE), row(D_MODEL),
                  pl.BlockSpec((1, 8, D_MODEL), lambda i: (i * (tm // TM), 0, 0)),
                  resident(wg), resident(wu), resident(wd), pl.BlockSpec(npost.shape, lambda i: (0, 0))],
        out_specs=_two_group_specs((tm, D_MODEL), n_first),
        out_shape=[jax.ShapeDtypeStruct((t_first, D_MODEL), F32), jax.ShapeDtypeStruct((t - t_first, D_MODEL), F32)],
        scratch_shapes=[pltpu.VMEM((tm, MOE_E * MOE_HID), BF16)],
        compiler_params=_cparams("arbitrary"), name="moe",
    )(h2, gate, x1, ada_t, wg, wu, wd, npost)


def _pack_w_in(w):
    d = w.shape[0]
    z32, z64 = jnp.zeros((d, 32), F32), jnp.zeros((d, 64), F32)
    kr = w[:, OFF_MLA_KR:OFF_SSD_Z]
    dtw = jnp.pad(w[:, OFF_SSD_DT:OFF_CONF], ((0, 0), (0, LANE - SSD_H)))
    cols = [w[:, OFF_S5:OFF_MLA_Q], w[:, OFF_MLA_Q:OFF_MLA_KV], w[:, OFF_MLA_KV:OFF_MLA_KR],
            z64, kr, z32, z64, _swap8(kr), z32,
            w[:, OFF_SSD_Z:OFF_SSD_XBC], w[:, OFF_SSD_XBC:OFF_SSD_DT], dtw, w[:, OFF_CONF:IN_COLS]]
    return jnp.concatenate(cols, -1).astype(BF16)


def _row(v):
    return v.reshape(1, -1)


def kernel(x_prompt, x_sample, cache_mla_ckv, cache_mla_krope, state_s5, state_ssd, c, c_ctx, w_ada, b_ada, norm_pre1, norm_post1, norm_pre2, norm_post2, w_in, w_out, s5_lam_re, s5_lam_im, s5_log_dt, s5_b_re, s5_b_im, s5_c_re, s5_c_im, s5_d, s5_w_glu, s5_b_glu, mla_qnorm, mla_kvnorm, mla_wuq, mla_wuk, mla_wuv, ssd_conv_w, ssd_conv_b, ssd_dt_bias, ssd_a_log, ssd_d, ssd_norm_w, conf_dw_w, conf_dw_b, conf_ln_g, conf_ln_b, moe_wg, moe_bg, moe_we, moe_be, moe_w_gate, moe_w_up, moe_w_down):
    bp, lp, d = x_prompt.shape
    bs, ls, _ = x_sample.shape
    tp, ts = bp * lp, bs * ls
    x = (x_prompt.reshape(tp, d), x_sample.reshape(ts, d))

    crows = jnp.concatenate([c_ctx[None], c, jnp.zeros((8 - 1 - bs, d), F32)], 0)
    ada = _ada_call(crows, w_ada, b_ada)
    tile_row = np.concatenate([np.zeros(tp // TM, np.int32), 1 + np.repeat(np.arange(bs, dtype=np.int32), ls // TM)])

    rope_p = _rope_tables(lp, rotate=False)
    rope_s = _rope_tables(ls, rotate=True)
    zeros_s5 = tuple(jnp.zeros((bp, S5_G * 2 * S5_P), F32) for _ in range(4))
    zeros_ssd = jnp.zeros((bp, 2, SSD_H * SSD_P, SSD_N), F32)
    pad_kr = lambda kr: jnp.pad(kr, ((0, 0), (0, 0), (MLA_NOPE, LANE - MLA_NOPE - MLA_ROPE)))
    moe_w = [w.astype(BF16) for w in (moe_w_gate, moe_w_up, moe_w_down)]

    ckv_l, kr_l, s5_l, ssd_l = [], [], [], []
    for l in range(DEPTH):
        ada_t = jnp.pad(ada[l][tile_row].reshape(-1, 6, d), ((0, 0), (0, 2), (0, 0)))
        u, cq, ckv, krab, z, xbc, dt, v = _inproj_call(
            *x, ada_t, _row(norm_pre1[l]), _pack_w_in(w_in[l]), _row(mla_qnorm[l]), _row(mla_kvnorm[l]))

        tabs = _s5_tables(s5_lam_re[l], s5_lam_im[l], s5_log_dt[l], s5_b_re[l], s5_b_im[l], s5_c_re[l], s5_c_im[l],
                          s5_d[l])
        ncp = tp // S5_CHUNK
        yp, stf, stb = _s5_call(_s5_rows_in(u[:, :ncp], bp), tabs, zeros_s5, bp)
        ys, _, _ = _s5_call(_s5_rows_in(u[:, ncp:], bs), tabs, _s5_state_in(state_s5[:, l]), bs)
        y_s5 = (_s5_rows_out(yp, bp), _s5_rows_out(ys, bs))
        s5_l.append(_s5_state_out(stf, stb))

        mw = _mla_weights(mla_wuq[l], mla_wuk[l], mla_wuv[l])
        y_mla = (_mla_call(cq, ckv, krab, rope_p, None, mw, bp, lp, 0),
                 _mla_call(cq, ckv, krab, rope_s, (cache_mla_ckv[:, l], pad_kr(cache_mla_krope[:, l])), mw, bs, ls, tp))
        ckv_l.append(ckv[:tp].reshape(bp, lp, MLA_KVLORA))
        kr_l.append(krab[:tp, MLA_NOPE:MLA_NOPE + MLA_ROPE].reshape(bp, lp, MLA_ROPE))

        sw = _ssd_weights(ssd_conv_w[l], ssd_conv_b[l], ssd_dt_bias[l], ssd_a_log[l], ssd_d[l], ssd_norm_w[l])
        yssd_p, st_ssd = _ssd_call(xbc, z, dt, sw, zeros_ssd, bp, lp, 0)
        yssd_s, _ = _ssd_call(xbc, z, dt, sw, state_ssd[:, l].reshape(bs, 2, SSD_H * SSD_P, SSD_N), bs, ls, tp)
        ssd_l.append(st_ssd.reshape(bp, 2, SSD_H, SSD_P, SSD_N))

        cw = (jnp.repeat(conf_dw_w[l], SUBLANE, axis=0), _row(conf_dw_b[l]),
              _row(conf_ln_g[l]), _row(conf_ln_b[l]))
        y_conf = (_conf_call(v, cw, bp, lp, 0), _conf_call(v, cw, bs, ls, tp))

        wr = jnp.pad(jnp.concatenate([moe_we[l], moe_wg[l]], -1), ((0, 0), (0, LANE - MOE_E - MOE_GROUPS)))
        wrh, wrl = _split_bf16(wr)
        br = jnp.pad(jnp.concatenate([moe_be[l], moe_bg[l]]), (0, LANE - MOE_E - MOE_GROUPS)).reshape(1, LANE)
        ow = (s5_w_glu[l].astype(BF16), _row(s5_b_glu[l]), w_out[l].astype(BF16), _row(norm_post1[l]),
              _row(norm_pre2[l]), wrh, wrl, br)
        x1, h2, gate = _outproj_call(x, ada_t, y_s5, y_mla, (yssd_p, yssd_s), y_conf, ow)

        x = _moe_call(h2, gate, x1, ada_t, moe_w[0][l], moe_w[1][l], moe_w[2][l], _row(norm_post2[l]), tp)

    return (x[0].reshape(bp, lp, d), x[1].reshape(bs, ls, d),
            jnp.stack(ckv_l, 1), jnp.stack(kr_l, 1), jnp.stack(s5_l, 1), jnp.stack(ssd_l, 1))
```

```python
import functools
import math

import jax
import jax.numpy as jnp
import numpy as np
from jax import lax
from jax.experimental import pallas as pl
from jax.experimental.pallas import tpu as pltpu

F32 = jnp.float32
BF16 = jnp.bfloat16

D_MODEL = 1024
DEPTH = 2
GRID_W = 64
EPS = 1e-6

S5_W = 256
S5_GSIZE = 16
S5_G = 16
S5_P = 64
S5_CHUNK = 16

MLA_H = 4
MLA_NOPE = 64
MLA_ROPE = 32
MLA_V = 64
MLA_QLORA = 256
MLA_KVLORA = 128
ROPE_BASE = 10000.0
MLA_TQ = 256
MLA_KC = 256

SSD_W = 256
SSD_P = 64
SSD_H = 4
SSD_G = 2
SSD_N = 64
SSD_CHUNK = 128
SSD_XBC = 512

CONF_W = 256
CONF_K = 31
CONF_Q = 128
CONF_PAD = 16
CONF_SEG = 1024

MOE_GROUPS = 4
MOE_PER_GROUP = 4
MOE_E = 16
MOE_HID = 256
MOE_TM = 512

OFF_S5 = 0
OFF_MLA_Q = 256
OFF_MLA_KV = 512
OFF_MLA_KR = 640
OFF_SSD_Z = 672
OFF_SSD_XBC = 928
OFF_SSD_DT = 1440
OFF_CONF = 1444
IN_COLS = 1956

LANE = 128
SUBLANE = 8
TM = 256
TOKEN_SUB = 4
P_U, P_Q, P_KV, P_KRA, P_KRB, P_Z, P_XBC, P_DT, P_CA, P_CB, P_END = (
    0, 256, 512, 640, 768, 896, 1152, 1664, 1792, 2048, 2304)

VMEM_LIMIT = 56 * 1024 * 1024


def _bdot(a, b):
    return jnp.dot(a.astype(BF16), b.astype(BF16), preferred_element_type=F32)


def _bdot_nt(a, b):
    return lax.dot_general(a.astype(BF16), b.astype(BF16), (((1,), (1,)), ((), ())), preferred_element_type=F32)


def _rms(x, g):
    return x * lax.rsqrt(jnp.mean(x * x, axis=-1, keepdims=True) + EPS) * g


def _cparams(*sem):
    return pltpu.CompilerParams(dimension_semantics=sem if sem else None, vmem_limit_bytes=VMEM_LIMIT)


def _ada_kernel(c_ref, w_ref, b_ref, o_ref):
    o_ref[0] = _bdot(jax.nn.silu(c_ref[...]), w_ref[0]) + b_ref[0]


def _ada_call(crows, w_ada, b_ada):
    tn = 1024
    d6 = 6 * D_MODEL
    return pl.pallas_call(
        _ada_kernel, grid=(DEPTH, d6 // tn),
        in_specs=[pl.BlockSpec((8, D_MODEL), lambda l, j: (0, 0)),
                  pl.BlockSpec((1, D_MODEL, tn), lambda l, j: (l, 0, j)),
                  pl.BlockSpec((1, 1, tn), lambda l, j: (l, 0, j))],
        out_specs=pl.BlockSpec((1, 8, tn), lambda l, j: (l, 0, j)),
        out_shape=jax.ShapeDtypeStruct((DEPTH, 8, d6), F32),
        compiler_params=_cparams("arbitrary", "arbitrary"), name="ada",
    )(crows, w_ada, b_ada.reshape(DEPTH, 1, d6))


def _lane_block(shape):
    return lax.broadcasted_iota(jnp.int32, shape, 1) // S5_GSIZE


def _chunk_rows_to_group_rows(x, perm):
    r = jnp.dot(perm, x.astype(BF16), preferred_element_type=F32)
    blk = _lane_block((S5_CHUNK, S5_W))
    out = []
    for g in range(S5_G):
        acc = None
        for t in range(S5_CHUNK):
            src = r[t * S5_CHUNK:(t + 1) * S5_CHUNK, :]
            shift = ((t - g) * S5_GSIZE) % S5_W
            src = pltpu.roll(src, shift, 1) if shift else src
            acc = src if acc is None else jnp.where(blk == t, src, acc)
        out.append(acc)
    return out


def _group_rows_to_chunk_rows(ys, perm_t):
    blk = _lane_block((S5_CHUNK, S5_W))
    rows = []
    for t in range(S5_CHUNK):
        acc = None
        for g in range(S5_G):
            shift = ((g - t) * S5_GSIZE) % S5_W
            src = pltpu.roll(ys[g], shift, 1) if shift else ys[g]
            acc = src if acc is None else jnp.where(blk == g, src, acc)
        rows.append(acc)
    z = jnp.concatenate(rows, 0)
    return jnp.dot(perm_t, z.astype(BF16), preferred_element_type=F32)


def _two_group_specs(block, n_first, lead=()):
    nl = len(lead)
    first = pl.BlockSpec(lead + block, lambda i, *_: (0,) * nl + (jnp.minimum(i, n_first - 1), 0))
    second = pl.BlockSpec(lead + block, lambda i, *_: (0,) * nl + (jnp.maximum(i - n_first, 0), 0))
    return [first, second]


def _inproj_kernel(xp_ref, xs_ref, ada_ref, npre_ref, w_ref, qn_ref, kvn_ref, perm_ref,
                   u_ref, cq_ref, ckv_ref, krab_ref, z_ref, xbc_ref, dt_ref, v_ref, *, n_first):
    cpt = TM // S5_CHUNK
    for s in range(TOKEN_SUB):
        rows = slice(s * TM, (s + 1) * TM)
        ada = ada_ref[s]
        x = jnp.where(pl.program_id(0) < n_first, xp_ref[rows, :], xs_ref[rows, :])
        h = _rms(x, npre_ref[...]) * (1.0 + ada[1:2]) + ada[0:1]
        p = _bdot(h, w_ref[...])
        for g, ug in enumerate(_chunk_rows_to_group_rows(p[:, P_U:P_Q], perm_ref[...])):
            u_ref[g, s * cpt:(s + 1) * cpt, :] = ug.astype(BF16)
        cq_ref[rows, :] = _rms(p[:, P_Q:P_KV], qn_ref[...]).astype(BF16)
        ckv_ref[rows, :] = _rms(p[:, P_KV:P_KRA], kvn_ref[...])
        krab_ref[rows, :] = p[:, P_KRA:P_Z]
        z_ref[rows, :] = p[:, P_Z:P_XBC]
        xbc_ref[rows, :] = p[:, P_XBC:P_DT]
        dt_ref[rows, :] = p[:, P_DT:P_CA]
        v_ref[rows, :] = p[:, P_CA:P_CB] * jax.nn.sigmoid(p[:, P_CB:P_END])


def _chunk_perm():
    idx = np.arange(TM)
    src = (idx % S5_CHUNK) * S5_CHUNK + idx // S5_CHUNK
    return jnp.asarray(np.eye(TM, dtype=np.float32)[src], BF16)


def _inproj_call(xp, xs, ada_t, npre, w, layer, qn, kvn):
    t = xp.shape[0] + xs.shape[0]
    tm = TOKEN_SUB * TM
    n_first = xp.shape[0] // tm
    row = lambda n: pl.BlockSpec((tm, n), lambda i: (i, 0))
    full = lambda a: pl.BlockSpec(a.shape, lambda i: (0,) * a.ndim)
    widths = (256, 128, 256, 256, 512, 128, 256)
    dts = (BF16, F32, F32, F32, F32, F32, F32)
    perm = _chunk_perm()
    cpt = tm // S5_CHUNK
    return pl.pallas_call(
        functools.partial(_inproj_kernel, n_first=n_first), grid=(t // tm,),
        in_specs=_two_group_specs((tm, D_MODEL), n_first)
        + [pl.BlockSpec((TOKEN_SUB, 8, D_MODEL), lambda i: (i, 0, 0)), full(npre),
           pl.BlockSpec((None,) + w.shape[1:], lambda i: (layer, 0, 0)), full(qn), full(kvn), full(perm)],
        out_specs=[pl.BlockSpec((S5_G, cpt, S5_W), lambda i: (0, i, 0))] + [row(n) for n in widths],
        out_shape=[jax.ShapeDtypeStruct((S5_G, t // S5_CHUNK, S5_W), BF16)]
        + [jax.ShapeDtypeStruct((t, n), d) for n, d in zip(widths, dts)],
        compiler_params=_cparams("arbitrary"), name="inproj",
    )(xp, xs, ada_t, npre, w, qn, kvn, perm)


def _s5_kernel(u_ref, r_ref, vf_ref, vb_ref, a_ref, h0f_ref, h0fs_ref, h0b_ref, h0bs_ref, perm_ref, permt_ref,
               y_ref, stf_ref, stb_ref, ucb, sf, sfs, sb, sbs, *, nb, nc):
    gw = 2 * S5_P
    for g in range(S5_G):
        ucb[g] = jnp.dot(perm_ref[...], u_ref[g], preferred_element_type=F32).astype(BF16)
        r = jnp.dot(ucb[g], r_ref[g, :, S5_W:], preferred_element_type=F32)
        sl = slice(g * gw, (g + 1) * gw)
        sf[:, sl] = r[:, 0:128]
        sfs[:, sl] = r[:, 128:256]
        sb[:, sl] = r[:, 256:384]
        sbs[:, sl] = r[:, 384:512]
    arf, aif, aisf = a_ref[0:1], a_ref[1:2], a_ref[2:3]
    arb, aib, aisb = a_ref[3:4], a_ref[4:5], a_ref[5:6]
    cpt = 1 if nb % SUBLANE == 0 else SUBLANE // nb
    rows = cpt * nb
    nt = nc // cpt

    def body(i, carry):
        hf, hfs, hb, hbs = carry
        rf = pl.ds(pl.multiple_of(i * rows, rows), rows)
        rb = pl.ds(pl.multiple_of((nt - 1 - i) * rows, rows), rows)
        inc_f, inc_fs, inc_b, inc_bs = sf[rf, :], sfs[rf, :], sb[rb, :], sbs[rb, :]
        in_f, in_b = [None] * cpt, [None] * cpt
        for k in range(cpt):
            kf = slice(k * nb, (k + 1) * nb)
            kb = slice((cpt - 1 - k) * nb, (cpt - k) * nb)
            in_f[k] = hf
            in_b[cpt - 1 - k] = hb
            hf, hfs = arf * hf + aif * hfs + inc_f[kf], arf * hfs + aisf * hf + inc_fs[kf]
            hb, hbs = arb * hb + aib * hbs + inc_b[kb], arb * hbs + aisb * hb + inc_bs[kb]
        sf[rf, :] = in_f[0] if cpt == 1 else jnp.concatenate(in_f, 0)
        sb[rb, :] = in_b[0] if cpt == 1 else jnp.concatenate(in_b, 0)
        return hf, hfs, hb, hbs

    hf, _, hb, _ = lax.fori_loop(0, nt, body, (h0f_ref[...], h0fs_ref[...], h0b_ref[...], h0bs_ref[...]))
    stf_ref[...] = hf
    stb_ref[...] = hb
    for g in range(S5_G):
        sl = slice(g * gw, (g + 1) * gw)
        y = (jnp.dot(ucb[g], r_ref[g, :, :S5_W], preferred_element_type=F32)
             + _bdot(sf[:, sl], vf_ref[g]) + _bdot(sb[:, sl], vb_ref[g]))
        y_ref[g] = jnp.dot(permt_ref[...], y.astype(BF16), preferred_element_type=F32).astype(BF16)


def _s5_call(u, tabs, h0, nb, n, block):
    r, vf, vb, avec = tabs
    sw = S5_G * 2 * S5_P
    nc = n // nb
    dst = np.arange(n)
    src = (dst % nb) * nc + dst // nb
    perm = np.eye(n, dtype=np.float32)[src]
    consts = (jnp.asarray(perm, BF16), jnp.asarray(perm.T, BF16))
    full = lambda a: pl.BlockSpec(a.shape, lambda i: (0,) * len(a.shape))
    st = jax.ShapeDtypeStruct((nb, sw), F32)
    args = (r, vf, vb, avec, *h0, *consts)
    return pl.pallas_call(
        functools.partial(_s5_kernel, nb=nb, nc=nc), grid=(1,),
        in_specs=[pl.BlockSpec((S5_G, n, S5_W), lambda i: (0, block, 0))] + [full(a) for a in args],
        out_specs=[full(jax.ShapeDtypeStruct((S5_G, n, S5_W), BF16)), full(st), full(st)],
        out_shape=[jax.ShapeDtypeStruct((S5_G, n, S5_W), BF16), st, st],
        scratch_shapes=[pltpu.VMEM((S5_G, n, S5_W), BF16)] + [pltpu.VMEM((n, sw), F32) for _ in range(4)],
        compiler_params=_cparams("arbitrary"), name="s5",
    )(u, *args)


def _cpow(br, bi, e, nbits):
    pr = jnp.ones(e.shape, F32)
    pi = jnp.zeros(e.shape, F32)
    for k in range(nbits):
        bit = ((e >> k) & 1) == 1
        pr, pi = jnp.where(bit, pr * br - pi * bi, pr), jnp.where(bit, pr * bi + pi * br, pi)
        br, bi = br * br - bi * bi, 2.0 * br * bi
    return pr, pi


def _s5tab_kernel(prow_ref, pcol_ref, bt_ref, ct_ref, dsk_ref, etile_ref, r_ref, vf_ref, vb_ref, a_ref):
    q = S5_CHUNK
    hp = lax.Precision.HIGHEST

    def abar(lre, lim, ldt):
        dt = jnp.exp(ldt)
        mag = jnp.exp(lre * dt)
        return mag * jnp.cos(lim * dt), mag * jnp.sin(lim * dt)

    def per_step(x):
        return jnp.dot(x, etile_ref[...], precision=hp, preferred_element_type=F32)

    step_of_col = lax.broadcasted_iota(jnp.int32, (S5_P, S5_W), 1) // S5_GSIZE
    step_of_row = lax.broadcasted_iota(jnp.int32, (S5_W, S5_P), 0) // S5_GSIZE
    kcat, wtabs, arows = [], [], []
    for d in range(2):
        prow = prow_ref[d, 0]
        lre, lim = prow[0:1], prow[1:2]
        abr, abi = abar(lre, lim, prow[2:3])
        den = lre * lre + lim * lim
        fr = ((abr - 1.0) * lre + abi * lim) / den
        fi = (abi * lre - (abr - 1.0) * lim) / den
        btr, bti = bt_ref[d, 0, 0], bt_ref[d, 0, 1]
        bbr, bbi = fr * btr - fi * bti, fr * bti + fi * btr
        pcol = pcol_ref[d, 0]
        cbr, cbi = abar(pcol[:, 0:1], pcol[:, 1:2], pcol[:, 2:3])
        ctr, cti = per_step(ct_ref[d, 0, 0]), per_step(ct_ref[d, 0, 1])

        def c_times_power(e):
            pr, pi = _cpow(cbr, cbi, e, 5)
            return ctr * pr - cti * pi, ctr * pi + cti * pr

        vr, vi = c_times_power(step_of_col + 1 if d == 0 else q - step_of_col)
        (vf_ref if d == 0 else vb_ref)[0] = jnp.concatenate([vr, -vi], 0).astype(BF16)
        lr, li = c_times_power(step_of_col if d == 0 else q - 1 - step_of_col)
        kcat.append(jnp.dot(bbr, lr, precision=hp, preferred_element_type=F32)
                    - jnp.dot(bbi, li, precision=hp, preferred_element_type=F32))
        pr, pi = _cpow(abr, abi, q - 1 - step_of_row if d == 0 else step_of_row, 4)
        tbr, tbi = jnp.concatenate([bbr] * q, 0), jnp.concatenate([bbi] * q, 0)
        wr, wi = pr * tbr - pi * tbi, pr * tbi + pi * tbr
        wtabs += [wr, wi, wi, wr]
        ar, ai = _cpow(abr, abi, jnp.full((1, S5_P), q, jnp.int32), 5)
        arows += [jnp.concatenate([ar, ar], 1), jnp.concatenate([-ai, ai], 1), jnp.concatenate([ai, -ai], 1)]
    lane = lax.broadcasted_iota(jnp.int32, (S5_GSIZE, S5_W), 1)
    rows = []
    for s in range(q):
        shr = S5_GSIZE * s
        shl = S5_GSIZE * (q - 1 - s)
        f = jnp.where(lane >= shr, pltpu.roll(kcat[0], shr, 1), 0.0) if shr else kcat[0]
        b = jnp.where(lane < S5_W - shl, pltpu.roll(kcat[1], S5_W - shl, 1), 0.0) if shl else kcat[1]
        rows.append(f + b)
    ri = lax.broadcasted_iota(jnp.int32, (S5_W, S5_W), 0)
    ci = lax.broadcasted_iota(jnp.int32, (S5_W, S5_W), 1)
    tt = jnp.concatenate(rows, 0) + jnp.where(ri == ci, dsk_ref[0], 0.0)
    r_ref[0] = jnp.concatenate([tt] + wtabs, 1).astype(BF16)
    a_ref[...] = jnp.concatenate(arows + [jnp.zeros((2, 2 * S5_P), F32)], 0)


def _s5_tables(lam_re, lam_im, log_dt, b_re, b_im, c_re, c_im, dskip):
    zeros = jnp.zeros((2, S5_G, 5, S5_P), F32)
    prow = jnp.concatenate([lam_re[:, :, None], lam_im[:, :, None],
                            jnp.broadcast_to(log_dt[:, :, None, None], (2, S5_G, 1, S5_P)), zeros], 2)
    pcol = jnp.swapaxes(prow, 2, 3)
    bt = jnp.stack([jnp.swapaxes(b_re, 2, 3), jnp.swapaxes(b_im, 2, 3)], 2)
    ct = jnp.stack([jnp.swapaxes(c_re, 2, 3), jnp.swapaxes(c_im, 2, 3)], 2)
    dsk = jnp.tile(dskip.reshape(S5_G, 1, S5_GSIZE), (1, 1, S5_CHUNK))
    etile = jnp.asarray(np.tile(np.eye(S5_GSIZE, dtype=np.float32), (1, S5_CHUNK)))
    gblk = lambda a: pl.BlockSpec((2, 1) + a.shape[2:], lambda g: (0, g) + (0,) * (a.ndim - 2))
    sw = S5_G * 2 * S5_P
    return pl.pallas_call(
        _s5tab_kernel, grid=(S5_G,),
        in_specs=[gblk(prow), gblk(pcol), gblk(bt), gblk(ct), pl.BlockSpec((1, 1, S5_W), lambda g: (g, 0, 0)),
                  pl.BlockSpec(etile.shape, lambda g: (0, 0))],
        out_specs=[pl.BlockSpec((1, S5_W, 768), lambda g: (g, 0, 0)), pl.BlockSpec((1, 2 * S5_P, S5_W), lambda g: (g, 0, 0)),
                   pl.BlockSpec((1, 2 * S5_P, S5_W), lambda g: (g, 0, 0)), pl.BlockSpec((8, 2 * S5_P), lambda g: (0, g))],
        out_shape=[jax.ShapeDtypeStruct((S5_G, S5_W, 768), BF16), jax.ShapeDtypeStruct((S5_G, 2 * S5_P, S5_W), BF16),
                   jax.ShapeDtypeStruct((S5_G, 2 * S5_P, S5_W), BF16), jax.ShapeDtypeStruct((8, sw), F32)],
        compiler_params=_cparams("arbitrary"), name="s5tab",
    )(prow, pcol, bt, ct, dsk, etile)


def _s5_state_in(state):
    b = state.shape[0]
    sw = jnp.transpose(state, (0, 1, 2, 4, 3))
    plain = sw.reshape(b, 2, S5_G * 2 * S5_P)
    swapped = sw[:, :, :, ::-1, :].reshape(b, 2, S5_G * 2 * S5_P)
    return plain[:, 0], swapped[:, 0], plain[:, 1], swapped[:, 1]


def _s5_state_out(stf, stb):
    b = stf.shape[0]
    st = jnp.stack([stf, stb], 1).reshape(b, 2, S5_G, 2, S5_P)
    return jnp.transpose(st, (0, 1, 2, 4, 3))


def _mla_kernel(*refs, seq, lctx):
    if lctx:
        (cq_ref, ckv_ref, krab_ref, cosq_ref, sinq_ref, cosk_ref, sink_ref, cckv_ref, ckr_ref,
         wqa_ref, wqb_ref, wk_ref, wv_ref, o_ref, k_scr, v_scr, s_scr) = refs
    else:
        (cq_ref, ckv_ref, krab_ref, cosq_ref, sinq_ref, cosk_ref, sink_ref,
         wqa_ref, wqb_ref, wk_ref, wv_ref, o_ref, k_scr, v_scr, s_scr) = refs
    lk = lctx + seq
    rb = min(seq, 512)

    @pl.when(pl.program_id(1) == 0)
    def _build_keys():
        if lctx:
            ckv = cckv_ref[0]
            kn = _bdot(ckv, wk_ref[...])
            vn = _bdot(ckv, wv_ref[...])
            for h in range(MLA_H):
                k_scr[h, 0:lctx, :] = (kn[:, h * LANE:(h + 1) * LANE] + ckr_ref[0]).astype(BF16)
                v_scr[h, 0:lctx, :] = vn[:, h * LANE:(h + 1) * LANE].astype(BF16)

        def chunk(i, _):
            r0 = pl.multiple_of(i * rb, rb)
            rows = pl.ds(r0, rb)
            ckv = ckv_ref[rows, :]
            kn = _bdot(ckv, wk_ref[...])
            vn = _bdot(ckv, wv_ref[...])
            krab = krab_ref[rows, :]
            krx = krab[:, :LANE] * cosk_ref[rows, :] + krab[:, LANE:] * sink_ref[rows, :]
            dst = pl.ds(pl.multiple_of(lctx + r0, rb if lctx % rb == 0 else 256), rb)
            for h in range(MLA_H):
                k_scr[h, dst, :] = (kn[:, h * LANE:(h + 1) * LANE] + krx).astype(BF16)
                v_scr[h, dst, :] = vn[:, h * LANE:(h + 1) * LANE].astype(BF16)
            return 0

        lax.fori_loop(0, seq // rb, chunk, 0)

    cq = cq_ref[...]
    qa = _bdot(cq, wqa_ref[...])
    qb = _bdot(cq, wqb_ref[...])
    cosq, sinq = cosq_ref[...], sinq_ref[...]
    scale = 1.0 / math.sqrt(MLA_NOPE + MLA_ROPE)
    outs = []
    chunks = [slice(j * MLA_KC, (j + 1) * MLA_KC) for j in range(lk // MLA_KC)]
    for h in range(MLA_H):
        hs = slice(h * LANE, (h + 1) * LANE)
        qh = ((qa[:, hs] * cosq + qb[:, hs] * sinq) * scale).astype(BF16)
        mp = None
        for ks in chunks:
            s = _bdot_nt(qh, k_scr[h, ks, :])
            s_scr[:, ks] = s
            for t in range(MLA_KC // LANE):
                part = s[:, t * LANE:(t + 1) * LANE]
                mp = part if mp is None else jnp.maximum(mp, part)
        m = jnp.max(mp, axis=-1, keepdims=True)
        lp = jnp.zeros((MLA_TQ, LANE), F32)
        acc = jnp.zeros((MLA_TQ, LANE), F32)
        for ks in chunks:
            e = jnp.exp(s_scr[:, ks] - m)
            for t in range(MLA_KC // LANE):
                lp = lp + e[:, t * LANE:(t + 1) * LANE]
            acc = acc + _bdot(e, v_scr[h, ks, :])
        outs.append((acc / jnp.sum(lp, axis=-1, keepdims=True))[:, :MLA_V])
    o_ref[...] = jnp.concatenate(outs, axis=-1)


def _mla_call(cq, ckv, krab, rope, ctx, w, nb, seq, row0):
    lctx = 0 if ctx is None else ctx[0].shape[1]
    nq = seq // MLA_TQ
    q0, s0 = row0 // MLA_TQ, row0 // seq
    cosx, sinx = rope
    qrow = lambda n: pl.BlockSpec((MLA_TQ, n), lambda b, i: (q0 + b * nq + i, 0))
    srow = lambda n: pl.BlockSpec((seq, n), lambda b, i: (s0 + b, 0))
    full = lambda a: pl.BlockSpec(a.shape, lambda b, i: (0,) * a.ndim)
    in_specs = [qrow(256), srow(128), srow(256), pl.BlockSpec((MLA_TQ, LANE), lambda b, i: (i, 0)),
                pl.BlockSpec((MLA_TQ, LANE), lambda b, i: (i, 0)), full(cosx), full(sinx)]
    args = [cq, ckv, krab, cosx, sinx, cosx, sinx]
    if lctx:
        in_specs += [pl.BlockSpec((1, lctx, LANE), lambda b, i: (b, 0, 0))] * 2
        args += list(ctx)
    in_specs += [full(a) for a in w]
    args += list(w)
    lk = lctx + seq
    return pl.pallas_call(
        functools.partial(_mla_kernel, seq=seq, lctx=lctx), grid=(nb, nq),
        in_specs=in_specs,
        out_specs=pl.BlockSpec((MLA_TQ, 256), lambda b, i: (b * nq + i, 0)),
        out_shape=jax.ShapeDtypeStruct((nb * seq, 256), F32),
        scratch_shapes=[pltpu.VMEM((MLA_H, lk, LANE), BF16), pltpu.VMEM((MLA_H, lk, LANE), BF16),
                        pltpu.VMEM((MLA_TQ, lk), F32)],
        compiler_params=_cparams("arbitrary", "arbitrary"), name="mla",
    )(*args)


def _swap8(w):
    q = MLA_ROPE // 4
    return jnp.concatenate([w[..., j * q:(j + 1) * q] for j in (1, 0, 3, 2)], -1)


def _mla_weights(wuq, wuk, wuv):
    z32 = jnp.zeros((MLA_QLORA, 32), F32)
    z64 = jnp.zeros((MLA_QLORA, 64), F32)
    k64 = jnp.zeros((MLA_KVLORA, 64), F32)
    qa, qb, wk, wv = [], [], [], []
    for h in range(MLA_H):
        rope_w = wuq[:, h, MLA_NOPE:]
        qa += [wuq[:, h, :MLA_NOPE], rope_w, z32]
        qb += [z64, _swap8(rope_w), z32]
        wk += [wuk[:, h], k64]
        wv += [wuv[:, h], k64]
    cat = lambda xs: jnp.concatenate(xs, -1).astype(BF16)
    return cat(qa), cat(qb), cat(wk), cat(wv)


def _rope_tables(seq, rotate):
    ones = np.ones((seq, MLA_NOPE))
    z32 = np.zeros((seq, 32))
    if not rotate:
        cosx, sinx = np.concatenate([ones, np.ones((seq, MLA_ROPE)), z32], -1), np.zeros((seq, LANE))
    else:
        pos = np.arange(seq)
        row = (pos // GRID_W).astype(np.float64)
        col = (pos % GRID_W).astype(np.float64)
        half = MLA_ROPE // 2
        inv = ROPE_BASE ** (-np.arange(0, half, 2, dtype=np.float64) / half)
        ang = np.concatenate([row[:, None] * inv, col[:, None] * inv], -1)
        cos, sin = np.cos(ang), np.sin(ang)
        q = MLA_ROPE // 4
        cos32 = np.concatenate([cos[:, :q], cos[:, :q], cos[:, q:], cos[:, q:]], -1)
        sin32 = np.concatenate([-sin[:, :q], sin[:, :q], -sin[:, q:], sin[:, q:]], -1)
        cosx, sinx = np.concatenate([ones, cos32, z32], -1), np.concatenate([0.0 * ones, sin32, z32], -1)
    return jnp.asarray(cosx, F32), jnp.asarray(sinx, F32)


def _cumsum_rows(x, reverse):
    n = x.shape[0]
    row = lax.broadcasted_iota(jnp.int32, x.shape, 0)
    s = 1
    while s < n:
        if reverse:
            x = x + jnp.where(row < n - s, pltpu.roll(x, n - s, 0), 0.0)
        else:
            x = x + jnp.where(row >= s, pltpu.roll(x, s, 0), 0.0)
        s *= 2
    return x


def _ssd_kernel(xbc_ref, z_ref, dt_ref, cw_ref, cb_ref, par_ref, dsk_ref, nw_ref, h0_ref,
                y_ref, st_ref, xc_scr, xt_scr, y_scr, hf_scr, hb_scr, *, seq):
    q = SSD_CHUNK
    nc = seq // q
    row = lax.broadcasted_iota(jnp.int32, (q, 1), 0)

    def conv(c, _):
        t0 = pl.multiple_of(c * q, q)
        cur = xbc_ref[pl.ds(t0, q), :]
        prev8 = xbc_ref[pl.ds(pl.multiple_of(jnp.maximum(t0 - 8, 0), 8), 8), :]
        next8 = xbc_ref[pl.ds(pl.multiple_of(jnp.minimum(t0 + q, seq - 8), 8), 8), :]
        prow = jnp.where(c > 0, prev8[7:8], 0.0)
        nrow = jnp.where(c < nc - 1, next8[0:1], 0.0)
        up = jnp.where(row == 0, prow, pltpu.roll(cur, 1, 0))
        dn = jnp.where(row == q - 1, nrow, pltpu.roll(cur, q - 1, 0))
        acc = up * cw_ref[0:1] + cur * cw_ref[1:2] + dn * cw_ref[2:3] + cb_ref[...]
        xc = jax.nn.silu(acc)
        xc_scr[pl.ds(t0, q), :] = xc
        xt_scr[c] = jnp.transpose(xc[:, :SSD_W])
        y_scr[pl.ds(t0, q), :] = jnp.zeros((q, SSD_W), F32)
        return 0

    lax.fori_loop(0, nc, conv, 0)

    pair_rows = (SSD_H // SSD_G) * SSD_P
    for d, h_scr in enumerate((hf_scr, hb_scr)):
        h_scr[...] = jnp.zeros_like(h_scr)
        for g in range(SSD_G):
            h_scr[g * pair_rows:(g + 1) * pair_rows, g * SSD_N:(g + 1) * SSD_N] = (
                h0_ref[0, d, g * pair_rows:(g + 1) * pair_rows, :])
    ii = lax.broadcasted_iota(jnp.int32, (q, q), 0)
    jj = lax.broadcasted_iota(jnp.int32, (q, q), 1)
    first_half = lax.broadcasted_iota(jnp.int32, (q, LANE), 1) < SSD_N

    def one_chunk(c, d, h_scr):
        t0 = pl.multiple_of(c * q, q)
        xc = xc_scr[pl.ds(t0, q), :]
        xs_t = xt_scr[c]
        dt = jax.nn.softplus(dt_ref[pl.ds(t0, q), :] + par_ref[d:d + 1])
        cs = _cumsum_rows(dt * par_ref[2 + d:3 + d], reverse=(d == 1))
        edge = cs[q - 1:q] if d == 0 else cs[0:1]
        cs_t = jnp.transpose(cs)
        dt_t = jnp.transpose(dt)
        wts_t = jnp.transpose(jnp.exp(edge - cs) * dt)
        decay = jnp.exp(edge)
        mask = (ii >= jj) if d == 0 else (ii <= jj)
        bm_pair = xc[:, SSD_W:SSD_W + LANE]
        cm_pair = xc[:, SSD_W + LANE:SSD_W + 2 * LANE]
        for g in range(SSD_G):
            in_g = first_half if g == 0 else jnp.logical_not(first_half)
            pair = slice(g * LANE, (g + 1) * LANE)
            heads = [g * (SSD_H // SSD_G) + hh for hh in range(SSD_H // SSD_G)]
            cm = jnp.where(in_g, cm_pair, 0.0)
            bm = jnp.where(in_g, bm_pair, 0.0)
            cbm = _bdot_nt(cm, bm_pair)
            cs_i = [jnp.transpose(jnp.broadcast_to(cs_t[h:h + 1, :], (q, q))) for h in heads]
            y_off = _bdot_nt(cm, h_scr[pair, :]) * jnp.exp(jnp.where(first_half, cs_i[0], cs_i[1]))
            ys = []
            for hh, h in enumerate(heads):
                lmat = jnp.exp(jnp.where(mask, cs_i[hh] - cs_t[h:h + 1, :], -jnp.inf))
                ys.append(_bdot(cbm * lmat * dt_t[h:h + 1, :], xc[:, pair]))
                ps = slice(h * SSD_P, (h + 1) * SSD_P)
                st = _bdot(xs_t[ps, :] * wts_t[h:h + 1, :], bm)
                h_scr[ps, :] = decay[:, h:h + 1] * h_scr[ps, :] + st
            y_scr[pl.ds(t0, q), pair] += jnp.where(first_half, ys[0], ys[1]) + y_off

    def chunks(c, _):
        one_chunk(c, 0, hf_scr)
        one_chunk(nc - 1 - c, 1, hb_scr)
        return 0

    lax.fori_loop(0, nc, chunks, 0, unroll=2)
    for d, h_scr in enumerate((hf_scr, hb_scr)):
        for g in range(SSD_G):
            st_ref[0, d, g * pair_rows:(g + 1) * pair_rows, :] = (
                h_scr[g * pair_rows:(g + 1) * pair_rows, g * SSD_N:(g + 1) * SSD_N])

    def finish(c, _):
        rows = pl.ds(pl.multiple_of(c * q, q), q)
        y = y_scr[rows, :] + dsk_ref[...] * xc_scr[rows, 0:SSD_W]
        y_ref[rows, :] = _rms(y * jax.nn.silu(z_ref[rows, :]), nw_ref[...])
        return 0

    lax.fori_loop(0, nc, finish, 0)


def _ssd_call(xbc, z, dt, w, h0, nb, seq, row0):
    s0 = row0 // seq
    srow = lambda n: pl.BlockSpec((seq, n), lambda b: (s0 + b, 0))
    full = lambda a: pl.BlockSpec(a.shape, lambda b: (0,) * a.ndim)
    hp = SSD_H * SSD_P
    st_spec = pl.BlockSpec((1, 2, hp, SSD_N), lambda b: (b, 0, 0, 0))
    return pl.pallas_call(
        functools.partial(_ssd_kernel, seq=seq), grid=(nb,),
        in_specs=[srow(SSD_XBC), srow(SSD_W), srow(LANE)] + [full(a) for a in w] + [st_spec],
        out_specs=[pl.BlockSpec((seq, SSD_W), lambda b: (b, 0)), st_spec],
        out_shape=[jax.ShapeDtypeStruct((nb * seq, SSD_W), F32), jax.ShapeDtypeStruct((nb, 2, hp, SSD_N), F32)],
        scratch_shapes=[pltpu.VMEM((seq, SSD_XBC), F32), pltpu.VMEM((seq // SSD_CHUNK, SSD_W, SSD_CHUNK), F32),
                        pltpu.VMEM((seq, SSD_W), F32), pltpu.VMEM((hp, LANE), F32), pltpu.VMEM((hp, LANE), F32)],
        compiler_params=_cparams("arbitrary"), name="ssd",
    )(xbc, z, dt, *w, h0)


def _ssd_weights(conv_w, conv_b, dt_bias, a_log, dskip, norm_w):
    pad = lambda x: jnp.pad(x, ((0, 0), (0, LANE - SSD_H)))
    par = jnp.concatenate([pad(dt_bias), pad(-jnp.exp(a_log)), jnp.zeros((4, LANE), F32)], 0)
    cw = jnp.concatenate([conv_w, jnp.zeros((5, SSD_XBC), F32)], 0)
    return (cw, conv_b.reshape(1, SSD_XBC), par, jnp.repeat(dskip, SSD_P).reshape(1, SSD_W),
            norm_w.reshape(1, SSD_W))


def _conf_kernel(v_ref, w_ref, b_ref, g_ref, bt_ref, o_ref, pad_scr, sh_scr, *, seq, seg):
    q = CONF_Q
    halo = 2 * CONF_PAD
    tail = pad_scr.shape[0] - seq - CONF_PAD
    pad_scr[0:CONF_PAD, :] = jnp.zeros((CONF_PAD, CONF_W), F32)
    pad_scr[seq + CONF_PAD:, :] = jnp.zeros((tail, CONF_W), F32)
    pad_scr[CONF_PAD:seq + CONF_PAD, :] = v_ref[...]
    lo = CONF_PAD - CONF_K // 2
    pieces = [(j, min(LANE, seg + halo - j)) for j in range(0, seg + halo, LANE)]

    def segment(s, _):
        s0 = pl.multiple_of(s * seg, seg)
        for j0, n in pieces:
            big = pad_scr[pl.ds(s0 + j0, n + SUBLANE), :]
            for r in range(SUBLANE):
                sh_scr[r, j0:j0 + n, :] = big[r:r + n]

        def chunk(c, _):
            t0 = pl.multiple_of(c * q, q)
            acc = jnp.zeros((q, CONF_W), F32) + b_ref[...]
            for k in range(CONF_K):
                r, off = (lo + k) % SUBLANE, (lo + k) // SUBLANE * SUBLANE
                wk = jnp.concatenate([w_ref[k * SUBLANE:(k + 1) * SUBLANE, :]] * (q // SUBLANE), 0)
                acc = acc + sh_scr[r, pl.ds(pl.multiple_of(t0 + off, SUBLANE), q), :] * wk
            mu = jnp.mean(acc, axis=-1, keepdims=True)
            xc = acc - mu
            var = jnp.mean(xc * xc, axis=-1, keepdims=True)
            o_ref[pl.ds(s0 + t0, q), :] = jax.nn.silu(xc * lax.rsqrt(var + EPS) * g_ref[...] + bt_ref[...])
            return 0

        lax.fori_loop(0, seg // q, chunk, 0, unroll=2)
        return 0

    lax.fori_loop(0, seq // seg, segment, 0)


def _conf_call(v, w, nb, seq, row0):
    s0 = row0 // seq
    full = lambda a: pl.BlockSpec(a.shape, lambda b: (0,) * a.ndim)
    seg = min(seq, CONF_SEG)
    halo = 2 * CONF_PAD
    return pl.pallas_call(
        functools.partial(_conf_kernel, seq=seq, seg=seg), grid=(nb,),
        in_specs=[pl.BlockSpec((seq, CONF_W), lambda b: (s0 + b, 0))] + [full(a) for a in w],
        out_specs=pl.BlockSpec((seq, CONF_W), lambda b: (b, 0)),
        out_shape=jax.ShapeDtypeStruct((nb * seq, CONF_W), F32),
        scratch_shapes=[pltpu.VMEM((seq + halo + 2 * SUBLANE, CONF_W), F32),
                        pltpu.VMEM((SUBLANE, seg + halo, CONF_W), F32)],
        compiler_params=_cparams("arbitrary"), name="conf",
    )(v, *w)


def _split_bf16(x):
    hi = x.astype(BF16)
    return hi, (x - hi.astype(F32)).astype(BF16)


def _outproj_kernel(xp_ref, xs_ref, ada_ref, s5p_ref, s5s_ref, mlap_ref, mlas_ref, ssdp_ref, ssds_ref, confp_ref,
                    confs_ref, wglu_ref, bglu_ref, wo_ref, npost_ref, npre_ref, wrh_ref, wrl_ref, br_ref, perm_ref,
                    x1_ref, h2_ref, gate_ref, *, n_first):
    for s in range(TOKEN_SUB):
        _outproj_tile(s, xp_ref, xs_ref, ada_ref, s5p_ref, s5s_ref, mlap_ref, mlas_ref, ssdp_ref, ssds_ref, confp_ref,
                      confs_ref, wglu_ref, bglu_ref, wo_ref, npost_ref, npre_ref, wrh_ref, wrl_ref, br_ref, perm_ref,
                      x1_ref, h2_ref, gate_ref, n_first)


def _outproj_tile(s, xp_ref, xs_ref, ada_ref, s5p_ref, s5s_ref, mlap_ref, mlas_ref, ssdp_ref, ssds_ref, confp_ref,
                  confs_ref, wglu_ref, bglu_ref, wo_ref, npost_ref, npre_ref, wrh_ref, wrl_ref, br_ref, perm_ref,
                  x1_ref, h2_ref, gate_ref, n_first):
    rows = slice(s * TM, (s + 1) * TM)
    cpt = TM // S5_CHUNK
    is_first = pl.program_id(0) < n_first
    pick = lambda first, second: jnp.where(is_first, first[rows, :], second[rows, :])
    ada = ada_ref[s]
    ys5 = jnp.where(is_first, s5p_ref[:, s * cpt:(s + 1) * cpt, :], s5s_ref[:, s * cpt:(s + 1) * cpt, :])
    a = jax.nn.gelu(_group_rows_to_chunk_rows([ys5[g].astype(F32) for g in range(S5_G)], perm_ref[...]))
    s5 = a * jax.nn.sigmoid(_bdot(a, wglu_ref[...]) + bglu_ref[...])
    mix = (_bdot(s5, wo_ref[0:256, :]) + _bdot(pick(mlap_ref, mlas_ref), wo_ref[256:512, :])
           + _bdot(pick(ssdp_ref, ssds_ref), wo_ref[512:768, :]) + _bdot(pick(confp_ref, confs_ref), wo_ref[768:1024, :]))
    x1 = pick(xp_ref, xs_ref) + ada[2:3] * _rms(mix, npost_ref[...])
    x1_ref[rows, :] = x1
    h2 = _rms(x1, npre_ref[...]) * (1.0 + ada[4:5]) + ada[3:4]
    h2_ref[rows, :] = h2.astype(BF16)
    hi, lo = _split_bf16(h2)
    lg = (jnp.dot(hi, wrh_ref[...], preferred_element_type=F32) + jnp.dot(lo, wrh_ref[...], preferred_element_type=F32)
          + jnp.dot(hi, wrl_ref[...], preferred_element_type=F32) + br_ref[...])
    lane = lax.broadcasted_iota(jnp.int32, lg.shape, 1)
    neg = -jnp.inf
    big = jnp.int32(1 << 20)
    first = lambda hit: jnp.min(jnp.where(hit, lane, big), axis=-1, keepdims=True)
    glm = jnp.where((lane >= MOE_E) & (lane < MOE_E + MOE_GROUPS), lg, neg)
    gmax = jnp.max(glm, axis=-1, keepdims=True)
    p_group = 1.0 / jnp.sum(jnp.exp(glm - gmax), axis=-1, keepdims=True)
    gsel = first(glm == gmax) - MOE_E
    elm = jnp.where((lane < MOE_E) & ((lane // MOE_PER_GROUP) == gsel), lg, neg)
    v1 = jnp.max(elm, axis=-1, keepdims=True)
    i1 = first(elm == v1)
    elm2 = jnp.where(lane == i1, neg, elm)
    v2 = jnp.max(elm2, axis=-1, keepdims=True)
    i2 = first(elm2 == v2)
    e2 = jnp.exp(v2 - v1)
    w1 = p_group / (1.0 + e2)
    w2 = p_group * e2 / (1.0 + e2)
    gate_ref[rows, :] = jnp.where(lane == i1, w1, 0.0) + jnp.where(lane == i2, w2, 0.0)


def _outproj_call(x, ada_t, ys5, ymla, yssd, yconf, w):
    t = x[0].shape[0] + x[1].shape[0]
    tm = TOKEN_SUB * TM
    n_first = x[0].shape[0] // tm
    row = lambda n: pl.BlockSpec((tm, n), lambda i: (i, 0))
    full = lambda a: pl.BlockSpec(a.shape, lambda i: (0,) * a.ndim)
    w = tuple(w) + (_chunk_perm(),)
    pair = lambda n: _two_group_specs((tm, n), n_first)
    return pl.pallas_call(
        functools.partial(_outproj_kernel, n_first=n_first), grid=(t // tm,),
        in_specs=pair(D_MODEL) + [pl.BlockSpec((TOKEN_SUB, 8, D_MODEL), lambda i: (i, 0, 0))]
        + _two_group_specs((tm // S5_CHUNK, S5_W), n_first, lead=(S5_G,)) + pair(256) + pair(256) + pair(256)
        + [full(a) for a in w],
        out_specs=[row(D_MODEL), row(D_MODEL), row(LANE)],
        out_shape=[jax.ShapeDtypeStruct((t, D_MODEL), F32), jax.ShapeDtypeStruct((t, D_MODEL), BF16),
                   jax.ShapeDtypeStruct((t, LANE), F32)],
        compiler_params=_cparams("arbitrary"), name="outproj",
    )(*x, ada_t, *ys5, *ymla, *yssd, *yconf, *w)


def _moe_kernel(h_ref, gate_ref, x1_ref, ada_ref, wg_ref, wu_ref, wd_ref, npost_ref, op_ref, os_ref, hid_scr,
                *, n_first):
    i = pl.program_id(0)
    h = h_ref[...]
    gate = gate_ref[...]
    lane = lax.broadcasted_iota(jnp.int32, gate.shape, 1)
    for e in range(MOE_E):
        ge = jnp.sum(jnp.where(lane == e, gate, 0.0), axis=-1, keepdims=True)
        hid = (jax.nn.silu(jnp.dot(h, wg_ref[e], preferred_element_type=F32))
               * jnp.dot(h, wu_ref[e], preferred_element_type=F32) * ge)
        hid_scr[:, e * MOE_HID:(e + 1) * MOE_HID] = hid.astype(BF16)
    out = jnp.dot(hid_scr[...], wd_ref[...], preferred_element_type=F32)
    res = x1_ref[...] + ada_ref[0][5:6] * _rms(out, npost_ref[...])

    @pl.when(i < n_first)
    def _store_first():
        op_ref[...] = res

    @pl.when(i >= n_first)
    def _store_second():
        os_ref[...] = res


def _moe_call(h2, gate, x1, ada_t, wg, wu, wd, npost, t_first):
    t = h2.shape[0]
    tm = MOE_TM
    n_first = t_first // tm
    row = lambda n: pl.BlockSpec((tm, n), lambda i: (i, 0))
    resident = lambda a: pl.BlockSpec(a.shape, lambda i: (0,) * a.ndim, pipeline_mode=pl.Buffered(1))
    wd = wd.reshape(MOE_E * MOE_HID, D_MODEL)
    return pl.pallas_call(
        functools.partial(_moe_kernel, n_first=n_first), grid=(t // tm,),
        in_specs=[row(D_MODEL), row(LANE), row(D_MODEL),
                  pl.BlockSpec((1, 8, D_MODEL), lambda i: (i * (tm // TM), 0, 0)),
                  resident(wg), resident(wu), resident(wd), pl.BlockSpec(npost.shape, lambda i: (0, 0))],
        out_specs=_two_group_specs((tm, D_MODEL), n_first),
        out_shape=[jax.ShapeDtypeStruct((t_first, D_MODEL), F32), jax.ShapeDtypeStruct((t - t_first, D_MODEL), F32)],
        scratch_shapes=[pltpu.VMEM((tm, MOE_E * MOE_HID), BF16)],
        compiler_params=_cparams("arbitrary"), name="moe",
    )(h2, gate, x1, ada_t, wg, wu, wd, npost)


def _pack_w_in(w):
    lead = w.shape[:-1]
    z32, z64 = jnp.zeros(lead + (32,), F32), jnp.zeros(lead + (64,), F32)
    kr = w[..., OFF_MLA_KR:OFF_SSD_Z]
    zdt = jnp.zeros(lead + (LANE - SSD_H,), F32)
    cols = [w[..., OFF_S5:OFF_MLA_KR], z64, kr, z32, z64, _swap8(kr), z32,
            w[..., OFF_SSD_Z:OFF_SSD_DT], w[..., OFF_SSD_DT:OFF_CONF], zdt, w[..., OFF_CONF:IN_COLS]]
    return jnp.concatenate(cols, -1).astype(BF16)


def _row(v):
    return v.reshape(1, -1)


def kernel(x_prompt, x_sample, cache_mla_ckv, cache_mla_krope, state_s5, state_ssd, c, c_ctx, w_ada, b_ada, norm_pre1, norm_post1, norm_pre2, norm_post2, w_in, w_out, s5_lam_re, s5_lam_im, s5_log_dt, s5_b_re, s5_b_im, s5_c_re, s5_c_im, s5_d, s5_w_glu, s5_b_glu, mla_qnorm, mla_kvnorm, mla_wuq, mla_wuk, mla_wuv, ssd_conv_w, ssd_conv_b, ssd_dt_bias, ssd_a_log, ssd_d, ssd_norm_w, conf_dw_w, conf_dw_b, conf_ln_g, conf_ln_b, moe_wg, moe_bg, moe_we, moe_be, moe_w_gate, moe_w_up, moe_w_down):
    bp, lp, d = x_prompt.shape
    bs, ls, _ = x_sample.shape
    tp, ts = bp * lp, bs * ls
    x = (x_prompt.reshape(tp, d), x_sample.reshape(ts, d))

    crows = jnp.concatenate([c_ctx[None], c, jnp.zeros((8 - 1 - bs, d), F32)], 0)
    ada = _ada_call(crows, w_ada, b_ada)
    tile_row = np.concatenate([np.zeros(tp // TM, np.int32), 1 + np.repeat(np.arange(bs, dtype=np.int32), ls // TM)])

    rope_p = _rope_tables(lp, rotate=False)
    rope_s = _rope_tables(ls, rotate=True)
    zeros_s5 = tuple(jnp.zeros((bp, S5_G * 2 * S5_P), F32) for _ in range(4))
    zeros_ssd = jnp.zeros((bp, 2, SSD_H * SSD_P, SSD_N), F32)
    pad_kr = lambda kr: jnp.pad(kr, ((0, 0), (0, 0), (MLA_NOPE, LANE - MLA_NOPE - MLA_ROPE)))

    w_in_packed = _pack_w_in(w_in)

    ckv_l, kr_l, s5_l, ssd_l = [], [], [], []
    for l in range(DEPTH):
        ada_t = jnp.pad(ada[l][tile_row].reshape(-1, 6, d), ((0, 0), (0, 2), (0, 0)))
        u, cq, ckv, krab, z, xbc, dt, v = _inproj_call(
            *x, ada_t, _row(norm_pre1[l]), w_in_packed, l, _row(mla_qnorm[l]), _row(mla_kvnorm[l]))

        tabs = _s5_tables(s5_lam_re[l], s5_lam_im[l], s5_log_dt[l], s5_b_re[l], s5_b_im[l], s5_c_re[l], s5_c_im[l],
                          s5_d[l])
        ncp = tp // S5_CHUNK
        ncs = ts // S5_CHUNK
        yp, stf, stb = _s5_call(u, tabs, zeros_s5, bp, ncp, 0)
        ys, _, _ = _s5_call(u, tabs, _s5_state_in(state_s5[:, l]), bs, ncs, ncp // ncs)
        y_s5 = (yp, ys)
        s5_l.append(_s5_state_out(stf, stb))

        mw = _mla_weights(mla_wuq[l], mla_wuk[l], mla_wuv[l])
        y_mla = (_mla_call(cq, ckv, krab, rope_p, None, mw, bp, lp, 0),
                 _mla_call(cq, ckv, krab, rope_s, (cache_mla_ckv[:, l], pad_kr(cache_mla_krope[:, l])), mw, bs, ls, tp))
        ckv_l.append(ckv[:tp].reshape(bp, lp, MLA_KVLORA))
        kr_l.append(krab[:tp, MLA_NOPE:MLA_NOPE + MLA_ROPE].reshape(bp, lp, MLA_ROPE))

        sw = _ssd_weights(ssd_conv_w[l], ssd_conv_b[l], ssd_dt_bias[l], ssd_a_log[l], ssd_d[l], ssd_norm_w[l])
        yssd_p, st_ssd = _ssd_call(xbc, z, dt, sw, zeros_ssd, bp, lp, 0)
        yssd_s, _ = _ssd_call(xbc, z, dt, sw, state_ssd[:, l].reshape(bs, 2, SSD_H * SSD_P, SSD_N), bs, ls, tp)
        ssd_l.append(st_ssd.reshape(bp, 2, SSD_H, SSD_P, SSD_N))

        cw = (jnp.repeat(conf_dw_w[l], SUBLANE, axis=0), _row(conf_dw_b[l]),
              _row(conf_ln_g[l]), _row(conf_ln_b[l]))
        y_conf = (_conf_call(v, cw, bp, lp, 0), _conf_call(v, cw, bs, ls, tp))

        wr = jnp.pad(jnp.concatenate([moe_we[l], moe_wg[l]], -1), ((0, 0), (0, LANE - MOE_E - MOE_GROUPS)))
        wrh, wrl = _split_bf16(wr)
        br = jnp.pad(jnp.concatenate([moe_be[l], moe_bg[l]]), (0, LANE - MOE_E - MOE_GROUPS)).reshape(1, LANE)
        ow = (s5_w_glu[l].astype(BF16), _row(s5_b_glu[l]), w_out[l].astype(BF16), _row(norm_post1[l]),
              _row(norm_pre2[l]), wrh, wrl, br)
        x1, h2, gate = _outproj_call(x, ada_t, y_s5, y_mla, (yssd_p, yssd_s), y_conf, ow)

        x = _moe_call(h2, gate, x1, ada_t, moe_w_gate[l].astype(BF16), moe_w_up[l].astype(BF16),
                      moe_w_down[l].astype(BF16), _row(norm_post2[l]), tp)

    return (x[0].reshape(bp, lp, d), x[1].reshape(bs, ls, d),
            jnp.stack(ckv_l, 1), jnp.stack(kr_l, 1), jnp.stack(s5_l, 1), jnp.stack(ssd_l, 1))
```

```python
import functools
import math

import jax
import jax.numpy as jnp
import numpy as np
from jax import lax
from jax.experimental import pallas as pl
from jax.experimental.pallas import tpu as pltpu

F32 = jnp.float32
BF16 = jnp.bfloat16

D_MODEL = 1024
DEPTH = 2
GRID_W = 64
EPS = 1e-6

S5_W = 256
S5_GSIZE = 16
S5_G = 16
S5_P = 64
S5_CHUNK = 16

MLA_H = 4
MLA_NOPE = 64
MLA_ROPE = 32
MLA_V = 64
MLA_QLORA = 256
MLA_KVLORA = 128
ROPE_BASE = 10000.0
MLA_TQ = 256
MLA_KC = 256

SSD_W = 256
SSD_P = 64
SSD_H = 4
SSD_G = 2
SSD_N = 64
SSD_CHUNK = 128
SSD_XBC = 512

CONF_W = 256
CONF_K = 31
CONF_Q = 128
CONF_PAD = 16
CONF_SEG = 1024

MOE_GROUPS = 4
MOE_PER_GROUP = 4
MOE_E = 16
MOE_HID = 256
MOE_TM = 512

OFF_S5 = 0
OFF_MLA_Q = 256
OFF_MLA_KV = 512
OFF_MLA_KR = 640
OFF_SSD_Z = 672
OFF_SSD_XBC = 928
OFF_SSD_DT = 1440
OFF_CONF = 1444
IN_COLS = 1956

LANE = 128
SUBLANE = 8
TM = 256
TOKEN_SUB = 4
P_U, P_Q, P_KV, P_KRA, P_KRB, P_Z, P_XBC, P_DT, P_CA, P_CB, P_END = (
    0, 256, 512, 640, 768, 896, 1152, 1664, 1792, 2048, 2304)

VMEM_LIMIT = 56 * 1024 * 1024


def _bdot(a, b):
    return jnp.dot(a.astype(BF16), b.astype(BF16), preferred_element_type=F32)


def _bdot_nt(a, b):
    return lax.dot_general(a.astype(BF16), b.astype(BF16), (((1,), (1,)), ((), ())), preferred_element_type=F32)


def _rms(x, g):
    return x * lax.rsqrt(jnp.mean(x * x, axis=-1, keepdims=True) + EPS) * g


def _cparams(*sem):
    return pltpu.CompilerParams(dimension_semantics=sem if sem else None, vmem_limit_bytes=VMEM_LIMIT)


def _ada_kernel(c_ref, w_ref, b_ref, o_ref):
    o_ref[0] = _bdot(jax.nn.silu(c_ref[...]), w_ref[0]) + b_ref[0]


def _ada_call(crows, w_ada, b_ada):
    tn = 1024
    d6 = 6 * D_MODEL
    return pl.pallas_call(
        _ada_kernel, grid=(DEPTH, d6 // tn),
        in_specs=[pl.BlockSpec((8, D_MODEL), lambda l, j: (0, 0)),
                  pl.BlockSpec((1, D_MODEL, tn), lambda l, j: (l, 0, j)),
                  pl.BlockSpec((1, 1, tn), lambda l, j: (l, 0, j))],
        out_specs=pl.BlockSpec((1, 8, tn), lambda l, j: (l, 0, j)),
        out_shape=jax.ShapeDtypeStruct((DEPTH, 8, d6), F32),
        compiler_params=_cparams("arbitrary", "arbitrary"), name="ada",
    )(crows, w_ada, b_ada.reshape(DEPTH, 1, d6))


def _lane_block(shape):
    return lax.broadcasted_iota(jnp.int32, shape, 1) // S5_GSIZE


def _chunk_rows_to_group_rows(x, perm):
    r = jnp.dot(perm, x.astype(BF16), preferred_element_type=F32)
    blk = _lane_block((S5_CHUNK, S5_W))
    out = []
    for g in range(S5_G):
        acc = None
        for t in range(S5_CHUNK):
            src = r[t * S5_CHUNK:(t + 1) * S5_CHUNK, :]
            shift = ((t - g) * S5_GSIZE) % S5_W
            src = pltpu.roll(src, shift, 1) if shift else src
            acc = src if acc is None else jnp.where(blk == t, src, acc)
        out.append(acc)
    return out


def _group_rows_to_chunk_rows(ys, perm_t):
    blk = _lane_block((S5_CHUNK, S5_W))
    rows = []
    for t in range(S5_CHUNK):
        acc = None
        for g in range(S5_G):
            shift = ((g - t) * S5_GSIZE) % S5_W
            src = pltpu.roll(ys[g], shift, 1) if shift else ys[g]
            acc = src if acc is None else jnp.where(blk == g, src, acc)
        rows.append(acc)
    z = jnp.concatenate(rows, 0)
    return jnp.dot(perm_t, z.astype(BF16), preferred_element_type=F32)


def _two_group_specs(block, n_first, lead=()):
    nl = len(lead)
    first = pl.BlockSpec(lead + block, lambda i, *_: (0,) * nl + (jnp.minimum(i, n_first - 1), 0))
    second = pl.BlockSpec(lead + block, lambda i, *_: (0,) * nl + (jnp.maximum(i - n_first, 0), 0))
    return [first, second]


def _inproj_kernel(xp_ref, xs_ref, ada_ref, npre_ref, w_ref, qn_ref, kvn_ref, perm_ref,
                   u_ref, cq_ref, ckv_ref, krab_ref, z_ref, xbc_ref, dt_ref, v_ref, *, n_first):
    cpt = TM // S5_CHUNK
    for s in range(TOKEN_SUB):
        rows = slice(s * TM, (s + 1) * TM)
        ada = ada_ref[s]
        x = jnp.where(pl.program_id(0) < n_first, xp_ref[rows, :], xs_ref[rows, :])
        h = _rms(x, npre_ref[...]) * (1.0 + ada[1:2]) + ada[0:1]
        p = _bdot(h, w_ref[...])
        for g, ug in enumerate(_chunk_rows_to_group_rows(p[:, P_U:P_Q], perm_ref[...])):
            u_ref[g, s * cpt:(s + 1) * cpt, :] = ug.astype(BF16)
        cq_ref[rows, :] = _rms(p[:, P_Q:P_KV], qn_ref[...]).astype(BF16)
        ckv_ref[rows, :] = _rms(p[:, P_KV:P_KRA], kvn_ref[...])
        krab_ref[rows, :] = p[:, P_KRA:P_Z]
        z_ref[rows, :] = p[:, P_Z:P_XBC]
        xbc_ref[rows, :] = p[:, P_XBC:P_DT]
        dt_ref[rows, :] = p[:, P_DT:P_CA]
        v_ref[rows, :] = p[:, P_CA:P_CB] * jax.nn.sigmoid(p[:, P_CB:P_END])


def _chunk_perm():
    idx = np.arange(TM)
    src = (idx % S5_CHUNK) * S5_CHUNK + idx // S5_CHUNK
    return jnp.asarray(np.eye(TM, dtype=np.float32)[src], BF16)


def _inproj_call(xp, xs, ada_t, npre, w, layer, qn, kvn):
    t = xp.shape[0] + xs.shape[0]
    tm = TOKEN_SUB * TM
    n_first = xp.shape[0] // tm
    row = lambda n: pl.BlockSpec((tm, n), lambda i: (i, 0))
    full = lambda a: pl.BlockSpec(a.shape, lambda i: (0,) * a.ndim)
    widths = (256, 128, 256, 256, 512, 128, 256)
    dts = (BF16, F32, F32, F32, F32, F32, F32)
    perm = _chunk_perm()
    cpt = tm // S5_CHUNK
    return pl.pallas_call(
        functools.partial(_inproj_kernel, n_first=n_first), grid=(t // tm,),
        in_specs=_two_group_specs((tm, D_MODEL), n_first)
        + [pl.BlockSpec((TOKEN_SUB, 8, D_MODEL), lambda i: (i, 0, 0)), full(npre),
           pl.BlockSpec((None,) + w.shape[1:], lambda i: (layer, 0, 0)), full(qn), full(kvn), full(perm)],
        out_specs=[pl.BlockSpec((S5_G, cpt, S5_W), lambda i: (0, i, 0))] + [row(n) for n in widths],
        out_shape=[jax.ShapeDtypeStruct((S5_G, t // S5_CHUNK, S5_W), BF16)]
        + [jax.ShapeDtypeStruct((t, n), d) for n, d in zip(widths, dts)],
        compiler_params=_cparams("arbitrary"), name="inproj",
    )(xp, xs, ada_t, npre, w, qn, kvn, perm)


def _s5_kernel(u_ref, r_ref, vf_ref, vb_ref, a_ref, h0f_ref, h0fs_ref, h0b_ref, h0bs_ref, perm_ref, permt_ref,
               y_ref, stf_ref, stb_ref, ucb, sf, sfs, sb, sbs, *, nb, nc):
    gw = 2 * S5_P
    for g in range(S5_G):
        ucb[g] = jnp.dot(perm_ref[...], u_ref[g], preferred_element_type=F32).astype(BF16)
        r = jnp.dot(ucb[g], r_ref[g, :, S5_W:], preferred_element_type=F32)
        sl = slice(g * gw, (g + 1) * gw)
        sf[:, sl] = r[:, 0:128]
        sfs[:, sl] = r[:, 128:256]
        sb[:, sl] = r[:, 256:384]
        sbs[:, sl] = r[:, 384:512]
    arf, aif, aisf = a_ref[0:1], a_ref[1:2], a_ref[2:3]
    arb, aib, aisb = a_ref[3:4], a_ref[4:5], a_ref[5:6]
    cpt = 1 if nb % SUBLANE == 0 else SUBLANE // nb
    rows = cpt * nb
    nt = nc // cpt

    def body(i, carry):
        hf, hfs, hb, hbs = carry
        rf = pl.ds(pl.multiple_of(i * rows, rows), rows)
        rb = pl.ds(pl.multiple_of((nt - 1 - i) * rows, rows), rows)
        inc_f, inc_fs, inc_b, inc_bs = sf[rf, :], sfs[rf, :], sb[rb, :], sbs[rb, :]
        in_f, in_b = [None] * cpt, [None] * cpt
        for k in range(cpt):
            kf = slice(k * nb, (k + 1) * nb)
            kb = slice((cpt - 1 - k) * nb, (cpt - k) * nb)
            in_f[k] = hf
            in_b[cpt - 1 - k] = hb
            hf, hfs = arf * hf + aif * hfs + inc_f[kf], arf * hfs + aisf * hf + inc_fs[kf]
            hb, hbs = arb * hb + aib * hbs + inc_b[kb], arb * hbs + aisb * hb + inc_bs[kb]
        sf[rf, :] = in_f[0] if cpt == 1 else jnp.concatenate(in_f, 0)
        sb[rb, :] = in_b[0] if cpt == 1 else jnp.concatenate(in_b, 0)
        return hf, hfs, hb, hbs

    hf, _, hb, _ = lax.fori_loop(0, nt, body, (h0f_ref[...], h0fs_ref[...], h0b_ref[...], h0bs_ref[...]))
    stf_ref[...] = hf
    stb_ref[...] = hb
    for g in range(S5_G):
        sl = slice(g * gw, (g + 1) * gw)
        y = (jnp.dot(ucb[g], r_ref[g, :, :S5_W], preferred_element_type=F32)
             + _bdot(sf[:, sl], vf_ref[g]) + _bdot(sb[:, sl], vb_ref[g]))
        y_ref[g] = jnp.dot(permt_ref[...], y.astype(BF16), preferred_element_type=F32).astype(BF16)


def _s5_call(u, tabs, h0, nb, n, block):
    r, vf, vb, avec = tabs
    sw = S5_G * 2 * S5_P
    nc = n // nb
    dst = np.arange(n)
    src = (dst % nb) * nc + dst // nb
    perm = np.eye(n, dtype=np.float32)[src]
    consts = (jnp.asarray(perm, BF16), jnp.asarray(perm.T, BF16))
    full = lambda a: pl.BlockSpec(a.shape, lambda i: (0,) * len(a.shape))
    st = jax.ShapeDtypeStruct((nb, sw), F32)
    args = (r, vf, vb, avec, *h0, *consts)
    return pl.pallas_call(
        functools.partial(_s5_kernel, nb=nb, nc=nc), grid=(1,),
        in_specs=[pl.BlockSpec((S5_G, n, S5_W), lambda i: (0, block, 0))] + [full(a) for a in args],
        out_specs=[full(jax.ShapeDtypeStruct((S5_G, n, S5_W), BF16)), full(st), full(st)],
        out_shape=[jax.ShapeDtypeStruct((S5_G, n, S5_W), BF16), st, st],
        scratch_shapes=[pltpu.VMEM((S5_G, n, S5_W), BF16)] + [pltpu.VMEM((n, sw), F32) for _ in range(4)],
        compiler_params=_cparams("arbitrary"), name="s5",
    )(u, *args)


def _cpow(br, bi, e, nbits):
    pr = jnp.ones(e.shape, F32)
    pi = jnp.zeros(e.shape, F32)
    for k in range(nbits):
        bit = ((e >> k) & 1) == 1
        pr, pi = jnp.where(bit, pr * br - pi * bi, pr), jnp.where(bit, pr * bi + pi * br, pi)
        br, bi = br * br - bi * bi, 2.0 * br * bi
    return pr, pi


def _s5tab_kernel(prow_ref, pcol_ref, bt_ref, ct_ref, dsk_ref, etile_ref, r_ref, vf_ref, vb_ref, a_ref):
    q = S5_CHUNK
    hp = lax.Precision.HIGHEST

    def abar(lre, lim, ldt):
        dt = jnp.exp(ldt)
        mag = jnp.exp(lre * dt)
        return mag * jnp.cos(lim * dt), mag * jnp.sin(lim * dt)

    def per_step(x):
        return jnp.dot(x, etile_ref[...], precision=hp, preferred_element_type=F32)

    step_of_col = lax.broadcasted_iota(jnp.int32, (S5_P, S5_W), 1) // S5_GSIZE
    step_of_row = lax.broadcasted_iota(jnp.int32, (S5_W, S5_P), 0) // S5_GSIZE
    kcat, wtabs, arows = [], [], []
    for d in range(2):
        prow = prow_ref[d, 0]
        lre, lim = prow[0:1], prow[1:2]
        abr, abi = abar(lre, lim, prow[2:3])
        den = lre * lre + lim * lim
        fr = ((abr - 1.0) * lre + abi * lim) / den
        fi = (abi * lre - (abr - 1.0) * lim) / den
        btr, bti = bt_ref[d, 0, 0], bt_ref[d, 0, 1]
        bbr, bbi = fr * btr - fi * bti, fr * bti + fi * btr
        pcol = pcol_ref[d, 0]
        cbr, cbi = abar(pcol[:, 0:1], pcol[:, 1:2], pcol[:, 2:3])
        ctr, cti = per_step(ct_ref[d, 0, 0]), per_step(ct_ref[d, 0, 1])

        pr, pi = _cpow(cbr, cbi, step_of_col if d == 0 else q - 1 - step_of_col, 4)
        lr, li = ctr * pr - cti * pi, ctr * pi + cti * pr
        vr, vi = lr * cbr - li * cbi, lr * cbi + li * cbr
        (vf_ref if d == 0 else vb_ref)[0] = jnp.concatenate([vr, -vi], 0).astype(BF16)
        kcat.append(jnp.dot(bbr, lr, precision=hp, preferred_element_type=F32)
                    - jnp.dot(bbi, li, precision=hp, preferred_element_type=F32))
        pr, pi = _cpow(abr, abi, q - 1 - step_of_row if d == 0 else step_of_row, 4)
        tbr, tbi = jnp.concatenate([bbr] * q, 0), jnp.concatenate([bbi] * q, 0)
        wr, wi = pr * tbr - pi * tbi, pr * tbi + pi * tbr
        wtabs += [wr, wi, wi, wr]
        ar, ai = _cpow(abr, abi, jnp.full((1, S5_P), q, jnp.int32), 5)
        arows += [jnp.concatenate([ar, ar], 1), jnp.concatenate([-ai, ai], 1), jnp.concatenate([ai, -ai], 1)]
    lane = lax.broadcasted_iota(jnp.int32, (S5_GSIZE, S5_W), 1)
    rows = []
    for s in range(q):
        shr = S5_GSIZE * s
        shl = S5_GSIZE * (q - 1 - s)
        f = jnp.where(lane >= shr, pltpu.roll(kcat[0], shr, 1), 0.0) if shr else kcat[0]
        b = jnp.where(lane < S5_W - shl, pltpu.roll(kcat[1], S5_W - shl, 1), 0.0) if shl else kcat[1]
        rows.append(f + b)
    ri = lax.broadcasted_iota(jnp.int32, (S5_W, S5_W), 0)
    ci = lax.broadcasted_iota(jnp.int32, (S5_W, S5_W), 1)
    tt = jnp.concatenate(rows, 0) + jnp.where(ri == ci, dsk_ref[0], 0.0)
    r_ref[0] = jnp.concatenate([tt] + wtabs, 1).astype(BF16)
    a_ref[...] = jnp.concatenate(arows + [jnp.zeros((2, 2 * S5_P), F32)], 0)


def _s5_tables(lam_re, lam_im, log_dt, b_re, b_im, c_re, c_im, dskip):
    zeros = jnp.zeros((2, S5_G, 5, S5_P), F32)
    prow = jnp.concatenate([lam_re[:, :, None], lam_im[:, :, None],
                            jnp.broadcast_to(log_dt[:, :, None, None], (2, S5_G, 1, S5_P)), zeros], 2)
    pcol = jnp.swapaxes(prow, 2, 3)
    bt = jnp.stack([jnp.swapaxes(b_re, 2, 3), jnp.swapaxes(b_im, 2, 3)], 2)
    ct = jnp.stack([jnp.swapaxes(c_re, 2, 3), jnp.swapaxes(c_im, 2, 3)], 2)
    dsk = jnp.tile(dskip.reshape(S5_G, 1, S5_GSIZE), (1, 1, S5_CHUNK))
    etile = jnp.asarray(np.tile(np.eye(S5_GSIZE, dtype=np.float32), (1, S5_CHUNK)))
    gblk = lambda a: pl.BlockSpec((2, 1) + a.shape[2:], lambda g: (0, g) + (0,) * (a.ndim - 2))
    sw = S5_G * 2 * S5_P
    return pl.pallas_call(
        _s5tab_kernel, grid=(S5_G,),
        in_specs=[gblk(prow), gblk(pcol), gblk(bt), gblk(ct), pl.BlockSpec((1, 1, S5_W), lambda g: (g, 0, 0)),
                  pl.BlockSpec(etile.shape, lambda g: (0, 0))],
        out_specs=[pl.BlockSpec((1, S5_W, 768), lambda g: (g, 0, 0)), pl.BlockSpec((1, 2 * S5_P, S5_W), lambda g: (g, 0, 0)),
                   pl.BlockSpec((1, 2 * S5_P, S5_W), lambda g: (g, 0, 0)), pl.BlockSpec((8, 2 * S5_P), lambda g: (0, g))],
        out_shape=[jax.ShapeDtypeStruct((S5_G, S5_W, 768), BF16), jax.ShapeDtypeStruct((S5_G, 2 * S5_P, S5_W), BF16),
                   jax.ShapeDtypeStruct((S5_G, 2 * S5_P, S5_W), BF16), jax.ShapeDtypeStruct((8, sw), F32)],
        compiler_params=_cparams("arbitrary"), name="s5tab",
    )(prow, pcol, bt, ct, dsk, etile)


def _s5_state_in(state):
    b = state.shape[0]
    sw = jnp.transpose(state, (0, 1, 2, 4, 3))
    plain = sw.reshape(b, 2, S5_G * 2 * S5_P)
    swapped = sw[:, :, :, ::-1, :].reshape(b, 2, S5_G * 2 * S5_P)
    return plain[:, 0], swapped[:, 0], plain[:, 1], swapped[:, 1]


def _s5_state_out(stf, stb):
    b = stf.shape[0]
    st = jnp.stack([stf, stb], 1).reshape(b, 2, S5_G, 2, S5_P)
    return jnp.transpose(st, (0, 1, 2, 4, 3))


def _mla_kernel(*refs, seq, lctx):
    if lctx:
        (cq_ref, ckv_ref, krab_ref, cosq_ref, sinq_ref, cosk_ref, sink_ref, cckv_ref, ckr_ref,
         wqa_ref, wqb_ref, wk_ref, wv_ref, o_ref, k_scr, v_scr, s_scr) = refs
    else:
        (cq_ref, ckv_ref, krab_ref, cosq_ref, sinq_ref, cosk_ref, sink_ref,
         wqa_ref, wqb_ref, wk_ref, wv_ref, o_ref, k_scr, v_scr, s_scr) = refs
    lk = lctx + seq
    rb = min(seq, 512)

    @pl.when(pl.program_id(1) == 0)
    def _build_keys():
        if lctx:
            ckv = cckv_ref[0]
            kn = _bdot(ckv, wk_ref[...])
            vn = _bdot(ckv, wv_ref[...])
            for h in range(MLA_H):
                k_scr[h, 0:lctx, :] = (kn[:, h * LANE:(h + 1) * LANE] + ckr_ref[0]).astype(BF16)
                v_scr[h, 0:lctx, :] = vn[:, h * LANE:(h + 1) * LANE].astype(BF16)

        def chunk(i, _):
            r0 = pl.multiple_of(i * rb, rb)
            rows = pl.ds(r0, rb)
            ckv = ckv_ref[rows, :]
            kn = _bdot(ckv, wk_ref[...])
            vn = _bdot(ckv, wv_ref[...])
            krab = krab_ref[rows, :]
            krx = krab[:, :LANE] * cosk_ref[rows, :] + krab[:, LANE:] * sink_ref[rows, :]
            dst = pl.ds(pl.multiple_of(lctx + r0, rb if lctx % rb == 0 else 256), rb)
            for h in range(MLA_H):
                k_scr[h, dst, :] = (kn[:, h * LANE:(h + 1) * LANE] + krx).astype(BF16)
                v_scr[h, dst, :] = vn[:, h * LANE:(h + 1) * LANE].astype(BF16)
            return 0

        lax.fori_loop(0, seq // rb, chunk, 0)

    cq = cq_ref[...]
    qa = _bdot(cq, wqa_ref[...])
    qb = _bdot(cq, wqb_ref[...])
    cosq, sinq = cosq_ref[...], sinq_ref[...]
    scale = 1.0 / math.sqrt(MLA_NOPE + MLA_ROPE)
    outs = []
    chunks = [slice(j * MLA_KC, (j + 1) * MLA_KC) for j in range(lk // MLA_KC)]
    for h in range(MLA_H):
        hs = slice(h * LANE, (h + 1) * LANE)
        qh = ((qa[:, hs] * cosq + qb[:, hs] * sinq) * scale).astype(BF16)
        mp = None
        for ks in chunks:
            s = _bdot_nt(qh, k_scr[h, ks, :])
            s_scr[:, ks] = s
            for t in range(MLA_KC // LANE):
                part = s[:, t * LANE:(t + 1) * LANE]
                mp = part if mp is None else jnp.maximum(mp, part)
        m = jnp.max(mp, axis=-1, keepdims=True)
        lp = jnp.zeros((MLA_TQ, LANE), F32)
        acc = jnp.zeros((MLA_TQ, LANE), F32)
        for ks in chunks:
            e = jnp.exp(s_scr[:, ks] - m)
            for t in range(MLA_KC // LANE):
                lp = lp + e[:, t * LANE:(t + 1) * LANE]
            acc = acc + _bdot(e, v_scr[h, ks, :])
        outs.append((acc / jnp.sum(lp, axis=-1, keepdims=True))[:, :MLA_V])
    o_ref[...] = jnp.concatenate(outs, axis=-1)


def _mla_call(cq, ckv, krab, rope, ctx, w, nb, seq, row0):
    lctx = 0 if ctx is None else ctx[0].shape[1]
    nq = seq // MLA_TQ
    q0, s0 = row0 // MLA_TQ, row0 // seq
    cosx, sinx = rope
    qrow = lambda n: pl.BlockSpec((MLA_TQ, n), lambda b, i: (q0 + b * nq + i, 0))
    srow = lambda n: pl.BlockSpec((seq, n), lambda b, i: (s0 + b, 0))
    full = lambda a: pl.BlockSpec(a.shape, lambda b, i: (0,) * a.ndim)
    in_specs = [qrow(256), srow(128), srow(256), pl.BlockSpec((MLA_TQ, LANE), lambda b, i: (i, 0)),
                pl.BlockSpec((MLA_TQ, LANE), lambda b, i: (i, 0)), full(cosx), full(sinx)]
    args = [cq, ckv, krab, cosx, sinx, cosx, sinx]
    if lctx:
        in_specs += [pl.BlockSpec((1, lctx, LANE), lambda b, i: (b, 0, 0))] * 2
        args += list(ctx)
    in_specs += [full(a) for a in w]
    args += list(w)
    lk = lctx + seq
    return pl.pallas_call(
        functools.partial(_mla_kernel, seq=seq, lctx=lctx), grid=(nb, nq),
        in_specs=in_specs,
        out_specs=pl.BlockSpec((MLA_TQ, 256), lambda b, i: (b * nq + i, 0)),
        out_shape=jax.ShapeDtypeStruct((nb * seq, 256), F32),
        scratch_shapes=[pltpu.VMEM((MLA_H, lk, LANE), BF16), pltpu.VMEM((MLA_H, lk, LANE), BF16),
                        pltpu.VMEM((MLA_TQ, lk), F32)],
        compiler_params=_cparams("arbitrary", "arbitrary"), name="mla",
    )(*args)


def _swap8(w):
    q = MLA_ROPE // 4
    return jnp.concatenate([w[..., j * q:(j + 1) * q] for j in (1, 0, 3, 2)], -1)


def _mla_weights(wuq, wuk, wuv):
    z32 = jnp.zeros((MLA_QLORA, 32), F32)
    z64 = jnp.zeros((MLA_QLORA, 64), F32)
    k64 = jnp.zeros((MLA_KVLORA, 64), F32)
    qa, qb, wk, wv = [], [], [], []
    for h in range(MLA_H):
        rope_w = wuq[:, h, MLA_NOPE:]
        qa += [wuq[:, h, :MLA_NOPE], rope_w, z32]
        qb += [z64, _swap8(rope_w), z32]
        wk += [wuk[:, h], k64]
        wv += [wuv[:, h], k64]
    cat = lambda xs: jnp.concatenate(xs, -1).astype(BF16)
    return cat(qa), cat(qb), cat(wk), cat(wv)


def _rope_tables(seq, rotate):
    ones = np.ones((seq, MLA_NOPE))
    z32 = np.zeros((seq, 32))
    if not rotate:
        cosx, sinx = np.concatenate([ones, np.ones((seq, MLA_ROPE)), z32], -1), np.zeros((seq, LANE))
    else:
        pos = np.arange(seq)
        row = (pos // GRID_W).astype(np.float64)
        col = (pos % GRID_W).astype(np.float64)
        half = MLA_ROPE // 2
        inv = ROPE_BASE ** (-np.arange(0, half, 2, dtype=np.float64) / half)
        ang = np.concatenate([row[:, None] * inv, col[:, None] * inv], -1)
        cos, sin = np.cos(ang), np.sin(ang)
        q = MLA_ROPE // 4
        cos32 = np.concatenate([cos[:, :q], cos[:, :q], cos[:, q:], cos[:, q:]], -1)
        sin32 = np.concatenate([-sin[:, :q], sin[:, :q], -sin[:, q:], sin[:, q:]], -1)
        cosx, sinx = np.concatenate([ones, cos32, z32], -1), np.concatenate([0.0 * ones, sin32, z32], -1)
    return jnp.asarray(cosx, F32), jnp.asarray(sinx, F32)


def _cumsum_rows(x, reverse):
    n = x.shape[0]
    row = lax.broadcasted_iota(jnp.int32, x.shape, 0)
    s = 1
    while s < n:
        if reverse:
            x = x + jnp.where(row < n - s, pltpu.roll(x, n - s, 0), 0.0)
        else:
            x = x + jnp.where(row >= s, pltpu.roll(x, s, 0), 0.0)
        s *= 2
    return x


def _ssd_kernel(xbc_ref, z_ref, dt_ref, cw_ref, cb_ref, par_ref, dsk_ref, nw_ref, h0_ref,
                y_ref, st_ref, xc_scr, xt_scr, y_scr, hf_scr, hb_scr, *, seq):
    q = SSD_CHUNK
    nc = seq // q
    row = lax.broadcasted_iota(jnp.int32, (q, 1), 0)

    def conv(c, _):
        t0 = pl.multiple_of(c * q, q)
        cur = xbc_ref[pl.ds(t0, q), :]
        prev8 = xbc_ref[pl.ds(pl.multiple_of(jnp.maximum(t0 - 8, 0), 8), 8), :]
        next8 = xbc_ref[pl.ds(pl.multiple_of(jnp.minimum(t0 + q, seq - 8), 8), 8), :]
        prow = jnp.where(c > 0, prev8[7:8], 0.0)
        nrow = jnp.where(c < nc - 1, next8[0:1], 0.0)
        up = jnp.where(row == 0, prow, pltpu.roll(cur, 1, 0))
        dn = jnp.where(row == q - 1, nrow, pltpu.roll(cur, q - 1, 0))
        acc = up * cw_ref[0:1] + cur * cw_ref[1:2] + dn * cw_ref[2:3] + cb_ref[...]
        xc = jax.nn.silu(acc)
        xc_scr[pl.ds(t0, q), :] = xc
        xt_scr[c] = jnp.transpose(xc[:, :SSD_W])
        y_scr[pl.ds(t0, q), :] = jnp.zeros((q, SSD_W), F32)
        return 0

    lax.fori_loop(0, nc, conv, 0)

    pair_rows = (SSD_H // SSD_G) * SSD_P
    for d, h_scr in enumerate((hf_scr, hb_scr)):
        h_scr[...] = jnp.zeros_like(h_scr)
        for g in range(SSD_G):
            h_scr[g * pair_rows:(g + 1) * pair_rows, g * SSD_N:(g + 1) * SSD_N] = (
                h0_ref[0, d, g * pair_rows:(g + 1) * pair_rows, :])
    ii = lax.broadcasted_iota(jnp.int32, (q, q), 0)
    jj = lax.broadcasted_iota(jnp.int32, (q, q), 1)
    first_half = lax.broadcasted_iota(jnp.int32, (q, LANE), 1) < SSD_N

    def one_chunk(c, d, h_scr):
        t0 = pl.multiple_of(c * q, q)
        xc = xc_scr[pl.ds(t0, q), :]
        xs_t = xt_scr[c]
        dt = jax.nn.softplus(dt_ref[pl.ds(t0, q), :] + par_ref[d:d + 1])
        cs = _cumsum_rows(dt * par_ref[2 + d:3 + d], reverse=(d == 1))
        edge = cs[q - 1:q] if d == 0 else cs[0:1]
        cs_t = jnp.transpose(cs)
        dt_t = jnp.transpose(dt)
        wts_t = jnp.transpose(jnp.exp(edge - cs) * dt)
        decay = jnp.exp(edge)
        mask = (ii >= jj) if d == 0 else (ii <= jj)
        bm_pair = xc[:, SSD_W:SSD_W + LANE]
        cm_pair = xc[:, SSD_W + LANE:SSD_W + 2 * LANE]
        for g in range(SSD_G):
            in_g = first_half if g == 0 else jnp.logical_not(first_half)
            pair = slice(g * LANE, (g + 1) * LANE)
            heads = [g * (SSD_H // SSD_G) + hh for hh in range(SSD_H // SSD_G)]
            cm = jnp.where(in_g, cm_pair, 0.0)
            bm = jnp.where(in_g, bm_pair, 0.0)
            cbm = _bdot_nt(cm, bm_pair)
            cs_i = [jnp.transpose(jnp.broadcast_to(cs_t[h:h + 1, :], (q, q))) for h in heads]
            y_off = _bdot_nt(cm, h_scr[pair, :]) * jnp.exp(jnp.where(first_half, cs_i[0], cs_i[1]))
            ys = []
            for hh, h in enumerate(heads):
                lmat = jnp.exp(jnp.where(mask, cs_i[hh] - cs_t[h:h + 1, :], -jnp.inf))
                ys.append(_bdot(cbm * lmat * dt_t[h:h + 1, :], xc[:, pair]))
                ps = slice(h * SSD_P, (h + 1) * SSD_P)
                st = _bdot(xs_t[ps, :] * wts_t[h:h + 1, :], bm)
                h_scr[ps, :] = decay[:, h:h + 1] * h_scr[ps, :] + st
            y_scr[pl.ds(t0, q), pair] += jnp.where(first_half, ys[0], ys[1]) + y_off

    def chunks(c, _):
        one_chunk(c, 0, hf_scr)
        one_chunk(nc - 1 - c, 1, hb_scr)
        return 0

    lax.fori_loop(0, nc, chunks, 0, unroll=min(nc, 4))
    for d, h_scr in enumerate((hf_scr, hb_scr)):
        for g in range(SSD_G):
            st_ref[0, d, g * pair_rows:(g + 1) * pair_rows, :] = (
                h_scr[g * pair_rows:(g + 1) * pair_rows, g * SSD_N:(g + 1) * SSD_N])

    def finish(c, _):
        rows = pl.ds(pl.multiple_of(c * q, q), q)
        y = y_scr[rows, :] + dsk_ref[...] * xc_scr[rows, 0:SSD_W]
        y_ref[rows, :] = _rms(y * jax.nn.silu(z_ref[rows, :]), nw_ref[...])
        return 0

    lax.fori_loop(0, nc, finish, 0)


def _ssd_call(xbc, z, dt, w, h0, nb, seq, row0):
    s0 = row0 // seq
    srow = lambda n: pl.BlockSpec((seq, n), lambda b: (s0 + b, 0))
    full = lambda a: pl.BlockSpec(a.shape, lambda b: (0,) * a.ndim)
    hp = SSD_H * SSD_P
    st_spec = pl.BlockSpec((1, 2, hp, SSD_N), lambda b: (b, 0, 0, 0))
    return pl.pallas_call(
        functools.partial(_ssd_kernel, seq=seq), grid=(nb,),
        in_specs=[srow(SSD_XBC), srow(SSD_W), srow(LANE)] + [full(a) for a in w] + [st_spec],
        out_specs=[pl.BlockSpec((seq, SSD_W), lambda b: (b, 0)), st_spec],
        out_shape=[jax.ShapeDtypeStruct((nb * seq, SSD_W), F32), jax.ShapeDtypeStruct((nb, 2, hp, SSD_N), F32)],
        scratch_shapes=[pltpu.VMEM((seq, SSD_XBC), F32), pltpu.VMEM((seq // SSD_CHUNK, SSD_W, SSD_CHUNK), F32),
                        pltpu.VMEM((seq, SSD_W), F32), pltpu.VMEM((hp, LANE), F32), pltpu.VMEM((hp, LANE), F32)],
        compiler_params=_cparams("arbitrary"), name="ssd",
    )(xbc, z, dt, *w, h0)


def _ssd_weights(conv_w, conv_b, dt_bias, a_log, dskip, norm_w):
    pad = lambda x: jnp.pad(x, ((0, 0), (0, LANE - SSD_H)))
    par = jnp.concatenate([pad(dt_bias), pad(-jnp.exp(a_log)), jnp.zeros((4, LANE), F32)], 0)
    cw = jnp.concatenate([conv_w, jnp.zeros((5, SSD_XBC), F32)], 0)
    return (cw, conv_b.reshape(1, SSD_XBC), par, jnp.repeat(dskip, SSD_P).reshape(1, SSD_W),
            norm_w.reshape(1, SSD_W))


def _conf_kernel(v_ref, w_ref, b_ref, g_ref, bt_ref, o_ref, pad_scr, sh_scr, *, seq, seg):
    q = CONF_Q
    halo = 2 * CONF_PAD
    tail = pad_scr.shape[0] - seq - CONF_PAD
    pad_scr[0:CONF_PAD, :] = jnp.zeros((CONF_PAD, CONF_W), F32)
    pad_scr[seq + CONF_PAD:, :] = jnp.zeros((tail, CONF_W), F32)
    pad_scr[CONF_PAD:seq + CONF_PAD, :] = v_ref[...]
    lo = CONF_PAD - CONF_K // 2
    pieces = [(j, min(LANE, seg + halo - j)) for j in range(0, seg + halo, LANE)]

    def segment(s, _):
        s0 = pl.multiple_of(s * seg, seg)
        for j0, n in pieces:
            big = pad_scr[pl.ds(s0 + j0, n + SUBLANE), :]
            for r in range(SUBLANE):
                sh_scr[r, j0:j0 + n, :] = big[r:r + n]

        def chunk(c, _):
            t0 = pl.multiple_of(c * q, q)
            acc = jnp.zeros((q, CONF_W), F32) + b_ref[...]
            for k in range(CONF_K):
                r, off = (lo + k) % SUBLANE, (lo + k) // SUBLANE * SUBLANE
                wk = jnp.concatenate([w_ref[k * SUBLANE:(k + 1) * SUBLANE, :]] * (q // SUBLANE), 0)
                acc = acc + sh_scr[r, pl.ds(pl.multiple_of(t0 + off, SUBLANE), q), :] * wk
            mu = jnp.mean(acc, axis=-1, keepdims=True)
            xc = acc - mu
            var = jnp.mean(xc * xc, axis=-1, keepdims=True)
            o_ref[pl.ds(s0 + t0, q), :] = jax.nn.silu(xc * lax.rsqrt(var + EPS) * g_ref[...] + bt_ref[...])
            return 0

        lax.fori_loop(0, seg // q, chunk, 0, unroll=2)
        return 0

    lax.fori_loop(0, seq // seg, segment, 0)


def _conf_call(v, w, nb, seq, row0):
    s0 = row0 // seq
    full = lambda a: pl.BlockSpec(a.shape, lambda b: (0,) * a.ndim)
    seg = min(seq, CONF_SEG)
    halo = 2 * CONF_PAD
    return pl.pallas_call(
        functools.partial(_conf_kernel, seq=seq, seg=seg), grid=(nb,),
        in_specs=[pl.BlockSpec((seq, CONF_W), lambda b: (s0 + b, 0))] + [full(a) for a in w],
        out_specs=pl.BlockSpec((seq, CONF_W), lambda b: (b, 0)),
        out_shape=jax.ShapeDtypeStruct((nb * seq, CONF_W), F32),
        scratch_shapes=[pltpu.VMEM((seq + halo + 2 * SUBLANE, CONF_W), F32),
                        pltpu.VMEM((SUBLANE, seg + halo, CONF_W), F32)],
        compiler_params=_cparams("arbitrary"), name="conf",
    )(v, *w)


def _split_bf16(x):
    hi = x.astype(BF16)
    return hi, (x - hi.astype(F32)).astype(BF16)


def _outproj_kernel(xp_ref, xs_ref, ada_ref, s5p_ref, s5s_ref, mlap_ref, mlas_ref, ssdp_ref, ssds_ref, confp_ref,
                    confs_ref, wglu_ref, bglu_ref, wo_ref, npost_ref, npre_ref, wrh_ref, wrl_ref, br_ref, perm_ref,
                    x1_ref, h2_ref, gate_ref, *, n_first):
    for s in range(TOKEN_SUB):
        _outproj_tile(s, xp_ref, xs_ref, ada_ref, s5p_ref, s5s_ref, mlap_ref, mlas_ref, ssdp_ref, ssds_ref, confp_ref,
                      confs_ref, wglu_ref, bglu_ref, wo_ref, npost_ref, npre_ref, wrh_ref, wrl_ref, br_ref, perm_ref,
                      x1_ref, h2_ref, gate_ref, n_first)


def _outproj_tile(s, xp_ref, xs_ref, ada_ref, s5p_ref, s5s_ref, mlap_ref, mlas_ref, ssdp_ref, ssds_ref, confp_ref,
                  confs_ref, wglu_ref, bglu_ref, wo_ref, npost_ref, npre_ref, wrh_ref, wrl_ref, br_ref, perm_ref,
                  x1_ref, h2_ref, gate_ref, n_first):
    rows = slice(s * TM, (s + 1) * TM)
    cpt = TM // S5_CHUNK
    is_first = pl.program_id(0) < n_first
    pick = lambda first, second: jnp.where(is_first, first[rows, :], second[rows, :])
    ada = ada_ref[s]
    ys5 = jnp.where(is_first, s5p_ref[:, s * cpt:(s + 1) * cpt, :], s5s_ref[:, s * cpt:(s + 1) * cpt, :])
    a = jax.nn.gelu(_group_rows_to_chunk_rows([ys5[g].astype(F32) for g in range(S5_G)], perm_ref[...]))
    s5 = a * jax.nn.sigmoid(_bdot(a, wglu_ref[...]) + bglu_ref[...])
    mix = (_bdot(s5, wo_ref[0:256, :]) + _bdot(pick(mlap_ref, mlas_ref), wo_ref[256:512, :])
           + _bdot(pick(ssdp_ref, ssds_ref), wo_ref[512:768, :]) + _bdot(pick(confp_ref, confs_ref), wo_ref[768:1024, :]))
    x1 = pick(xp_ref, xs_ref) + ada[2:3] * _rms(mix, npost_ref[...])
    x1_ref[rows, :] = x1
    h2 = _rms(x1, npre_ref[...]) * (1.0 + ada[4:5]) + ada[3:4]
    h2_ref[rows, :] = h2.astype(BF16)
    hi, lo = _split_bf16(h2)
    lg = (jnp.dot(hi, wrh_ref[...], preferred_element_type=F32) + jnp.dot(lo, wrh_ref[...], preferred_element_type=F32)
          + jnp.dot(hi, wrl_ref[...], preferred_element_type=F32) + br_ref[...])
    lane = lax.broadcasted_iota(jnp.int32, lg.shape, 1)
    neg = -jnp.inf
    big = jnp.int32(1 << 20)
    first = lambda hit: jnp.min(jnp.where(hit, lane, big), axis=-1, keepdims=True)
    glm = jnp.where((lane >= MOE_E) & (lane < MOE_E + MOE_GROUPS), lg, neg)
    gmax = jnp.max(glm, axis=-1, keepdims=True)
    p_group = 1.0 / jnp.sum(jnp.exp(glm - gmax), axis=-1, keepdims=True)
    gsel = first(glm == gmax) - MOE_E
    elm = jnp.where((lane < MOE_E) & ((lane // MOE_PER_GROUP) == gsel), lg, neg)
    v1 = jnp.max(elm, axis=-1, keepdims=True)
    i1 = first(elm == v1)
    elm2 = jnp.where(lane == i1, neg, elm)
    v2 = jnp.max(elm2, axis=-1, keepdims=True)
    i2 = first(elm2 == v2)
    e2 = jnp.exp(v2 - v1)
    w1 = p_group / (1.0 + e2)
    w2 = p_group * e2 / (1.0 + e2)
    gate_ref[rows, :] = jnp.where(lane == i1, w1, 0.0) + jnp.where(lane == i2, w2, 0.0)


def _outproj_call(x, ada_t, ys5, ymla, yssd, yconf, w):
    t = x[0].shape[0] + x[1].shape[0]
    tm = TOKEN_SUB * TM
    n_first = x[0].shape[0] // tm
    row = lambda n: pl.BlockSpec((tm, n), lambda i: (i, 0))
    full = lambda a: pl.BlockSpec(a.shape, lambda i: (0,) * a.ndim)
    w = tuple(w) + (_chunk_perm(),)
    pair = lambda n: _two_group_specs((tm, n), n_first)
    return pl.pallas_call(
        functools.partial(_outproj_kernel, n_first=n_first), grid=(t // tm,),
        in_specs=pair(D_MODEL) + [pl.BlockSpec((TOKEN_SUB, 8, D_MODEL), lambda i: (i, 0, 0))]
        + _two_group_specs((tm // S5_CHUNK, S5_W), n_first, lead=(S5_G,)) + pair(256) + pair(256) + pair(256)
        + [full(a) for a in w],
        out_specs=[row(D_MODEL), row(D_MODEL), row(LANE)],
        out_shape=[jax.ShapeDtypeStruct((t, D_MODEL), F32), jax.ShapeDtypeStruct((t, D_MODEL), BF16),
                   jax.ShapeDtypeStruct((t, LANE), F32)],
        compiler_params=_cparams("arbitrary"), name="outproj",
    )(*x, ada_t, *ys5, *ymla, *yssd, *yconf, *w)


def _moe_kernel(h_ref, gate_ref, x1_ref, ada_ref, wg_ref, wu_ref, wd_ref, npost_ref, op_ref, os_ref, hid_scr,
                *, n_first):
    i = pl.program_id(0)
    h = h_ref[...]
    gate = gate_ref[...]
    lane = lax.broadcasted_iota(jnp.int32, gate.shape, 1)
    for e in range(MOE_E):
        ge = jnp.sum(jnp.where(lane == e, gate, 0.0), axis=-1, keepdims=True)
        hid = (jax.nn.silu(jnp.dot(h, wg_ref[e], preferred_element_type=F32))
               * jnp.dot(h, wu_ref[e], preferred_element_type=F32) * ge)
        hid_scr[:, e * MOE_HID:(e + 1) * MOE_HID] = hid.astype(BF16)
    out = jnp.dot(hid_scr[...], wd_ref[...], preferred_element_type=F32)
    res = x1_ref[...] + ada_ref[0][5:6] * _rms(out, npost_ref[...])

    @pl.when(i < n_first)
    def _store_first():
        op_ref[...] = res

    @pl.when(i >= n_first)
    def _store_second():
        os_ref[...] = res


def _moe_call(h2, gate, x1, ada_t, wg, wu, wd, layer, npost, t_first):
    t = h2.shape[0]
    tm = MOE_TM
    n_first = t_first // tm
    row = lambda n: pl.BlockSpec((tm, n), lambda i: (i, 0))
    resident = lambda a: pl.BlockSpec((None,) + a.shape[1:], lambda i: (layer,) + (0,) * (a.ndim - 1),
                                      pipeline_mode=pl.Buffered(1))
    wd = wd.reshape(wd.shape[0], MOE_E * MOE_HID, D_MODEL)
    return pl.pallas_call(
        functools.partial(_moe_kernel, n_first=n_first), grid=(t // tm,),
        in_specs=[row(D_MODEL), row(LANE), row(D_MODEL),
                  pl.BlockSpec((1, 8, D_MODEL), lambda i: (i * (tm // TM), 0, 0)),
                  resident(wg), resident(wu), resident(wd), pl.BlockSpec(npost.shape, lambda i: (0, 0))],
        out_specs=_two_group_specs((tm, D_MODEL), n_first),
        out_shape=[jax.ShapeDtypeStruct((t_first, D_MODEL), F32), jax.ShapeDtypeStruct((t - t_first, D_MODEL), F32)],
        scratch_shapes=[pltpu.VMEM((tm, MOE_E * MOE_HID), BF16)],
        compiler_params=_cparams("arbitrary"), name="moe",
    )(h2, gate, x1, ada_t, wg, wu, wd, npost)


def _pack_w_in(w):
    lead = w.shape[:-1]
    z32, z64 = jnp.zeros(lead + (32,), F32), jnp.zeros(lead + (64,), F32)
    kr = w[..., OFF_MLA_KR:OFF_SSD_Z]
    zdt = jnp.zeros(lead + (LANE - SSD_H,), F32)
    cols = [w[..., OFF_S5:OFF_MLA_KR], z64, kr, z32, z64, _swap8(kr), z32,
            w[..., OFF_SSD_Z:OFF_SSD_DT], w[..., OFF_SSD_DT:OFF_CONF], zdt, w[..., OFF_CONF:IN_COLS]]
    return jnp.concatenate(cols, -1).astype(BF16)


def _row(v):
    return v.reshape(1, -1)


def kernel(x_prompt, x_sample, cache_mla_ckv, cache_mla_krope, state_s5, state_ssd, c, c_ctx, w_ada, b_ada, norm_pre1, norm_post1, norm_pre2, norm_post2, w_in, w_out, s5_lam_re, s5_lam_im, s5_log_dt, s5_b_re, s5_b_im, s5_c_re, s5_c_im, s5_d, s5_w_glu, s5_b_glu, mla_qnorm, mla_kvnorm, mla_wuq, mla_wuk, mla_wuv, ssd_conv_w, ssd_conv_b, ssd_dt_bias, ssd_a_log, ssd_d, ssd_norm_w, conf_dw_w, conf_dw_b, conf_ln_g, conf_ln_b, moe_wg, moe_bg, moe_we, moe_be, moe_w_gate, moe_w_up, moe_w_down):
    bp, lp, d = x_prompt.shape
    bs, ls, _ = x_sample.shape
    tp, ts = bp * lp, bs * ls
    x = (x_prompt.reshape(tp, d), x_sample.reshape(ts, d))

    crows = jnp.concatenate([c_ctx[None], c, jnp.zeros((8 - 1 - bs, d), F32)], 0)
    ada = _ada_call(crows, w_ada, b_ada)
    tile_row = np.concatenate([np.zeros(tp // TM, np.int32), 1 + np.repeat(np.arange(bs, dtype=np.int32), ls // TM)])

    rope_p = _rope_tables(lp, rotate=False)
    rope_s = _rope_tables(ls, rotate=True)
    zeros_s5 = tuple(jnp.zeros((bp, S5_G * 2 * S5_P), F32) for _ in range(4))
    zeros_ssd = jnp.zeros((bp, 2, SSD_H * SSD_P, SSD_N), F32)
    pad_kr = lambda kr: jnp.pad(kr, ((0, 0), (0, 0), (MLA_NOPE, LANE - MLA_NOPE - MLA_ROPE)))

    w_in_packed = _pack_w_in(w_in)
    moe_w = [w.astype(BF16) for w in (moe_w_gate, moe_w_up, moe_w_down)]

    ckv_l, kr_l, s5_l, ssd_l = [], [], [], []
    for l in range(DEPTH):
        ada_t = jnp.pad(ada[l][tile_row].reshape(-1, 6, d), ((0, 0), (0, 2), (0, 0)))
        u, cq, ckv, krab, z, xbc, dt, v = _inproj_call(
            *x, ada_t, _row(norm_pre1[l]), w_in_packed, l, _row(mla_qnorm[l]), _row(mla_kvnorm[l]))

        tabs = _s5_tables(s5_lam_re[l], s5_lam_im[l], s5_log_dt[l], s5_b_re[l], s5_b_im[l], s5_c_re[l], s5_c_im[l],
                          s5_d[l])
        ncp = tp // S5_CHUNK
        ncs = ts // S5_CHUNK
        yp, stf, stb = _s5_call(u, tabs, zeros_s5, bp, ncp, 0)
        ys, _, _ = _s5_call(u, tabs, _s5_state_in(state_s5[:, l]), bs, ncs, ncp // ncs)
        y_s5 = (yp, ys)
        s5_l.append(_s5_state_out(stf, stb))

        mw = _mla_weights(mla_wuq[l], mla_wuk[l], mla_wuv[l])
        y_mla = (_mla_call(cq, ckv, krab, rope_p, None, mw, bp, lp, 0),
                 _mla_call(cq, ckv, krab, rope_s, (cache_mla_ckv[:, l], pad_kr(cache_mla_krope[:, l])), mw, bs, ls, tp))
        ckv_l.append(ckv[:tp].reshape(bp, lp, MLA_KVLORA))
        kr_l.append(krab[:tp, MLA_NOPE:MLA_NOPE + MLA_ROPE].reshape(bp, lp, MLA_ROPE))

        sw = _ssd_weights(ssd_conv_w[l], ssd_conv_b[l], ssd_dt_bias[l], ssd_a_log[l], ssd_d[l], ssd_norm_w[l])
        yssd_p, st_ssd = _ssd_call(xbc, z, dt, sw, zeros_ssd, bp, lp, 0)
        yssd_s, _ = _ssd_call(xbc, z, dt, sw, state_ssd[:, l].reshape(bs, 2, SSD_H * SSD_P, SSD_N), bs, ls, tp)
        ssd_l.append(st_ssd.reshape(bp, 2, SSD_H, SSD_P, SSD_N))

        cw = (jnp.repeat(conf_dw_w[l], SUBLANE, axis=0), _row(conf_dw_b[l]),
              _row(conf_ln_g[l]), _row(conf_ln_b[l]))
        y_conf = (_conf_call(v, cw, bp, lp, 0), _conf_call(v, cw, bs, ls, tp))

        wr = jnp.pad(jnp.concatenate([moe_we[l], moe_wg[l]], -1), ((0, 0), (0, LANE - MOE_E - MOE_GROUPS)))
        wrh, wrl = _split_bf16(wr)
        br = jnp.pad(jnp.concatenate([moe_be[l], moe_bg[l]]), (0, LANE - MOE_E - MOE_GROUPS)).reshape(1, LANE)
        ow = (s5_w_glu[l].astype(BF16), _row(s5_b_glu[l]), w_out[l].astype(BF16), _row(norm_post1[l]),
              _row(norm_pre2[l]), wrh, wrl, br)
        x1, h2, gate = _outproj_call(x, ada_t, y_s5, y_mla, (yssd_p, yssd_s), y_conf, ow)

        x = _moe_call(h2, gate, x1, ada_t, *moe_w, l, _row(norm_post2[l]), tp)

    return (x[0].reshape(bp, lp, d), x[1].reshape(bs, ls, d),
            jnp.stack(ckv_l, 1), jnp.stack(kr_l, 1), jnp.stack(s5_l, 1), jnp.stack(ssd_l, 1))
```

```python
import functools
import math

import jax
import jax.numpy as jnp
import numpy as np
from jax import lax
from jax.experimental import pallas as pl
from jax.experimental.pallas import tpu as pltpu

F32 = jnp.float32
BF16 = jnp.bfloat16

D_MODEL = 1024
DEPTH = 2
GRID_W = 64
EPS = 1e-6

S5_W = 256
S5_GSIZE = 16
S5_G = 16
S5_P = 64
S5_CHUNK = 16

MLA_H = 4
MLA_NOPE = 64
MLA_ROPE = 32
MLA_V = 64
MLA_QLORA = 256
MLA_KVLORA = 128
ROPE_BASE = 10000.0
MLA_TQ = 256
MLA_KC = 256

SSD_W = 256
SSD_P = 64
SSD_H = 4
SSD_G = 2
SSD_N = 64
SSD_CHUNK = 128
SSD_XBC = 512
SSD_STEP_ROWS = 512

CONF_W = 256
CONF_K = 31
CONF_Q = 128
CONF_PAD = 16
CONF_SEG = 1024

MOE_GROUPS = 4
MOE_PER_GROUP = 4
MOE_E = 16
MOE_HID = 256
MOE_TM = 512

OFF_S5 = 0
OFF_MLA_Q = 256
OFF_MLA_KV = 512
OFF_MLA_KR = 640
OFF_SSD_Z = 672
OFF_SSD_XBC = 928
OFF_SSD_DT = 1440
OFF_CONF = 1444
IN_COLS = 1956

LANE = 128
SUBLANE = 8
TM = 256
TOKEN_SUB = 4
P_U, P_Q, P_KV, P_KRA, P_KRB, P_Z, P_XBC, P_DT, P_CA, P_CB, P_END = (
    0, 256, 512, 640, 768, 896, 1152, 1664, 1792, 2048, 2304)

VMEM_LIMIT = 56 * 1024 * 1024


def _bdot(a, b):
    return jnp.dot(a.astype(BF16), b.astype(BF16), preferred_element_type=F32)


def _bdot_nt(a, b):
    return lax.dot_general(a.astype(BF16), b.astype(BF16), (((1,), (1,)), ((), ())), preferred_element_type=F32)


def _rms(x, g):
    return x * lax.rsqrt(jnp.mean(x * x, axis=-1, keepdims=True) + EPS) * g


def _cparams(*sem):
    return pltpu.CompilerParams(dimension_semantics=sem if sem else None, vmem_limit_bytes=VMEM_LIMIT)


def _ada_kernel(c_ref, w_ref, b_ref, o_ref):
    o_ref[0] = _bdot(jax.nn.silu(c_ref[...]), w_ref[0]) + b_ref[0]


def _ada_call(crows, w_ada, b_ada):
    tn = 1024
    d6 = 6 * D_MODEL
    return pl.pallas_call(
        _ada_kernel, grid=(DEPTH, d6 // tn),
        in_specs=[pl.BlockSpec((8, D_MODEL), lambda l, j: (0, 0)),
                  pl.BlockSpec((1, D_MODEL, tn), lambda l, j: (l, 0, j)),
                  pl.BlockSpec((1, 1, tn), lambda l, j: (l, 0, j))],
        out_specs=pl.BlockSpec((1, 8, tn), lambda l, j: (l, 0, j)),
        out_shape=jax.ShapeDtypeStruct((DEPTH, 8, d6), F32),
        compiler_params=_cparams("arbitrary", "arbitrary"), name="ada",
    )(crows, w_ada, b_ada.reshape(DEPTH, 1, d6))


def _lane_block(shape):
    return lax.broadcasted_iota(jnp.int32, shape, 1) // S5_GSIZE


def _chunk_rows_to_group_rows(x, perm):
    r = jnp.dot(perm, x.astype(BF16), preferred_element_type=F32)
    blk = _lane_block((S5_CHUNK, S5_W))
    out = []
    for g in range(S5_G):
        acc = None
        for t in range(S5_CHUNK):
            src = r[t * S5_CHUNK:(t + 1) * S5_CHUNK, :]
            shift = ((t - g) * S5_GSIZE) % S5_W
            src = pltpu.roll(src, shift, 1) if shift else src
            acc = src if acc is None else jnp.where(blk == t, src, acc)
        out.append(acc)
    return out


def _group_rows_to_chunk_rows(ys, perm_t):
    blk = _lane_block((S5_CHUNK, S5_W))
    rows = []
    for t in range(S5_CHUNK):
        acc = None
        for g in range(S5_G):
            shift = ((g - t) * S5_GSIZE) % S5_W
            src = pltpu.roll(ys[g], shift, 1) if shift else ys[g]
            acc = src if acc is None else jnp.where(blk == g, src, acc)
        rows.append(acc)
    z = jnp.concatenate(rows, 0)
    return jnp.dot(perm_t, z.astype(BF16), preferred_element_type=F32)


def _two_group_specs(block, n_first, lead=()):
    nl = len(lead)
    first = pl.BlockSpec(lead + block, lambda i, *_: (0,) * nl + (jnp.minimum(i, n_first - 1), 0))
    second = pl.BlockSpec(lead + block, lambda i, *_: (0,) * nl + (jnp.maximum(i - n_first, 0), 0))
    return [first, second]


def _inproj_kernel(xp_ref, xs_ref, ada_ref, npre_ref, w_ref, qn_ref, kvn_ref, perm_ref,
                   u_ref, cq_ref, ckv_ref, krab_ref, z_ref, xbc_ref, dt_ref, v_ref, *, n_first):
    cpt = TM // S5_CHUNK
    for s in range(TOKEN_SUB):
        rows = slice(s * TM, (s + 1) * TM)
        ada = ada_ref[s]
        x = jnp.where(pl.program_id(0) < n_first, xp_ref[rows, :], xs_ref[rows, :])
        h = _rms(x, npre_ref[...] * (1.0 + ada[1:2])) + ada[0:1]
        p = _bdot(h, w_ref[...])
        for g, ug in enumerate(_chunk_rows_to_group_rows(p[:, P_U:P_Q], perm_ref[...])):
            u_ref[g, s * cpt:(s + 1) * cpt, :] = ug.astype(BF16)
        cq_ref[rows, :] = _rms(p[:, P_Q:P_KV], qn_ref[...]).astype(BF16)
        ckv_ref[rows, :] = _rms(p[:, P_KV:P_KRA], kvn_ref[...])
        krab_ref[rows, :] = p[:, P_KRA:P_Z]
        z_ref[rows, :] = p[:, P_Z:P_XBC]
        xbc_ref[rows, :] = p[:, P_XBC:P_DT]
        dt_ref[rows, :] = p[:, P_DT:P_CA]
        v_ref[rows, :] = p[:, P_CA:P_CB] * jax.nn.sigmoid(p[:, P_CB:P_END])


def _chunk_perm():
    idx = np.arange(TM)
    src = (idx % S5_CHUNK) * S5_CHUNK + idx // S5_CHUNK
    return jnp.asarray(np.eye(TM, dtype=np.float32)[src], BF16)


def _inproj_call(xp, xs, ada_t, npre, w, layer, qn, kvn):
    t = xp.shape[0] + xs.shape[0]
    tm = TOKEN_SUB * TM
    n_first = xp.shape[0] // tm
    row = lambda n: pl.BlockSpec((tm, n), lambda i: (i, 0))
    full = lambda a: pl.BlockSpec(a.shape, lambda i: (0,) * a.ndim)
    widths = (256, 128, 256, 256, 512, 128, 256)
    dts = (BF16, F32, F32, F32, F32, F32, F32)
    perm = _chunk_perm()
    cpt = tm // S5_CHUNK
    return pl.pallas_call(
        functools.partial(_inproj_kernel, n_first=n_first), grid=(t // tm,),
        in_specs=_two_group_specs((tm, D_MODEL), n_first)
        + [pl.BlockSpec((TOKEN_SUB, 8, D_MODEL), lambda i: (i, 0, 0)), full(npre),
           pl.BlockSpec((None,) + w.shape[1:], lambda i: (layer, 0, 0)), full(qn), full(kvn), full(perm)],
        out_specs=[pl.BlockSpec((S5_G, cpt, S5_W), lambda i: (0, i, 0))] + [row(n) for n in widths],
        out_shape=[jax.ShapeDtypeStruct((S5_G, t // S5_CHUNK, S5_W), BF16)]
        + [jax.ShapeDtypeStruct((t, n), d) for n, d in zip(widths, dts)],
        compiler_params=_cparams("arbitrary"), name="inproj",
    )(xp, xs, ada_t, npre, w, qn, kvn, perm)


def _s5_kernel(u_ref, r_ref, vf_ref, vb_ref, a_ref, h0f_ref, h0fs_ref, h0b_ref, h0bs_ref, perm_ref, permt_ref,
               y_ref, stf_ref, stb_ref, ucb, sf, sfs, sb, sbs, *, nb, nc):
    gw = 2 * S5_P
    for g in range(S5_G):
        ucb[g] = jnp.dot(perm_ref[...], u_ref[g], preferred_element_type=F32).astype(BF16)
        r = jnp.dot(ucb[g], r_ref[g, :, S5_W:], preferred_element_type=F32)
        sl = slice(g * gw, (g + 1) * gw)
        sf[:, sl] = r[:, 0:128]
        sfs[:, sl] = r[:, 128:256]
        sb[:, sl] = r[:, 256:384]
        sbs[:, sl] = r[:, 384:512]
    arf, aif, aisf = a_ref[0:1], a_ref[1:2], a_ref[2:3]
    arb, aib, aisb = a_ref[3:4], a_ref[4:5], a_ref[5:6]
    cpt = 1 if nb % SUBLANE == 0 else SUBLANE // nb
    rows = cpt * nb
    nt = nc // cpt

    def body(i, carry):
        hf, hfs, hb, hbs = carry
        rf = pl.ds(pl.multiple_of(i * rows, rows), rows)
        rb = pl.ds(pl.multiple_of((nt - 1 - i) * rows, rows), rows)
        inc_f, inc_fs, inc_b, inc_bs = sf[rf, :], sfs[rf, :], sb[rb, :], sbs[rb, :]
        in_f, in_b = [None] * cpt, [None] * cpt
        for k in range(cpt):
            kf = slice(k * nb, (k + 1) * nb)
            kb = slice((cpt - 1 - k) * nb, (cpt - k) * nb)
            in_f[k] = hf
            in_b[cpt - 1 - k] = hb
            hf, hfs = arf * hf + aif * hfs + inc_f[kf], arf * hfs + aisf * hf + inc_fs[kf]
            hb, hbs = arb * hb + aib * hbs + inc_b[kb], arb * hbs + aisb * hb + inc_bs[kb]
        sf[rf, :] = in_f[0] if cpt == 1 else jnp.concatenate(in_f, 0)
        sb[rb, :] = in_b[0] if cpt == 1 else jnp.concatenate(in_b, 0)
        return hf, hfs, hb, hbs

    hf, _, hb, _ = lax.fori_loop(0, nt, body, (h0f_ref[...], h0fs_ref[...], h0b_ref[...], h0bs_ref[...]))
    stf_ref[...] = hf
    stb_ref[...] = hb
    for g in range(S5_G):
        sl = slice(g * gw, (g + 1) * gw)
        y = (jnp.dot(ucb[g], r_ref[g, :, :S5_W], preferred_element_type=F32)
             + _bdot(sf[:, sl], vf_ref[g]) + _bdot(sb[:, sl], vb_ref[g]))
        y_ref[g] = jnp.dot(permt_ref[...], y.astype(BF16), preferred_element_type=F32).astype(BF16)


def _s5_call(u, tabs, h0, nb, n, block):
    r, vf, vb, avec = tabs
    sw = S5_G * 2 * S5_P
    nc = n // nb
    dst = np.arange(n)
    src = (dst % nb) * nc + dst // nb
    perm = np.eye(n, dtype=np.float32)[src]
    consts = (jnp.asarray(perm, BF16), jnp.asarray(perm.T, BF16))
    full = lambda a: pl.BlockSpec(a.shape, lambda i: (0,) * len(a.shape))
    st = jax.ShapeDtypeStruct((nb, sw), F32)
    args = (r, vf, vb, avec, *h0, *consts)
    return pl.pallas_call(
        functools.partial(_s5_kernel, nb=nb, nc=nc), grid=(1,),
        in_specs=[pl.BlockSpec((S5_G, n, S5_W), lambda i: (0, block, 0))] + [full(a) for a in args],
        out_specs=[full(jax.ShapeDtypeStruct((S5_G, n, S5_W), BF16)), full(st), full(st)],
        out_shape=[jax.ShapeDtypeStruct((S5_G, n, S5_W), BF16), st, st],
        scratch_shapes=[pltpu.VMEM((S5_G, n, S5_W), BF16)] + [pltpu.VMEM((n, sw), F32) for _ in range(4)],
        compiler_params=_cparams("arbitrary"), name="s5",
    )(u, *args)


def _cpow(br, bi, e, nbits):
    pr = jnp.ones(e.shape, F32)
    pi = jnp.zeros(e.shape, F32)
    for k in range(nbits):
        bit = ((e >> k) & 1) == 1
        pr, pi = jnp.where(bit, pr * br - pi * bi, pr), jnp.where(bit, pr * bi + pi * br, pi)
        br, bi = br * br - bi * bi, 2.0 * br * bi
    return pr, pi


def _s5tab_kernel(prow_ref, pcol_ref, bt_ref, ct_ref, dsk_ref, etile_ref, r_ref, vf_ref, vb_ref, a_ref):
    q = S5_CHUNK
    hp = lax.Precision.HIGHEST

    def abar(lre, lim, ldt):
        dt = jnp.exp(ldt)
        mag = jnp.exp(lre * dt)
        return mag * jnp.cos(lim * dt), mag * jnp.sin(lim * dt)

    def per_step(x):
        return jnp.dot(x, etile_ref[...], precision=hp, preferred_element_type=F32)

    step_of_col = lax.broadcasted_iota(jnp.int32, (S5_P, S5_W), 1) // S5_GSIZE
    step_of_row = lax.broadcasted_iota(jnp.int32, (S5_W, S5_P), 0) // S5_GSIZE
    kcat, wtabs, arows = [], [], []
    for d in range(2):
        prow = prow_ref[d, 0]
        lre, lim = prow[0:1], prow[1:2]
        abr, abi = abar(lre, lim, prow[2:3])
        den = lre * lre + lim * lim
        fr = ((abr - 1.0) * lre + abi * lim) / den
        fi = (abi * lre - (abr - 1.0) * lim) / den
        btr, bti = bt_ref[d, 0, 0], bt_ref[d, 0, 1]
        bbr, bbi = fr * btr - fi * bti, fr * bti + fi * btr
        pcol = pcol_ref[d, 0]
        cbr, cbi = abar(pcol[:, 0:1], pcol[:, 1:2], pcol[:, 2:3])
        ctr, cti = per_step(ct_ref[d, 0, 0]), per_step(ct_ref[d, 0, 1])

        pr, pi = _cpow(cbr, cbi, step_of_col if d == 0 else q - 1 - step_of_col, 4)
        lr, li = ctr * pr - cti * pi, ctr * pi + cti * pr
        vr, vi = lr * cbr - li * cbi, lr * cbi + li * cbr
        (vf_ref if d == 0 else vb_ref)[0] = jnp.concatenate([vr, -vi], 0).astype(BF16)
        kcat.append(jnp.dot(bbr, lr, precision=hp, preferred_element_type=F32)
                    - jnp.dot(bbi, li, precision=hp, preferred_element_type=F32))
        pr, pi = _cpow(abr, abi, q - 1 - step_of_row if d == 0 else step_of_row, 4)
        tbr, tbi = jnp.concatenate([bbr] * q, 0), jnp.concatenate([bbi] * q, 0)
        wr, wi = pr * tbr - pi * tbi, pr * tbi + pi * tbr
        wtabs += [wr, wi, wi, wr]
        ar, ai = _cpow(abr, abi, jnp.full((1, S5_P), q, jnp.int32), 5)
        arows += [jnp.concatenate([ar, ar], 1), jnp.concatenate([-ai, ai], 1), jnp.concatenate([ai, -ai], 1)]
    lane = lax.broadcasted_iota(jnp.int32, (S5_GSIZE, S5_W), 1)
    rows = []
    for s in range(q):
        shr = S5_GSIZE * s
        shl = S5_GSIZE * (q - 1 - s)
        f = jnp.where(lane >= shr, pltpu.roll(kcat[0], shr, 1), 0.0) if shr else kcat[0]
        b = jnp.where(lane < S5_W - shl, pltpu.roll(kcat[1], S5_W - shl, 1), 0.0) if shl else kcat[1]
        rows.append(f + b)
    ri = lax.broadcasted_iota(jnp.int32, (S5_W, S5_W), 0)
    ci = lax.broadcasted_iota(jnp.int32, (S5_W, S5_W), 1)
    tt = jnp.concatenate(rows, 0) + jnp.where(ri == ci, dsk_ref[0], 0.0)
    r_ref[0] = jnp.concatenate([tt] + wtabs, 1).astype(BF16)
    a_ref[...] = jnp.concatenate(arows + [jnp.zeros((2, 2 * S5_P), F32)], 0)


def _s5_tables(lam_re, lam_im, log_dt, b_re, b_im, c_re, c_im, dskip):
    zeros = jnp.zeros((2, S5_G, 5, S5_P), F32)
    prow = jnp.concatenate([lam_re[:, :, None], lam_im[:, :, None],
                            jnp.broadcast_to(log_dt[:, :, None, None], (2, S5_G, 1, S5_P)), zeros], 2)
    pcol = jnp.swapaxes(prow, 2, 3)
    bt = jnp.stack([jnp.swapaxes(b_re, 2, 3), jnp.swapaxes(b_im, 2, 3)], 2)
    ct = jnp.stack([jnp.swapaxes(c_re, 2, 3), jnp.swapaxes(c_im, 2, 3)], 2)
    dsk = jnp.tile(dskip.reshape(S5_G, 1, S5_GSIZE), (1, 1, S5_CHUNK))
    etile = jnp.asarray(np.tile(np.eye(S5_GSIZE, dtype=np.float32), (1, S5_CHUNK)))
    gblk = lambda a: pl.BlockSpec((2, 1) + a.shape[2:], lambda g: (0, g) + (0,) * (a.ndim - 2))
    sw = S5_G * 2 * S5_P
    return pl.pallas_call(
        _s5tab_kernel, grid=(S5_G,),
        in_specs=[gblk(prow), gblk(pcol), gblk(bt), gblk(ct), pl.BlockSpec((1, 1, S5_W), lambda g: (g, 0, 0)),
                  pl.BlockSpec(etile.shape, lambda g: (0, 0))],
        out_specs=[pl.BlockSpec((1, S5_W, 768), lambda g: (g, 0, 0)), pl.BlockSpec((1, 2 * S5_P, S5_W), lambda g: (g, 0, 0)),
                   pl.BlockSpec((1, 2 * S5_P, S5_W), lambda g: (g, 0, 0)), pl.BlockSpec((8, 2 * S5_P), lambda g: (0, g))],
        out_shape=[jax.ShapeDtypeStruct((S5_G, S5_W, 768), BF16), jax.ShapeDtypeStruct((S5_G, 2 * S5_P, S5_W), BF16),
                   jax.ShapeDtypeStruct((S5_G, 2 * S5_P, S5_W), BF16), jax.ShapeDtypeStruct((8, sw), F32)],
        compiler_params=_cparams("arbitrary"), name="s5tab",
    )(prow, pcol, bt, ct, dsk, etile)


def _s5_state_in(state):
    b = state.shape[0]
    sw = jnp.transpose(state, (0, 1, 2, 4, 3))
    plain = sw.reshape(b, 2, S5_G * 2 * S5_P)
    swapped = sw[:, :, :, ::-1, :].reshape(b, 2, S5_G * 2 * S5_P)
    return plain[:, 0], swapped[:, 0], plain[:, 1], swapped[:, 1]


def _s5_state_out(stf, stb):
    b = stf.shape[0]
    st = jnp.stack([stf, stb], 1).reshape(b, 2, S5_G, 2, S5_P)
    return jnp.transpose(st, (0, 1, 2, 4, 3))


def _mla_kernel(*refs, seq, lctx):
    if lctx:
        (cq_ref, ckv_ref, krab_ref, cosq_ref, sinq_ref, cosk_ref, sink_ref, cckv_ref, ckr_ref,
         wqa_ref, wqb_ref, wk_ref, wv_ref, o_ref, k_scr, v_scr, s_scr) = refs
    else:
        (cq_ref, ckv_ref, krab_ref, cosq_ref, sinq_ref, cosk_ref, sink_ref,
         wqa_ref, wqb_ref, wk_ref, wv_ref, o_ref, k_scr, v_scr, s_scr) = refs
    lk = lctx + seq
    rb = min(seq, 512)

    @pl.when(pl.program_id(1) == 0)
    def _build_keys():
        if lctx:
            ckv = cckv_ref[0]
            kn = _bdot(ckv, wk_ref[...])
            vn = _bdot(ckv, wv_ref[...])
            for h in range(MLA_H):
                k_scr[h, 0:lctx, :] = (kn[:, h * LANE:(h + 1) * LANE] + ckr_ref[0]).astype(BF16)
                v_scr[h, 0:lctx, :] = vn[:, h * LANE:(h + 1) * LANE].astype(BF16)

        def chunk(i, _):
            r0 = pl.multiple_of(i * rb, rb)
            rows = pl.ds(r0, rb)
            ckv = ckv_ref[rows, :]
            kn = _bdot(ckv, wk_ref[...])
            vn = _bdot(ckv, wv_ref[...])
            krab = krab_ref[rows, :]
            krx = krab[:, :LANE] * cosk_ref[rows, :] + krab[:, LANE:] * sink_ref[rows, :]
            dst = pl.ds(pl.multiple_of(lctx + r0, rb if lctx % rb == 0 else 256), rb)
            for h in range(MLA_H):
                k_scr[h, dst, :] = (kn[:, h * LANE:(h + 1) * LANE] + krx).astype(BF16)
                v_scr[h, dst, :] = vn[:, h * LANE:(h + 1) * LANE].astype(BF16)
            return 0

        lax.fori_loop(0, seq // rb, chunk, 0)

    cq = cq_ref[...]
    qa = _bdot(cq, wqa_ref[...])
    qb = _bdot(cq, wqb_ref[...])
    cosq, sinq = cosq_ref[...], sinq_ref[...]
    scale = 1.0 / math.sqrt(MLA_NOPE + MLA_ROPE)
    outs = []
    chunks = [slice(j * MLA_KC, (j + 1) * MLA_KC) for j in range(lk // MLA_KC)]
    for h in range(MLA_H):
        hs = slice(h * LANE, (h + 1) * LANE)
        qh = ((qa[:, hs] * cosq + qb[:, hs] * sinq) * scale).astype(BF16)
        mp = None
        for ks in chunks:
            s = _bdot_nt(qh, k_scr[h, ks, :])
            s_scr[:, ks] = s
            for t in range(MLA_KC // LANE):
                part = s[:, t * LANE:(t + 1) * LANE]
                mp = part if mp is None else jnp.maximum(mp, part)
        m = jnp.max(mp, axis=-1, keepdims=True)
        lp = jnp.zeros((MLA_TQ, LANE), F32)
        acc = jnp.zeros((MLA_TQ, LANE), F32)
        for ks in chunks:
            e = jnp.exp(s_scr[:, ks] - m)
            for t in range(MLA_KC // LANE):
                lp = lp + e[:, t * LANE:(t + 1) * LANE]
            acc = acc + _bdot(e, v_scr[h, ks, :])
        outs.append((acc / jnp.sum(lp, axis=-1, keepdims=True))[:, :MLA_V])
    o_ref[...] = jnp.concatenate(outs, axis=-1)


def _mla_call(cq, ckv, krab, rope, ctx, w, nb, seq, row0):
    lctx = 0 if ctx is None else ctx[0].shape[1]
    nq = seq // MLA_TQ
    q0, s0 = row0 // MLA_TQ, row0 // seq
    cosx, sinx = rope
    qrow = lambda n: pl.BlockSpec((MLA_TQ, n), lambda b, i: (q0 + b * nq + i, 0))
    srow = lambda n: pl.BlockSpec((seq, n), lambda b, i: (s0 + b, 0))
    full = lambda a: pl.BlockSpec(a.shape, lambda b, i: (0,) * a.ndim)
    in_specs = [qrow(256), srow(128), srow(256), pl.BlockSpec((MLA_TQ, LANE), lambda b, i: (i, 0)),
                pl.BlockSpec((MLA_TQ, LANE), lambda b, i: (i, 0)), full(cosx), full(sinx)]
    args = [cq, ckv, krab, cosx, sinx, cosx, sinx]
    if lctx:
        in_specs += [pl.BlockSpec((1, lctx, LANE), lambda b, i: (b, 0, 0))] * 2
        args += list(ctx)
    in_specs += [full(a) for a in w]
    args += list(w)
    lk = lctx + seq
    return pl.pallas_call(
        functools.partial(_mla_kernel, seq=seq, lctx=lctx), grid=(nb, nq),
        in_specs=in_specs,
        out_specs=pl.BlockSpec((MLA_TQ, 256), lambda b, i: (b * nq + i, 0)),
        out_shape=jax.ShapeDtypeStruct((nb * seq, 256), F32),
        scratch_shapes=[pltpu.VMEM((MLA_H, lk, LANE), BF16), pltpu.VMEM((MLA_H, lk, LANE), BF16),
                        pltpu.VMEM((MLA_TQ, lk), F32)],
        compiler_params=_cparams("arbitrary", "arbitrary"), name="mla",
    )(*args)


def _swap8(w):
    q = MLA_ROPE // 4
    return jnp.concatenate([w[..., j * q:(j + 1) * q] for j in (1, 0, 3, 2)], -1)


def _mla_weights(wuq, wuk, wuv):
    z32 = jnp.zeros((MLA_QLORA, 32), F32)
    z64 = jnp.zeros((MLA_QLORA, 64), F32)
    k64 = jnp.zeros((MLA_KVLORA, 64), F32)
    qa, qb, wk, wv = [], [], [], []
    for h in range(MLA_H):
        rope_w = wuq[:, h, MLA_NOPE:]
        qa += [wuq[:, h, :MLA_NOPE], rope_w, z32]
        qb += [z64, _swap8(rope_w), z32]
        wk += [wuk[:, h], k64]
        wv += [wuv[:, h], k64]
    cat = lambda xs: jnp.concatenate(xs, -1).astype(BF16)
    return cat(qa), cat(qb), cat(wk), cat(wv)


def _rope_tables(seq, rotate):
    ones = np.ones((seq, MLA_NOPE))
    z32 = np.zeros((seq, 32))
    if not rotate:
        cosx, sinx = np.concatenate([ones, np.ones((seq, MLA_ROPE)), z32], -1), np.zeros((seq, LANE))
    else:
        pos = np.arange(seq)
        row = (pos // GRID_W).astype(np.float64)
        col = (pos % GRID_W).astype(np.float64)
        half = MLA_ROPE // 2
        inv = ROPE_BASE ** (-np.arange(0, half, 2, dtype=np.float64) / half)
        ang = np.concatenate([row[:, None] * inv, col[:, None] * inv], -1)
        cos, sin = np.cos(ang), np.sin(ang)
        q = MLA_ROPE // 4
        cos32 = np.concatenate([cos[:, :q], cos[:, :q], cos[:, q:], cos[:, q:]], -1)
        sin32 = np.concatenate([-sin[:, :q], sin[:, :q], -sin[:, q:], sin[:, q:]], -1)
        cosx, sinx = np.concatenate([ones, cos32, z32], -1), np.concatenate([0.0 * ones, sin32, z32], -1)
    return jnp.asarray(cosx, F32), jnp.asarray(sinx, F32)


def _cumsum_rows(x, reverse):
    n = x.shape[0]
    row = lax.broadcasted_iota(jnp.int32, x.shape, 0)
    s = 1
    while s < n:
        if reverse:
            x = x + jnp.where(row < n - s, pltpu.roll(x, n - s, 0), 0.0)
        else:
            x = x + jnp.where(row >= s, pltpu.roll(x, s, 0), 0.0)
        s *= 2
    return x


def _ssd_kernel(xbc_ref, z_ref, dt_ref, cw_ref, cb_ref, par_ref, dsk_ref, nw_ref, h0_ref,
                y_ref, st_ref, xc_scr, xt_scr, y_scr, hf_scr, hb_scr, *, seq, ns):
    q = SSD_CHUNK
    nc = seq // q
    hp = SSD_H * SSD_P
    row = lax.broadcasted_iota(jnp.int32, (q, 1), 0)

    def conv(c, _):
        t0 = pl.multiple_of(c * q, q)
        cur = xbc_ref[pl.ds(t0, q), :]
        prev8 = xbc_ref[pl.ds(pl.multiple_of(jnp.maximum(t0 - 8, 0), 8), 8), :]
        next8 = xbc_ref[pl.ds(pl.multiple_of(jnp.minimum(t0 + q, ns * seq - 8), 8), 8), :]
        c_in = lax.rem(c, nc)
        prow = jnp.where(c_in > 0, prev8[7:8], 0.0)
        nrow = jnp.where(c_in < nc - 1, next8[0:1], 0.0)
        up = jnp.where(row == 0, prow, pltpu.roll(cur, 1, 0))
        dn = jnp.where(row == q - 1, nrow, pltpu.roll(cur, q - 1, 0))
        acc = up * cw_ref[0:1] + cur * cw_ref[1:2] + dn * cw_ref[2:3] + cb_ref[...]
        xc = jax.nn.silu(acc)
        xc_scr[pl.ds(t0, q), :] = xc
        xt_scr[c] = jnp.transpose(xc[:, :SSD_W])
        y_scr[pl.ds(t0, q), :] = jnp.zeros((q, SSD_W), F32)
        return 0

    lax.fori_loop(0, ns * nc, conv, 0)

    pair_rows = (SSD_H // SSD_G) * SSD_P
    for d, h_scr in enumerate((hf_scr, hb_scr)):
        h_scr[...] = jnp.zeros_like(h_scr)
        for s in range(ns):
            for g in range(SSD_G):
                h_scr[s * hp + g * pair_rows:s * hp + (g + 1) * pair_rows, g * SSD_N:(g + 1) * SSD_N] = (
                    h0_ref[s, d, g * pair_rows:(g + 1) * pair_rows, :])
    ii = lax.broadcasted_iota(jnp.int32, (q, q), 0)
    jj = lax.broadcasted_iota(jnp.int32, (q, q), 1)
    first_half = lax.broadcasted_iota(jnp.int32, (q, LANE), 1) < SSD_N

    def one_chunk(c, d, h_scr, base):
        t0 = pl.multiple_of(c * q, q)
        xc = xc_scr[pl.ds(t0, q), :]
        xs_t = xt_scr[c]
        dt = jax.nn.softplus(dt_ref[pl.ds(t0, q), :] + par_ref[d:d + 1])
        cs = _cumsum_rows(dt * par_ref[2 + d:3 + d], reverse=(d == 1))
        edge = cs[q - 1:q] if d == 0 else cs[0:1]
        cs_t = jnp.transpose(cs)
        dt_t = jnp.transpose(dt)
        wts_t = jnp.transpose(jnp.exp(edge - cs) * dt)
        decay = jnp.exp(edge)
        mask = (ii >= jj) if d == 0 else (ii <= jj)
        bm_pair = xc[:, SSD_W:SSD_W + LANE]
        cm_pair = xc[:, SSD_W + LANE:SSD_W + 2 * LANE]
        for g in range(SSD_G):
            in_g = first_half if g == 0 else jnp.logical_not(first_half)
            pair = slice(g * LANE, (g + 1) * LANE)
            pair_state = slice(base + g * pair_rows, base + (g + 1) * pair_rows)
            heads = [g * (SSD_H // SSD_G) + hh for hh in range(SSD_H // SSD_G)]
            cm = jnp.where(in_g, cm_pair, 0.0)
            bm = jnp.where(in_g, bm_pair, 0.0)
            cbm = _bdot_nt(cm, bm_pair)
            cs_i = [jnp.transpose(jnp.broadcast_to(cs_t[h:h + 1, :], (q, q))) for h in heads]
            y_off = _bdot_nt(cm, h_scr[pair_state, :]) * jnp.exp(jnp.where(first_half, cs_i[0], cs_i[1]))
            ys = []
            for hh, h in enumerate(heads):
                lmat = jnp.exp(jnp.where(mask, cs_i[hh] - cs_t[h:h + 1, :], -jnp.inf))
                ys.append(_bdot(cbm * lmat * dt_t[h:h + 1, :], xc[:, pair]))
                ps = slice(h * SSD_P, (h + 1) * SSD_P)
                st = _bdot(xs_t[ps, :] * wts_t[h:h + 1, :], bm)
                hs = slice(base + h * SSD_P, base + (h + 1) * SSD_P)
                h_scr[hs, :] = decay[:, h:h + 1] * h_scr[hs, :] + st
            y_scr[pl.ds(t0, q), pair] += jnp.where(first_half, ys[0], ys[1]) + y_off

    def chunks(c, _):
        for s in range(ns):
            one_chunk(s * nc + c, 0, hf_scr, s * hp)
            one_chunk(s * nc + nc - 1 - c, 1, hb_scr, s * hp)
        return 0

    lax.fori_loop(0, nc, chunks, 0, unroll=min(nc, max(1, 4 // ns)))
    for d, h_scr in enumerate((hf_scr, hb_scr)):
        for s in range(ns):
            for g in range(SSD_G):
                st_ref[s, d, g * pair_rows:(g + 1) * pair_rows, :] = (
                    h_scr[s * hp + g * pair_rows:s * hp + (g + 1) * pair_rows, g * SSD_N:(g + 1) * SSD_N])

    def finish(c, _):
        rows = pl.ds(pl.multiple_of(c * q, q), q)
        y = y_scr[rows, :] + dsk_ref[...] * xc_scr[rows, 0:SSD_W]
        y_ref[rows, :] = _rms(y * jax.nn.silu(z_ref[rows, :]), nw_ref[...])
        return 0

    lax.fori_loop(0, ns * nc, finish, 0)


def _ssd_call(xbc, z, dt, w, h0, nb, seq, row0):
    ns = max(1, min(nb, SSD_STEP_ROWS // seq))
    rows = ns * seq
    s0 = row0 // rows
    srow = lambda n: pl.BlockSpec((rows, n), lambda b: (s0 + b, 0))
    full = lambda a: pl.BlockSpec(a.shape, lambda b: (0,) * a.ndim)
    hp = SSD_H * SSD_P
    st_spec = pl.BlockSpec((ns, 2, hp, SSD_N), lambda b: (b, 0, 0, 0))
    return pl.pallas_call(
        functools.partial(_ssd_kernel, seq=seq, ns=ns), grid=(nb // ns,),
        in_specs=[srow(SSD_XBC), srow(SSD_W), srow(LANE)] + [full(a) for a in w] + [st_spec],
        out_specs=[pl.BlockSpec((rows, SSD_W), lambda b: (b, 0)), st_spec],
        out_shape=[jax.ShapeDtypeStruct((nb * seq, SSD_W), F32), jax.ShapeDtypeStruct((nb, 2, hp, SSD_N), F32)],
        scratch_shapes=[pltpu.VMEM((rows, SSD_XBC), F32), pltpu.VMEM((rows // SSD_CHUNK, SSD_W, SSD_CHUNK), F32),
                        pltpu.VMEM((rows, SSD_W), F32), pltpu.VMEM((ns * hp, LANE), F32),
                        pltpu.VMEM((ns * hp, LANE), F32)],
        compiler_params=_cparams("arbitrary"), name="ssd",
    )(xbc, z, dt, *w, h0)


def _ssd_weights(conv_w, conv_b, dt_bias, a_log, dskip, norm_w):
    pad = lambda x: jnp.pad(x, ((0, 0), (0, LANE - SSD_H)))
    par = jnp.concatenate([pad(dt_bias), pad(-jnp.exp(a_log)), jnp.zeros((4, LANE), F32)], 0)
    cw = jnp.concatenate([conv_w, jnp.zeros((5, SSD_XBC), F32)], 0)
    return (cw, conv_b.reshape(1, SSD_XBC), par, jnp.repeat(dskip, SSD_P).reshape(1, SSD_W),
            norm_w.reshape(1, SSD_W))


def _conf_kernel(v_ref, w_ref, b_ref, g_ref, bt_ref, o_ref, pad_scr, sh_scr, *, seq, seg):
    q = CONF_Q
    halo = 2 * CONF_PAD
    tail = pad_scr.shape[0] - seq - CONF_PAD
    pad_scr[0:CONF_PAD, :] = jnp.zeros((CONF_PAD, CONF_W), F32)
    pad_scr[seq + CONF_PAD:, :] = jnp.zeros((tail, CONF_W), F32)
    pad_scr[CONF_PAD:seq + CONF_PAD, :] = v_ref[...]
    lo = CONF_PAD - CONF_K // 2
    pieces = [(j, min(LANE, seg + halo - j)) for j in range(0, seg + halo, LANE)]

    def segment(s, _):
        s0 = pl.multiple_of(s * seg, seg)
        for j0, n in pieces:
            big = pad_scr[pl.ds(s0 + j0, n + SUBLANE), :]
            for r in range(SUBLANE):
                sh_scr[r, j0:j0 + n, :] = big[r:r + n]

        def chunk(c, _):
            t0 = pl.multiple_of(c * q, q)
            acc = jnp.zeros((q, CONF_W), F32) + b_ref[...]
            for k in range(CONF_K):
                r, off = (lo + k) % SUBLANE, (lo + k) // SUBLANE * SUBLANE
                wk = jnp.concatenate([w_ref[k * SUBLANE:(k + 1) * SUBLANE, :]] * (q // SUBLANE), 0)
                acc = acc + sh_scr[r, pl.ds(pl.multiple_of(t0 + off, SUBLANE), q), :] * wk
            mu = jnp.mean(acc, axis=-1, keepdims=True)
            xc = acc - mu
            var = jnp.mean(xc * xc, axis=-1, keepdims=True)
            o_ref[pl.ds(s0 + t0, q), :] = jax.nn.silu(xc * lax.rsqrt(var + EPS) * g_ref[...] + bt_ref[...])
            return 0

        lax.fori_loop(0, seg // q, chunk, 0, unroll=2)
        return 0

    lax.fori_loop(0, seq // seg, segment, 0)


def _conf_call(v, w, nb, seq, row0):
    s0 = row0 // seq
    full = lambda a: pl.BlockSpec(a.shape, lambda b: (0,) * a.ndim)
    seg = min(seq, CONF_SEG)
    halo = 2 * CONF_PAD
    return pl.pallas_call(
        functools.partial(_conf_kernel, seq=seq, seg=seg), grid=(nb,),
        in_specs=[pl.BlockSpec((seq, CONF_W), lambda b: (s0 + b, 0))] + [full(a) for a in w],
        out_specs=pl.BlockSpec((seq, CONF_W), lambda b: (b, 0)),
        out_shape=jax.ShapeDtypeStruct((nb * seq, CONF_W), F32),
        scratch_shapes=[pltpu.VMEM((seq + halo + 2 * SUBLANE, CONF_W), F32),
                        pltpu.VMEM((SUBLANE, seg + halo, CONF_W), F32)],
        compiler_params=_cparams("arbitrary"), name="conf",
    )(v, *w)


def _split_bf16(x):
    hi = x.astype(BF16)
    return hi, (x - hi.astype(F32)).astype(BF16)


def _outproj_kernel(xp_ref, xs_ref, ada_ref, s5p_ref, s5s_ref, mlap_ref, mlas_ref, ssdp_ref, ssds_ref, confp_ref,
                    confs_ref, wglu_ref, bglu_ref, wo_ref, npost_ref, npre_ref, wrh_ref, wrl_ref, br_ref, perm_ref,
                    x1_ref, h2_ref, gate_ref, *, n_first):
    for s in range(TOKEN_SUB):
        _outproj_tile(s, xp_ref, xs_ref, ada_ref, s5p_ref, s5s_ref, mlap_ref, mlas_ref, ssdp_ref, ssds_ref, confp_ref,
                      confs_ref, wglu_ref, bglu_ref, wo_ref, npost_ref, npre_ref, wrh_ref, wrl_ref, br_ref, perm_ref,
                      x1_ref, h2_ref, gate_ref, n_first)


def _outproj_tile(s, xp_ref, xs_ref, ada_ref, s5p_ref, s5s_ref, mlap_ref, mlas_ref, ssdp_ref, ssds_ref, confp_ref,
                  confs_ref, wglu_ref, bglu_ref, wo_ref, npost_ref, npre_ref, wrh_ref, wrl_ref, br_ref, perm_ref,
                  x1_ref, h2_ref, gate_ref, n_first):
    rows = slice(s * TM, (s + 1) * TM)
    cpt = TM // S5_CHUNK
    is_first = pl.program_id(0) < n_first
    pick = lambda first, second: jnp.where(is_first, first[rows, :], second[rows, :])
    ada = ada_ref[s]
    ys5 = jnp.where(is_first, s5p_ref[:, s * cpt:(s + 1) * cpt, :], s5s_ref[:, s * cpt:(s + 1) * cpt, :])
    a = jax.nn.gelu(_group_rows_to_chunk_rows([ys5[g].astype(F32) for g in range(S5_G)], perm_ref[...]))
    s5 = a * jax.nn.sigmoid(_bdot(a, wglu_ref[...]) + bglu_ref[...])
    mix = (_bdot(s5, wo_ref[0:256, :]) + _bdot(pick(mlap_ref, mlas_ref), wo_ref[256:512, :])
           + _bdot(pick(ssdp_ref, ssds_ref), wo_ref[512:768, :]) + _bdot(pick(confp_ref, confs_ref), wo_ref[768:1024, :]))
    x1 = pick(xp_ref, xs_ref) + _rms(mix, npost_ref[...] * ada[2:3])
    x1_ref[rows, :] = x1
    h2 = _rms(x1, npre_ref[...] * (1.0 + ada[4:5])) + ada[3:4]
    h2_ref[rows, :] = h2.astype(BF16)
    hi, lo = _split_bf16(h2)
    lg = (jnp.dot(hi, wrh_ref[...], preferred_element_type=F32) + jnp.dot(lo, wrh_ref[...], preferred_element_type=F32)
          + jnp.dot(hi, wrl_ref[...], preferred_element_type=F32) + br_ref[...])
    lane = lax.broadcasted_iota(jnp.int32, lg.shape, 1)
    neg = -jnp.inf
    big = jnp.int32(1 << 20)
    first = lambda hit: jnp.min(jnp.where(hit, lane, big), axis=-1, keepdims=True)
    glm = jnp.where((lane >= MOE_E) & (lane < MOE_E + MOE_GROUPS), lg, neg)
    gmax = jnp.max(glm, axis=-1, keepdims=True)
    p_group = 1.0 / jnp.sum(jnp.exp(glm - gmax), axis=-1, keepdims=True)
    gsel = first(glm == gmax) - MOE_E
    elm = jnp.where((lane < MOE_E) & ((lane // MOE_PER_GROUP) == gsel), lg, neg)
    v1 = jnp.max(elm, axis=-1, keepdims=True)
    i1 = first(elm == v1)
    elm2 = jnp.where(lane == i1, neg, elm)
    v2 = jnp.max(elm2, axis=-1, keepdims=True)
    i2 = first(elm2 == v2)
    e2 = jnp.exp(v2 - v1)
    w1 = p_group / (1.0 + e2)
    w2 = p_group * e2 / (1.0 + e2)
    gate_ref[rows, :] = jnp.where(lane == i1, w1, 0.0) + jnp.where(lane == i2, w2, 0.0)


def _outproj_call(x, ada_t, ys5, ymla, yssd, yconf, w):
    t = x[0].shape[0] + x[1].shape[0]
    tm = TOKEN_SUB * TM
    n_first = x[0].shape[0] // tm
    row = lambda n: pl.BlockSpec((tm, n), lambda i: (i, 0))
    full = lambda a: pl.BlockSpec(a.shape, lambda i: (0,) * a.ndim)
    w = tuple(w) + (_chunk_perm(),)
    pair = lambda n: _two_group_specs((tm, n), n_first)
    return pl.pallas_call(
        functools.partial(_outproj_kernel, n_first=n_first), grid=(t // tm,),
        in_specs=pair(D_MODEL) + [pl.BlockSpec((TOKEN_SUB, 8, D_MODEL), lambda i: (i, 0, 0))]
        + _two_group_specs((tm // S5_CHUNK, S5_W), n_first, lead=(S5_G,)) + pair(256) + pair(256) + pair(256)
        + [full(a) for a in w],
        out_specs=[row(D_MODEL), row(D_MODEL), row(LANE)],
        out_shape=[jax.ShapeDtypeStruct((t, D_MODEL), F32), jax.ShapeDtypeStruct((t, D_MODEL), BF16),
                   jax.ShapeDtypeStruct((t, LANE), F32)],
        compiler_params=_cparams("arbitrary"), name="outproj",
    )(*x, ada_t, *ys5, *ymla, *yssd, *yconf, *w)


def _moe_kernel(h_ref, gate_ref, x1_ref, ada_ref, wg_ref, wu_ref, wd_ref, npost_ref, op_ref, os_ref, hid_scr,
                *, n_first):
    i = pl.program_id(0)
    h = h_ref[...]
    gate = gate_ref[...]
    lane = lax.broadcasted_iota(jnp.int32, gate.shape, 1)
    for e in range(MOE_E):
        ge = jnp.sum(jnp.where(lane == e, gate, 0.0), axis=-1, keepdims=True)
        hid = (jax.nn.silu(jnp.dot(h, wg_ref[e], preferred_element_type=F32))
               * jnp.dot(h, wu_ref[e], preferred_element_type=F32) * ge)
        hid_scr[:, e * MOE_HID:(e + 1) * MOE_HID] = hid.astype(BF16)
    out = jnp.dot(hid_scr[...], wd_ref[...], preferred_element_type=F32)
    res = x1_ref[...] + _rms(out, npost_ref[...] * ada_ref[0][5:6])

    @pl.when(i < n_first)
    def _store_first():
        op_ref[...] = res

    @pl.when(i >= n_first)
    def _store_second():
        os_ref[...] = res


def _moe_call(h2, gate, x1, ada_t, wg, wu, wd, layer, npost, t_first):
    t = h2.shape[0]
    tm = MOE_TM
    n_first = t_first // tm
    row = lambda n: pl.BlockSpec((tm, n), lambda i: (i, 0))
    resident = lambda a: pl.BlockSpec((None,) + a.shape[1:], lambda i: (layer,) + (0,) * (a.ndim - 1),
                                      pipeline_mode=pl.Buffered(1))
    wd = wd.reshape(wd.shape[0], MOE_E * MOE_HID, D_MODEL)
    return pl.pallas_call(
        functools.partial(_moe_kernel, n_first=n_first), grid=(t // tm,),
        in_specs=[row(D_MODEL), row(LANE), row(D_MODEL),
                  pl.BlockSpec((1, 8, D_MODEL), lambda i: (i * (tm // TM), 0, 0)),
                  resident(wg), resident(wu), resident(wd), pl.BlockSpec(npost.shape, lambda i: (0, 0))],
        out_specs=_two_group_specs((tm, D_MODEL), n_first),
        out_shape=[jax.ShapeDtypeStruct((t_first, D_MODEL), F32), jax.ShapeDtypeStruct((t - t_first, D_MODEL), F32)],
        scratch_shapes=[pltpu.VMEM((tm, MOE_E * MOE_HID), BF16)],
        compiler_params=_cparams("arbitrary"), name="moe",
    )(h2, gate, x1, ada_t, wg, wu, wd, npost)


def _pack_w_in(w):
    lead = w.shape[:-1]
    z32, z64 = jnp.zeros(lead + (32,), F32), jnp.zeros(lead + (64,), F32)
    kr = w[..., OFF_MLA_KR:OFF_SSD_Z]
    zdt = jnp.zeros(lead + (LANE - SSD_H,), F32)
    cols = [w[..., OFF_S5:OFF_MLA_KR], z64, kr, z32, z64, _swap8(kr), z32,
            w[..., OFF_SSD_Z:OFF_SSD_DT], w[..., OFF_SSD_DT:OFF_CONF], zdt, w[..., OFF_CONF:IN_COLS]]
    return jnp.concatenate(cols, -1).astype(BF16)


def _row(v):
    return v.reshape(1, -1)


def kernel(x_prompt, x_sample, cache_mla_ckv, cache_mla_krope, state_s5, state_ssd, c, c_ctx, w_ada, b_ada, norm_pre1, norm_post1, norm_pre2, norm_post2, w_in, w_out, s5_lam_re, s5_lam_im, s5_log_dt, s5_b_re, s5_b_im, s5_c_re, s5_c_im, s5_d, s5_w_glu, s5_b_glu, mla_qnorm, mla_kvnorm, mla_wuq, mla_wuk, mla_wuv, ssd_conv_w, ssd_conv_b, ssd_dt_bias, ssd_a_log, ssd_d, ssd_norm_w, conf_dw_w, conf_dw_b, conf_ln_g, conf_ln_b, moe_wg, moe_bg, moe_we, moe_be, moe_w_gate, moe_w_up, moe_w_down):
    bp, lp, d = x_prompt.shape
    bs, ls, _ = x_sample.shape
    tp, ts = bp * lp, bs * ls
    x = (x_prompt.reshape(tp, d), x_sample.reshape(ts, d))

    crows = jnp.concatenate([c_ctx[None], c, jnp.zeros((8 - 1 - bs, d), F32)], 0)
    ada = _ada_call(crows, w_ada, b_ada)
    tile_row = np.concatenate([np.zeros(tp // TM, np.int32), 1 + np.repeat(np.arange(bs, dtype=np.int32), ls // TM)])

    rope_p = _rope_tables(lp, rotate=False)
    rope_s = _rope_tables(ls, rotate=True)
    zeros_s5 = tuple(jnp.zeros((bp, S5_G * 2 * S5_P), F32) for _ in range(4))
    zeros_ssd = jnp.zeros((bp, 2, SSD_H * SSD_P, SSD_N), F32)
    pad_kr = lambda kr: jnp.pad(kr, ((0, 0), (0, 0), (MLA_NOPE, LANE - MLA_NOPE - MLA_ROPE)))

    w_in_packed = _pack_w_in(w_in)
    moe_w = [w.astype(BF16) for w in (moe_w_gate, moe_w_up, moe_w_down)]

    ckv_l, kr_l, s5_l, ssd_l = [], [], [], []
    for l in range(DEPTH):
        ada_t = jnp.pad(ada[l][tile_row].reshape(-1, 6, d), ((0, 0), (0, 2), (0, 0)))
        u, cq, ckv, krab, z, xbc, dt, v = _inproj_call(
            *x, ada_t, _row(norm_pre1[l]), w_in_packed, l, _row(mla_qnorm[l]), _row(mla_kvnorm[l]))

        tabs = _s5_tables(s5_lam_re[l], s5_lam_im[l], s5_log_dt[l], s5_b_re[l], s5_b_im[l], s5_c_re[l], s5_c_im[l],
                          s5_d[l])
        ncp = tp // S5_CHUNK
        ncs = ts // S5_CHUNK
        yp, stf, stb = _s5_call(u, tabs, zeros_s5, bp, ncp, 0)
        ys, _, _ = _s5_call(u, tabs, _s5_state_in(state_s5[:, l]), bs, ncs, ncp // ncs)
        y_s5 = (yp, ys)
        s5_l.append(_s5_state_out(stf, stb))

        mw = _mla_weights(mla_wuq[l], mla_wuk[l], mla_wuv[l])
        y_mla = (_mla_call(cq, ckv, krab, rope_p, None, mw, bp, lp, 0),
                 _mla_call(cq, ckv, krab, rope_s, (cache_mla_ckv[:, l], pad_kr(cache_mla_krope[:, l])), mw, bs, ls, tp))
        ckv_l.append(ckv[:tp].reshape(bp, lp, MLA_KVLORA))
        kr_l.append(krab[:tp, MLA_NOPE:MLA_NOPE + MLA_ROPE].reshape(bp, lp, MLA_ROPE))

        sw = _ssd_weights(ssd_conv_w[l], ssd_conv_b[l], ssd_dt_bias[l], ssd_a_log[l], ssd_d[l], ssd_norm_w[l])
        yssd_p, st_ssd = _ssd_call(xbc, z, dt, sw, zeros_ssd, bp, lp, 0)
        yssd_s, _ = _ssd_call(xbc, z, dt, sw, state_ssd[:, l].reshape(bs, 2, SSD_H * SSD_P, SSD_N), bs, ls, tp)
        ssd_l.append(st_ssd.reshape(bp, 2, SSD_H, SSD_P, SSD_N))

        cw = (jnp.repeat(conf_dw_w[l], SUBLANE, axis=0), _row(conf_dw_b[l]),
              _row(conf_ln_g[l]), _row(conf_ln_b[l]))
        y_conf = (_conf_call(v, cw, bp, lp, 0), _conf_call(v, cw, bs, ls, tp))

        wr = jnp.pad(jnp.concatenate([moe_we[l], moe_wg[l]], -1), ((0, 0), (0, LANE - MOE_E - MOE_GROUPS)))
        wrh, wrl = _split_bf16(wr)
        br = jnp.pad(jnp.concatenate([moe_be[l], moe_bg[l]]), (0, LANE - MOE_E - MOE_GROUPS)).reshape(1, LANE)
        ow = (s5_w_glu[l].astype(BF16), _row(s5_b_glu[l]), w_out[l].astype(BF16), _row(norm_post1[l]),
              _row(norm_pre2[l]), wrh, wrl, br)
        x1, h2, gate = _outproj_call(x, ada_t, y_s5, y_mla, (yssd_p, yssd_s), y_conf, ow)

        x = _moe_call(h2, gate, x1, ada_t, *moe_w, l, _row(norm_post2[l]), tp)

    return (x[0].reshape(bp, lp, d), x[1].reshape(bs, ls, d),
            jnp.stack(ckv_l, 1), jnp.stack(kr_l, 1), jnp.stack(s5_l, 1), jnp.stack(ssd_l, 1))
```

```python
import functools
import math

import jax
import jax.numpy as jnp
import numpy as np
from jax import lax
from jax.experimental import pallas as pl
from jax.experimental.pallas import tpu as pltpu

F32 = jnp.float32
BF16 = jnp.bfloat16

D_MODEL = 1024
DEPTH = 2
GRID_W = 64
EPS = 1e-6

S5_W = 256
S5_GSIZE = 16
S5_G = 16
S5_P = 64
S5_CHUNK = 16

MLA_H = 4
MLA_NOPE = 64
MLA_ROPE = 32
MLA_V = 64
MLA_QLORA = 256
MLA_KVLORA = 128
ROPE_BASE = 10000.0
MLA_TQ = 256
MLA_KC = 256

SSD_W = 256
SSD_P = 64
SSD_H = 4
SSD_G = 2
SSD_N = 64
SSD_CHUNK = 128
SSD_XBC = 512
SSD_STEP_ROWS = 512

CONF_W = 256
CONF_K = 31
CONF_Q = 128
CONF_PAD = 16
CONF_SEG = 1024

MOE_GROUPS = 4
MOE_PER_GROUP = 4
MOE_E = 16
MOE_HID = 256
MOE_TM = 512
MOE_SUB = 2

OFF_S5 = 0
OFF_MLA_Q = 256
OFF_MLA_KV = 512
OFF_MLA_KR = 640
OFF_SSD_Z = 672
OFF_SSD_XBC = 928
OFF_SSD_DT = 1440
OFF_CONF = 1444
IN_COLS = 1956

LANE = 128
SUBLANE = 8
TM = 256
TOKEN_SUB = 4
P_U, P_Q, P_KV, P_KRA, P_KRB, P_Z, P_XBC, P_DT, P_CA, P_CB, P_END = (
    0, 256, 512, 640, 768, 896, 1152, 1664, 1792, 2048, 2304)

VMEM_LIMIT = 56 * 1024 * 1024


def _bdot(a, b):
    return jnp.dot(a.astype(BF16), b.astype(BF16), preferred_element_type=F32)


def _bdot_nt(a, b):
    return lax.dot_general(a.astype(BF16), b.astype(BF16), (((1,), (1,)), ((), ())), preferred_element_type=F32)


def _rms(x, g):
    return x * lax.rsqrt(jnp.mean(x * x, axis=-1, keepdims=True) + EPS) * g


def _cparams(*sem):
    return pltpu.CompilerParams(dimension_semantics=sem if sem else None, vmem_limit_bytes=VMEM_LIMIT)


def _ada_kernel(c_ref, w_ref, b_ref, o_ref):
    o_ref[0] = _bdot(jax.nn.silu(c_ref[...]), w_ref[0]) + b_ref[0]


def _ada_call(crows, w_ada, b_ada):
    tn = 1024
    d6 = 6 * D_MODEL
    return pl.pallas_call(
        _ada_kernel, grid=(DEPTH, d6 // tn),
        in_specs=[pl.BlockSpec((8, D_MODEL), lambda l, j: (0, 0)),
                  pl.BlockSpec((1, D_MODEL, tn), lambda l, j: (l, 0, j)),
                  pl.BlockSpec((1, 1, tn), lambda l, j: (l, 0, j))],
        out_specs=pl.BlockSpec((1, 8, tn), lambda l, j: (l, 0, j)),
        out_shape=jax.ShapeDtypeStruct((DEPTH, 8, d6), F32),
        compiler_params=_cparams("arbitrary", "arbitrary"), name="ada",
    )(crows, w_ada, b_ada.reshape(DEPTH, 1, d6))


def _lane_block(shape):
    return lax.broadcasted_iota(jnp.int32, shape, 1) // S5_GSIZE


def _chunk_rows_to_group_rows(x, perm):
    r = jnp.dot(perm, x.astype(BF16), preferred_element_type=F32)
    blk = _lane_block((S5_CHUNK, S5_W))
    out = []
    for g in range(S5_G):
        acc = None
        for t in range(S5_CHUNK):
            src = r[t * S5_CHUNK:(t + 1) * S5_CHUNK, :]
            shift = ((t - g) * S5_GSIZE) % S5_W
            src = pltpu.roll(src, shift, 1) if shift else src
            acc = src if acc is None else jnp.where(blk == t, src, acc)
        out.append(acc)
    return out


def _group_rows_to_chunk_rows(ys, perm_t):
    blk = _lane_block((S5_CHUNK, S5_W))
    rows = []
    for t in range(S5_CHUNK):
        acc = None
        for g in range(S5_G):
            shift = ((g - t) * S5_GSIZE) % S5_W
            src = pltpu.roll(ys[g], shift, 1) if shift else ys[g]
            acc = src if acc is None else jnp.where(blk == g, src, acc)
        rows.append(acc)
    z = jnp.concatenate(rows, 0)
    return jnp.dot(perm_t, z.astype(BF16), preferred_element_type=F32)


def _two_group_specs(block, n_first, lead=()):
    nl = len(lead)
    first = pl.BlockSpec(lead + block, lambda i, *_: (0,) * nl + (jnp.minimum(i, n_first - 1), 0))
    second = pl.BlockSpec(lead + block, lambda i, *_: (0,) * nl + (jnp.maximum(i - n_first, 0), 0))
    return [first, second]


def _inproj_kernel(xp_ref, xs_ref, ada_ref, npre_ref, w_ref, qn_ref, kvn_ref, perm_ref,
                   u_ref, cq_ref, ckv_ref, krab_ref, z_ref, xbc_ref, dt_ref, v_ref, *, n_first):
    cpt = TM // S5_CHUNK
    for s in range(TOKEN_SUB):
        rows = slice(s * TM, (s + 1) * TM)
        ada = ada_ref[s]
        x = jnp.where(pl.program_id(0) < n_first, xp_ref[rows, :], xs_ref[rows, :])
        h = _rms(x, npre_ref[...] * (1.0 + ada[1:2])) + ada[0:1]
        p = _bdot(h, w_ref[...])
        for g, ug in enumerate(_chunk_rows_to_group_rows(p[:, P_U:P_Q], perm_ref[...])):
            u_ref[g, s * cpt:(s + 1) * cpt, :] = ug.astype(BF16)
        cq_ref[rows, :] = _rms(p[:, P_Q:P_KV], qn_ref[...]).astype(BF16)
        ckv_ref[rows, :] = _rms(p[:, P_KV:P_KRA], kvn_ref[...])
        krab_ref[rows, :] = p[:, P_KRA:P_Z]
        z_ref[rows, :] = p[:, P_Z:P_XBC]
        xbc_ref[rows, :] = p[:, P_XBC:P_DT]
        dt_ref[rows, :] = p[:, P_DT:P_CA]
        v_ref[rows, :] = p[:, P_CA:P_CB] * jax.nn.sigmoid(p[:, P_CB:P_END])


def _chunk_perm():
    idx = np.arange(TM)
    src = (idx % S5_CHUNK) * S5_CHUNK + idx // S5_CHUNK
    return jnp.asarray(np.eye(TM, dtype=np.float32)[src], BF16)


def _inproj_call(xp, xs, ada_t, npre, w, layer, qn, kvn):
    t = xp.shape[0] + xs.shape[0]
    tm = TOKEN_SUB * TM
    n_first = xp.shape[0] // tm
    row = lambda n: pl.BlockSpec((tm, n), lambda i: (i, 0))
    full = lambda a: pl.BlockSpec(a.shape, lambda i: (0,) * a.ndim)
    widths = (256, 128, 256, 256, 512, 128, 256)
    dts = (BF16, F32, F32, F32, F32, F32, F32)
    perm = _chunk_perm()
    cpt = tm // S5_CHUNK
    return pl.pallas_call(
        functools.partial(_inproj_kernel, n_first=n_first), grid=(t // tm,),
        in_specs=_two_group_specs((tm, D_MODEL), n_first)
        + [pl.BlockSpec((TOKEN_SUB, 8, D_MODEL), lambda i: (i, 0, 0)), full(npre),
           pl.BlockSpec((None,) + w.shape[1:], lambda i: (layer, 0, 0)), full(qn), full(kvn), full(perm)],
        out_specs=[pl.BlockSpec((S5_G, cpt, S5_W), lambda i: (0, i, 0))] + [row(n) for n in widths],
        out_shape=[jax.ShapeDtypeStruct((S5_G, t // S5_CHUNK, S5_W), BF16)]
        + [jax.ShapeDtypeStruct((t, n), d) for n, d in zip(widths, dts)],
        compiler_params=_cparams("arbitrary"), name="inproj",
    )(xp, xs, ada_t, npre, w, qn, kvn, perm)


def _s5_kernel(u_ref, r_ref, vf_ref, vb_ref, a_ref, h0f_ref, h0fs_ref, h0b_ref, h0bs_ref, perm_ref, permt_ref,
               y_ref, stf_ref, stb_ref, ucb, sf, sfs, sb, sbs, *, nb, nc):
    gw = 2 * S5_P
    for g in range(S5_G):
        ucb[g] = jnp.dot(perm_ref[...], u_ref[g], preferred_element_type=F32).astype(BF16)
        r = jnp.dot(ucb[g], r_ref[g, :, S5_W:], preferred_element_type=F32)
        sl = slice(g * gw, (g + 1) * gw)
        sf[:, sl] = r[:, 0:128]
        sfs[:, sl] = r[:, 128:256]
        sb[:, sl] = r[:, 256:384]
        sbs[:, sl] = r[:, 384:512]
    arf, aif, aisf = a_ref[0:1], a_ref[1:2], a_ref[2:3]
    arb, aib, aisb = a_ref[3:4], a_ref[4:5], a_ref[5:6]
    cpt = 1 if nb % SUBLANE == 0 else SUBLANE // nb
    rows = cpt * nb
    nt = nc // cpt

    def body(i, carry):
        hf, hfs, hb, hbs = carry
        rf = pl.ds(pl.multiple_of(i * rows, rows), rows)
        rb = pl.ds(pl.multiple_of((nt - 1 - i) * rows, rows), rows)
        inc_f, inc_fs, inc_b, inc_bs = sf[rf, :], sfs[rf, :], sb[rb, :], sbs[rb, :]
        in_f, in_b = [None] * cpt, [None] * cpt
        for k in range(cpt):
            kf = slice(k * nb, (k + 1) * nb)
            kb = slice((cpt - 1 - k) * nb, (cpt - k) * nb)
            in_f[k] = hf
            in_b[cpt - 1 - k] = hb
            hf, hfs = arf * hf + aif * hfs + inc_f[kf], arf * hfs + aisf * hf + inc_fs[kf]
            hb, hbs = arb * hb + aib * hbs + inc_b[kb], arb * hbs + aisb * hb + inc_bs[kb]
        sf[rf, :] = in_f[0] if cpt == 1 else jnp.concatenate(in_f, 0)
        sb[rb, :] = in_b[0] if cpt == 1 else jnp.concatenate(in_b, 0)
        return hf, hfs, hb, hbs

    hf, _, hb, _ = lax.fori_loop(0, nt, body, (h0f_ref[...], h0fs_ref[...], h0b_ref[...], h0bs_ref[...]))
    stf_ref[...] = hf
    stb_ref[...] = hb
    for g in range(S5_G):
        sl = slice(g * gw, (g + 1) * gw)
        y = (jnp.dot(ucb[g], r_ref[g, :, :S5_W], preferred_element_type=F32)
             + _bdot(sf[:, sl], vf_ref[g]) + _bdot(sb[:, sl], vb_ref[g]))
        y_ref[g] = jnp.dot(permt_ref[...], y.astype(BF16), preferred_element_type=F32).astype(BF16)


def _s5_call(u, tabs, h0, nb, n, block):
    r, vf, vb, avec = tabs
    sw = S5_G * 2 * S5_P
    nc = n // nb
    dst = np.arange(n)
    src = (dst % nb) * nc + dst // nb
    perm = np.eye(n, dtype=np.float32)[src]
    consts = (jnp.asarray(perm, BF16), jnp.asarray(perm.T, BF16))
    full = lambda a: pl.BlockSpec(a.shape, lambda i: (0,) * len(a.shape))
    st = jax.ShapeDtypeStruct((nb, sw), F32)
    args = (r, vf, vb, avec, *h0, *consts)
    return pl.pallas_call(
        functools.partial(_s5_kernel, nb=nb, nc=nc), grid=(1,),
        in_specs=[pl.BlockSpec((S5_G, n, S5_W), lambda i: (0, block, 0))] + [full(a) for a in args],
        out_specs=[full(jax.ShapeDtypeStruct((S5_G, n, S5_W), BF16)), full(st), full(st)],
        out_shape=[jax.ShapeDtypeStruct((S5_G, n, S5_W), BF16), st, st],
        scratch_shapes=[pltpu.VMEM((S5_G, n, S5_W), BF16)] + [pltpu.VMEM((n, sw), F32) for _ in range(4)],
        compiler_params=_cparams("arbitrary"), name="s5",
    )(u, *args)


def _cpow(br, bi, e, nbits):
    pr = jnp.ones(e.shape, F32)
    pi = jnp.zeros(e.shape, F32)
    for k in range(nbits):
        bit = ((e >> k) & 1) == 1
        pr, pi = jnp.where(bit, pr * br - pi * bi, pr), jnp.where(bit, pr * bi + pi * br, pi)
        br, bi = br * br - bi * bi, 2.0 * br * bi
    return pr, pi


def _s5tab_kernel(prow_ref, pcol_ref, bt_ref, ct_ref, dsk_ref, etile_ref, r_ref, vf_ref, vb_ref, a_ref):
    q = S5_CHUNK
    hp = lax.Precision.HIGHEST

    def abar(lre, lim, ldt):
        dt = jnp.exp(ldt)
        mag = jnp.exp(lre * dt)
        return mag * jnp.cos(lim * dt), mag * jnp.sin(lim * dt)

    def per_step(x):
        return jnp.dot(x, etile_ref[...], precision=hp, preferred_element_type=F32)

    step_of_col = lax.broadcasted_iota(jnp.int32, (S5_P, S5_W), 1) // S5_GSIZE
    step_of_row = lax.broadcasted_iota(jnp.int32, (S5_W, S5_P), 0) // S5_GSIZE
    kcat, wtabs, arows = [], [], []
    for d in range(2):
        prow = prow_ref[d, 0]
        lre, lim = prow[0:1], prow[1:2]
        abr, abi = abar(lre, lim, prow[2:3])
        den = lre * lre + lim * lim
        fr = ((abr - 1.0) * lre + abi * lim) / den
        fi = (abi * lre - (abr - 1.0) * lim) / den
        btr, bti = bt_ref[d, 0, 0], bt_ref[d, 0, 1]
        bbr, bbi = fr * btr - fi * bti, fr * bti + fi * btr
        pcol = pcol_ref[d, 0]
        cbr, cbi = abar(pcol[:, 0:1], pcol[:, 1:2], pcol[:, 2:3])
        ctr, cti = per_step(ct_ref[d, 0, 0]), per_step(ct_ref[d, 0, 1])

        pr, pi = _cpow(cbr, cbi, step_of_col if d == 0 else q - 1 - step_of_col, 4)
        lr, li = ctr * pr - cti * pi, ctr * pi + cti * pr
        vr, vi = lr * cbr - li * cbi, lr * cbi + li * cbr
        (vf_ref if d == 0 else vb_ref)[0] = jnp.concatenate([vr, -vi], 0).astype(BF16)
        kcat.append(jnp.dot(bbr, lr, precision=hp, preferred_element_type=F32)
                    - jnp.dot(bbi, li, precision=hp, preferred_element_type=F32))
        pr, pi = _cpow(abr, abi, q - 1 - step_of_row if d == 0 else step_of_row, 4)
        tbr, tbi = jnp.concatenate([bbr] * q, 0), jnp.concatenate([bbi] * q, 0)
        wr, wi = pr * tbr - pi * tbi, pr * tbi + pi * tbr
        wtabs += [wr, wi, wi, wr]
        ar, ai = _cpow(abr, abi, jnp.full((1, S5_P), q, jnp.int32), 5)
        arows += [jnp.concatenate([ar, ar], 1), jnp.concatenate([-ai, ai], 1), jnp.concatenate([ai, -ai], 1)]
    lane = lax.broadcasted_iota(jnp.int32, (S5_GSIZE, S5_W), 1)
    rows = []
    for s in range(q):
        shr = S5_GSIZE * s
        shl = S5_GSIZE * (q - 1 - s)
        f = jnp.where(lane >= shr, pltpu.roll(kcat[0], shr, 1), 0.0) if shr else kcat[0]
        b = jnp.where(lane < S5_W - shl, pltpu.roll(kcat[1], S5_W - shl, 1), 0.0) if shl else kcat[1]
        rows.append(f + b)
    ri = lax.broadcasted_iota(jnp.int32, (S5_W, S5_W), 0)
    ci = lax.broadcasted_iota(jnp.int32, (S5_W, S5_W), 1)
    tt = jnp.concatenate(rows, 0) + jnp.where(ri == ci, dsk_ref[0], 0.0)
    r_ref[0] = jnp.concatenate([tt] + wtabs, 1).astype(BF16)
    a_ref[...] = jnp.concatenate(arows + [jnp.zeros((2, 2 * S5_P), F32)], 0)


def _s5_tables(lam_re, lam_im, log_dt, b_re, b_im, c_re, c_im, dskip):
    zeros = jnp.zeros((2, S5_G, 5, S5_P), F32)
    prow = jnp.concatenate([lam_re[:, :, None], lam_im[:, :, None],
                            jnp.broadcast_to(log_dt[:, :, None, None], (2, S5_G, 1, S5_P)), zeros], 2)
    pcol = jnp.swapaxes(prow, 2, 3)
    bt = jnp.stack([jnp.swapaxes(b_re, 2, 3), jnp.swapaxes(b_im, 2, 3)], 2)
    ct = jnp.stack([jnp.swapaxes(c_re, 2, 3), jnp.swapaxes(c_im, 2, 3)], 2)
    dsk = jnp.tile(dskip.reshape(S5_G, 1, S5_GSIZE), (1, 1, S5_CHUNK))
    etile = jnp.asarray(np.tile(np.eye(S5_GSIZE, dtype=np.float32), (1, S5_CHUNK)))
    gblk = lambda a: pl.BlockSpec((2, 1) + a.shape[2:], lambda g: (0, g) + (0,) * (a.ndim - 2))
    sw = S5_G * 2 * S5_P
    return pl.pallas_call(
        _s5tab_kernel, grid=(S5_G,),
        in_specs=[gblk(prow), gblk(pcol), gblk(bt), gblk(ct), pl.BlockSpec((1, 1, S5_W), lambda g: (g, 0, 0)),
                  pl.BlockSpec(etile.shape, lambda g: (0, 0))],
        out_specs=[pl.BlockSpec((1, S5_W, 768), lambda g: (g, 0, 0)), pl.BlockSpec((1, 2 * S5_P, S5_W), lambda g: (g, 0, 0)),
                   pl.BlockSpec((1, 2 * S5_P, S5_W), lambda g: (g, 0, 0)), pl.BlockSpec((8, 2 * S5_P), lambda g: (0, g))],
        out_shape=[jax.ShapeDtypeStruct((S5_G, S5_W, 768), BF16), jax.ShapeDtypeStruct((S5_G, 2 * S5_P, S5_W), BF16),
                   jax.ShapeDtypeStruct((S5_G, 2 * S5_P, S5_W), BF16), jax.ShapeDtypeStruct((8, sw), F32)],
        compiler_params=_cparams("arbitrary"), name="s5tab",
    )(prow, pcol, bt, ct, dsk, etile)


def _s5_state_in(state):
    b = state.shape[0]
    sw = jnp.transpose(state, (0, 1, 2, 4, 3))
    plain = sw.reshape(b, 2, S5_G * 2 * S5_P)
    swapped = sw[:, :, :, ::-1, :].reshape(b, 2, S5_G * 2 * S5_P)
    return plain[:, 0], swapped[:, 0], plain[:, 1], swapped[:, 1]


def _s5_state_out(stf, stb):
    b = stf.shape[0]
    st = jnp.stack([stf, stb], 1).reshape(b, 2, S5_G, 2, S5_P)
    return jnp.transpose(st, (0, 1, 2, 4, 3))


def _mla_kernel(*refs, seq, lctx):
    if lctx:
        (cq_ref, ckv_ref, krab_ref, cosq_ref, sinq_ref, cosk_ref, sink_ref, cckv_ref, ckr_ref,
         wqa_ref, wqb_ref, wk_ref, wv_ref, o_ref, k_scr, v_scr, s_scr) = refs
    else:
        (cq_ref, ckv_ref, krab_ref, cosq_ref, sinq_ref, cosk_ref, sink_ref,
         wqa_ref, wqb_ref, wk_ref, wv_ref, o_ref, k_scr, v_scr, s_scr) = refs
    lk = lctx + seq
    rb = min(seq, 512)

    @pl.when(pl.program_id(1) == 0)
    def _build_keys():
        if lctx:
            ckv = cckv_ref[0]
            kn = _bdot(ckv, wk_ref[...])
            vn = _bdot(ckv, wv_ref[...])
            for h in range(MLA_H):
                k_scr[h, 0:lctx, :] = (kn[:, h * LANE:(h + 1) * LANE] + ckr_ref[0]).astype(BF16)
                v_scr[h, 0:lctx, :] = vn[:, h * LANE:(h + 1) * LANE].astype(BF16)

        def chunk(i, _):
            r0 = pl.multiple_of(i * rb, rb)
            rows = pl.ds(r0, rb)
            ckv = ckv_ref[rows, :]
            kn = _bdot(ckv, wk_ref[...])
            vn = _bdot(ckv, wv_ref[...])
            krab = krab_ref[rows, :]
            krx = krab[:, :LANE] * cosk_ref[rows, :] + krab[:, LANE:] * sink_ref[rows, :]
            dst = pl.ds(pl.multiple_of(lctx + r0, rb if lctx % rb == 0 else 256), rb)
            for h in range(MLA_H):
                k_scr[h, dst, :] = (kn[:, h * LANE:(h + 1) * LANE] + krx).astype(BF16)
                v_scr[h, dst, :] = vn[:, h * LANE:(h + 1) * LANE].astype(BF16)
            return 0

        lax.fori_loop(0, seq // rb, chunk, 0)

    cq = cq_ref[...]
    qa = _bdot(cq, wqa_ref[...])
    qb = _bdot(cq, wqb_ref[...])
    cosq, sinq = cosq_ref[...], sinq_ref[...]
    scale = 1.0 / math.sqrt(MLA_NOPE + MLA_ROPE)
    outs = []
    chunks = [slice(j * MLA_KC, (j + 1) * MLA_KC) for j in range(lk // MLA_KC)]
    for h in range(MLA_H):
        hs = slice(h * LANE, (h + 1) * LANE)
        qh = ((qa[:, hs] * cosq + qb[:, hs] * sinq) * scale).astype(BF16)
        mp = None
        for ks in chunks:
            s = _bdot_nt(qh, k_scr[h, ks, :])
            s_scr[:, ks] = s
            for t in range(MLA_KC // LANE):
                part = s[:, t * LANE:(t + 1) * LANE]
                mp = part if mp is None else jnp.maximum(mp, part)
        m = jnp.max(mp, axis=-1, keepdims=True)
        lp = jnp.zeros((MLA_TQ, LANE), F32)
        acc = jnp.zeros((MLA_TQ, LANE), F32)
        for ks in chunks:
            e = jnp.exp(s_scr[:, ks] - m)
            for t in range(MLA_KC // LANE):
                lp = lp + e[:, t * LANE:(t + 1) * LANE]
            acc = acc + _bdot(e, v_scr[h, ks, :])
        outs.append((acc / jnp.sum(lp, axis=-1, keepdims=True))[:, :MLA_V])
    o_ref[...] = jnp.concatenate(outs, axis=-1)


def _mla_call(cq, ckv, krab, rope, ctx, w, nb, seq, row0):
    lctx = 0 if ctx is None else ctx[0].shape[1]
    nq = seq // MLA_TQ
    q0, s0 = row0 // MLA_TQ, row0 // seq
    cosx, sinx = rope
    qrow = lambda n: pl.BlockSpec((MLA_TQ, n), lambda b, i: (q0 + b * nq + i, 0))
    srow = lambda n: pl.BlockSpec((seq, n), lambda b, i: (s0 + b, 0))
    full = lambda a: pl.BlockSpec(a.shape, lambda b, i: (0,) * a.ndim)
    in_specs = [qrow(256), srow(128), srow(256), pl.BlockSpec((MLA_TQ, LANE), lambda b, i: (i, 0)),
                pl.BlockSpec((MLA_TQ, LANE), lambda b, i: (i, 0)), full(cosx), full(sinx)]
    args = [cq, ckv, krab, cosx, sinx, cosx, sinx]
    if lctx:
        in_specs += [pl.BlockSpec((1, lctx, LANE), lambda b, i: (b, 0, 0))] * 2
        args += list(ctx)
    in_specs += [full(a) for a in w]
    args += list(w)
    lk = lctx + seq
    return pl.pallas_call(
        functools.partial(_mla_kernel, seq=seq, lctx=lctx), grid=(nb, nq),
        in_specs=in_specs,
        out_specs=pl.BlockSpec((MLA_TQ, 256), lambda b, i: (b * nq + i, 0)),
        out_shape=jax.ShapeDtypeStruct((nb * seq, 256), F32),
        scratch_shapes=[pltpu.VMEM((MLA_H, lk, LANE), BF16), pltpu.VMEM((MLA_H, lk, LANE), BF16),
                        pltpu.VMEM((MLA_TQ, lk), F32)],
        compiler_params=_cparams("arbitrary", "arbitrary"), name="mla",
    )(*args)


def _swap8(w):
    q = MLA_ROPE // 4
    return jnp.concatenate([w[..., j * q:(j + 1) * q] for j in (1, 0, 3, 2)], -1)


def _mla_weights(wuq, wuk, wuv):
    z32 = jnp.zeros((MLA_QLORA, 32), F32)
    z64 = jnp.zeros((MLA_QLORA, 64), F32)
    k64 = jnp.zeros((MLA_KVLORA, 64), F32)
    qa, qb, wk, wv = [], [], [], []
    for h in range(MLA_H):
        rope_w = wuq[:, h, MLA_NOPE:]
        qa += [wuq[:, h, :MLA_NOPE], rope_w, z32]
        qb += [z64, _swap8(rope_w), z32]
        wk += [wuk[:, h], k64]
        wv += [wuv[:, h], k64]
    cat = lambda xs: jnp.concatenate(xs, -1).astype(BF16)
    return cat(qa), cat(qb), cat(wk), cat(wv)


def _rope_tables(seq, rotate):
    ones = np.ones((seq, MLA_NOPE))
    z32 = np.zeros((seq, 32))
    if not rotate:
        cosx, sinx = np.concatenate([ones, np.ones((seq, MLA_ROPE)), z32], -1), np.zeros((seq, LANE))
    else:
        pos = np.arange(seq)
        row = (pos // GRID_W).astype(np.float64)
        col = (pos % GRID_W).astype(np.float64)
        half = MLA_ROPE // 2
        inv = ROPE_BASE ** (-np.arange(0, half, 2, dtype=np.float64) / half)
        ang = np.concatenate([row[:, None] * inv, col[:, None] * inv], -1)
        cos, sin = np.cos(ang), np.sin(ang)
        q = MLA_ROPE // 4
        cos32 = np.concatenate([cos[:, :q], cos[:, :q], cos[:, q:], cos[:, q:]], -1)
        sin32 = np.concatenate([-sin[:, :q], sin[:, :q], -sin[:, q:], sin[:, q:]], -1)
        cosx, sinx = np.concatenate([ones, cos32, z32], -1), np.concatenate([0.0 * ones, sin32, z32], -1)
    return jnp.asarray(cosx, F32), jnp.asarray(sinx, F32)


def _cumsum_rows(x, reverse):
    n = x.shape[0]
    row = lax.broadcasted_iota(jnp.int32, x.shape, 0)
    s = 1
    while s < n:
        if reverse:
            x = x + jnp.where(row < n - s, pltpu.roll(x, n - s, 0), 0.0)
        else:
            x = x + jnp.where(row >= s, pltpu.roll(x, s, 0), 0.0)
        s *= 2
    return x


def _ssd_kernel(xbc_ref, z_ref, dt_ref, cw_ref, cb_ref, par_ref, dsk_ref, nw_ref, h0_ref,
                y_ref, st_ref, xc_scr, xt_scr, y_scr, hf_scr, hb_scr, *, seq, ns):
    q = SSD_CHUNK
    nc = seq // q
    hp = SSD_H * SSD_P
    row = lax.broadcasted_iota(jnp.int32, (q, 1), 0)

    def conv(c, _):
        t0 = pl.multiple_of(c * q, q)
        cur = xbc_ref[pl.ds(t0, q), :]
        prev8 = xbc_ref[pl.ds(pl.multiple_of(jnp.maximum(t0 - 8, 0), 8), 8), :]
        next8 = xbc_ref[pl.ds(pl.multiple_of(jnp.minimum(t0 + q, ns * seq - 8), 8), 8), :]
        c_in = lax.rem(c, nc)
        prow = jnp.where(c_in > 0, prev8[7:8], 0.0)
        nrow = jnp.where(c_in < nc - 1, next8[0:1], 0.0)
        up = jnp.where(row == 0, prow, pltpu.roll(cur, 1, 0))
        dn = jnp.where(row == q - 1, nrow, pltpu.roll(cur, q - 1, 0))
        acc = up * cw_ref[0:1] + cur * cw_ref[1:2] + dn * cw_ref[2:3] + cb_ref[...]
        xc = jax.nn.silu(acc)
        xc_scr[pl.ds(t0, q), :] = xc
        xt_scr[c] = jnp.transpose(xc[:, :SSD_W])
        y_scr[pl.ds(t0, q), :] = jnp.zeros((q, SSD_W), F32)
        return 0

    lax.fori_loop(0, ns * nc, conv, 0)

    pair_rows = (SSD_H // SSD_G) * SSD_P
    for d, h_scr in enumerate((hf_scr, hb_scr)):
        h_scr[...] = jnp.zeros_like(h_scr)
        for s in range(ns):
            for g in range(SSD_G):
                h_scr[s * hp + g * pair_rows:s * hp + (g + 1) * pair_rows, g * SSD_N:(g + 1) * SSD_N] = (
                    h0_ref[s, d, g * pair_rows:(g + 1) * pair_rows, :])
    ii = lax.broadcasted_iota(jnp.int32, (q, q), 0)
    jj = lax.broadcasted_iota(jnp.int32, (q, q), 1)
    first_half = lax.broadcasted_iota(jnp.int32, (q, LANE), 1) < SSD_N

    def one_chunk(c, d, h_scr, base):
        t0 = pl.multiple_of(c * q, q)
        xc = xc_scr[pl.ds(t0, q), :]
        xs_t = xt_scr[c]
        dt = jax.nn.softplus(dt_ref[pl.ds(t0, q), :] + par_ref[d:d + 1])
        cs = _cumsum_rows(dt * par_ref[2 + d:3 + d], reverse=(d == 1))
        edge = cs[q - 1:q] if d == 0 else cs[0:1]
        cs_t = jnp.transpose(cs)
        dt_t = jnp.transpose(dt)
        wts_t = jnp.transpose(jnp.exp(edge - cs) * dt)
        decay = jnp.exp(edge)
        mask = (ii >= jj) if d == 0 else (ii <= jj)
        bm_pair = xc[:, SSD_W:SSD_W + LANE]
        cm_pair = xc[:, SSD_W + LANE:SSD_W + 2 * LANE]
        for g in range(SSD_G):
            in_g = first_half if g == 0 else jnp.logical_not(first_half)
            pair = slice(g * LANE, (g + 1) * LANE)
            pair_state = slice(base + g * pair_rows, base + (g + 1) * pair_rows)
            heads = [g * (SSD_H // SSD_G) + hh for hh in range(SSD_H // SSD_G)]
            cm = jnp.where(in_g, cm_pair, 0.0)
            bm = jnp.where(in_g, bm_pair, 0.0)
            cbm = _bdot_nt(cm, bm_pair)
            cs_i = [jnp.transpose(jnp.broadcast_to(cs_t[h:h + 1, :], (q, q))) for h in heads]
            y_off = _bdot_nt(cm, h_scr[pair_state, :]) * jnp.exp(jnp.where(first_half, cs_i[0], cs_i[1]))
            ys = []
            for hh, h in enumerate(heads):
                lmat = jnp.exp(jnp.where(mask, cs_i[hh] - cs_t[h:h + 1, :], -jnp.inf))
                ys.append(_bdot(cbm * lmat * dt_t[h:h + 1, :], xc[:, pair]))
                ps = slice(h * SSD_P, (h + 1) * SSD_P)
                st = _bdot(xs_t[ps, :] * wts_t[h:h + 1, :], bm)
                hs = slice(base + h * SSD_P, base + (h + 1) * SSD_P)
                h_scr[hs, :] = decay[:, h:h + 1] * h_scr[hs, :] + st
            y_scr[pl.ds(t0, q), pair] += jnp.where(first_half, ys[0], ys[1]) + y_off

    def chunks(c, _):
        for s in range(ns):
            one_chunk(s * nc + c, 0, hf_scr, s * hp)
            one_chunk(s * nc + nc - 1 - c, 1, hb_scr, s * hp)
        return 0

    lax.fori_loop(0, nc, chunks, 0, unroll=min(nc, max(1, 4 // ns)))
    for d, h_scr in enumerate((hf_scr, hb_scr)):
        for s in range(ns):
            for g in range(SSD_G):
                st_ref[s, d, g * pair_rows:(g + 1) * pair_rows, :] = (
                    h_scr[s * hp + g * pair_rows:s * hp + (g + 1) * pair_rows, g * SSD_N:(g + 1) * SSD_N])

    def finish(c, _):
        rows = pl.ds(pl.multiple_of(c * q, q), q)
        y = y_scr[rows, :] + dsk_ref[...] * xc_scr[rows, 0:SSD_W]
        y_ref[rows, :] = _rms(y * jax.nn.silu(z_ref[rows, :]), nw_ref[...])
        return 0

    lax.fori_loop(0, ns * nc, finish, 0)


def _ssd_call(xbc, z, dt, w, h0, nb, seq, row0):
    ns = max(1, min(nb, SSD_STEP_ROWS // seq))
    rows = ns * seq
    s0 = row0 // rows
    srow = lambda n: pl.BlockSpec((rows, n), lambda b: (s0 + b, 0))
    full = lambda a: pl.BlockSpec(a.shape, lambda b: (0,) * a.ndim)
    hp = SSD_H * SSD_P
    st_spec = pl.BlockSpec((ns, 2, hp, SSD_N), lambda b: (b, 0, 0, 0))
    return pl.pallas_call(
        functools.partial(_ssd_kernel, seq=seq, ns=ns), grid=(nb // ns,),
        in_specs=[srow(SSD_XBC), srow(SSD_W), srow(LANE)] + [full(a) for a in w] + [st_spec],
        out_specs=[pl.BlockSpec((rows, SSD_W), lambda b: (b, 0)), st_spec],
        out_shape=[jax.ShapeDtypeStruct((nb * seq, SSD_W), F32), jax.ShapeDtypeStruct((nb, 2, hp, SSD_N), F32)],
        scratch_shapes=[pltpu.VMEM((rows, SSD_XBC), F32), pltpu.VMEM((rows // SSD_CHUNK, SSD_W, SSD_CHUNK), F32),
                        pltpu.VMEM((rows, SSD_W), F32), pltpu.VMEM((ns * hp, LANE), F32),
                        pltpu.VMEM((ns * hp, LANE), F32)],
        compiler_params=_cparams("arbitrary"), name="ssd",
    )(xbc, z, dt, *w, h0)


def _ssd_weights(conv_w, conv_b, dt_bias, a_log, dskip, norm_w):
    pad = lambda x: jnp.pad(x, ((0, 0), (0, LANE - SSD_H)))
    par = jnp.concatenate([pad(dt_bias), pad(-jnp.exp(a_log)), jnp.zeros((4, LANE), F32)], 0)
    cw = jnp.concatenate([conv_w, jnp.zeros((5, SSD_XBC), F32)], 0)
    return (cw, conv_b.reshape(1, SSD_XBC), par, jnp.repeat(dskip, SSD_P).reshape(1, SSD_W),
            norm_w.reshape(1, SSD_W))


def _conf_kernel(v_ref, w_ref, b_ref, g_ref, bt_ref, o_ref, pad_scr, sh_scr, *, seq, seg):
    q = CONF_Q
    halo = 2 * CONF_PAD
    tail = pad_scr.shape[0] - seq - CONF_PAD
    pad_scr[0:CONF_PAD, :] = jnp.zeros((CONF_PAD, CONF_W), F32)
    pad_scr[seq + CONF_PAD:, :] = jnp.zeros((tail, CONF_W), F32)
    pad_scr[CONF_PAD:seq + CONF_PAD, :] = v_ref[...]
    lo = CONF_PAD - CONF_K // 2
    pieces = [(j, min(LANE, seg + halo - j)) for j in range(0, seg + halo, LANE)]

    def segment(s, _):
        s0 = pl.multiple_of(s * seg, seg)
        for j0, n in pieces:
            big = pad_scr[pl.ds(s0 + j0, n + SUBLANE), :]
            for r in range(SUBLANE):
                sh_scr[r, j0:j0 + n, :] = big[r:r + n]

        def chunk(c, _):
            t0 = pl.multiple_of(c * q, q)
            acc = jnp.zeros((q, CONF_W), F32) + b_ref[...]
            for k in range(CONF_K):
                r, off = (lo + k) % SUBLANE, (lo + k) // SUBLANE * SUBLANE
                wk = jnp.concatenate([w_ref[k * SUBLANE:(k + 1) * SUBLANE, :]] * (q // SUBLANE), 0)
                acc = acc + sh_scr[r, pl.ds(pl.multiple_of(t0 + off, SUBLANE), q), :] * wk
            mu = jnp.mean(acc, axis=-1, keepdims=True)
            xc = acc - mu
            var = jnp.mean(xc * xc, axis=-1, keepdims=True)
            o_ref[pl.ds(s0 + t0, q), :] = jax.nn.silu(xc * lax.rsqrt(var + EPS) * g_ref[...] + bt_ref[...])
            return 0

        lax.fori_loop(0, seg // q, chunk, 0, unroll=2)
        return 0

    lax.fori_loop(0, seq // seg, segment, 0)


def _conf_call(v, w, nb, seq, row0):
    s0 = row0 // seq
    full = lambda a: pl.BlockSpec(a.shape, lambda b: (0,) * a.ndim)
    seg = min(seq, CONF_SEG)
    halo = 2 * CONF_PAD
    return pl.pallas_call(
        functools.partial(_conf_kernel, seq=seq, seg=seg), grid=(nb,),
        in_specs=[pl.BlockSpec((seq, CONF_W), lambda b: (s0 + b, 0))] + [full(a) for a in w],
        out_specs=pl.BlockSpec((seq, CONF_W), lambda b: (b, 0)),
        out_shape=jax.ShapeDtypeStruct((nb * seq, CONF_W), F32),
        scratch_shapes=[pltpu.VMEM((seq + halo + 2 * SUBLANE, CONF_W), F32),
                        pltpu.VMEM((SUBLANE, seg + halo, CONF_W), F32)],
        compiler_params=_cparams("arbitrary"), name="conf",
    )(v, *w)


def _split_bf16(x):
    hi = x.astype(BF16)
    return hi, (x - hi.astype(F32)).astype(BF16)


def _outproj_kernel(xp_ref, xs_ref, ada_ref, s5p_ref, s5s_ref, mlap_ref, mlas_ref, ssdp_ref, ssds_ref, confp_ref,
                    confs_ref, wglu_ref, bglu_ref, wo_ref, npost_ref, npre_ref, wrh_ref, wrl_ref, br_ref, perm_ref,
                    x1_ref, h2_ref, gate_ref, *, n_first):
    for s in range(TOKEN_SUB):
        _outproj_tile(s, xp_ref, xs_ref, ada_ref, s5p_ref, s5s_ref, mlap_ref, mlas_ref, ssdp_ref, ssds_ref, confp_ref,
                      confs_ref, wglu_ref, bglu_ref, wo_ref, npost_ref, npre_ref, wrh_ref, wrl_ref, br_ref, perm_ref,
                      x1_ref, h2_ref, gate_ref, n_first)


def _outproj_tile(s, xp_ref, xs_ref, ada_ref, s5p_ref, s5s_ref, mlap_ref, mlas_ref, ssdp_ref, ssds_ref, confp_ref,
                  confs_ref, wglu_ref, bglu_ref, wo_ref, npost_ref, npre_ref, wrh_ref, wrl_ref, br_ref, perm_ref,
                  x1_ref, h2_ref, gate_ref, n_first):
    rows = slice(s * TM, (s + 1) * TM)
    cpt = TM // S5_CHUNK
    is_first = pl.program_id(0) < n_first
    pick = lambda first, second: jnp.where(is_first, first[rows, :], second[rows, :])
    ada = ada_ref[s]
    ys5 = jnp.where(is_first, s5p_ref[:, s * cpt:(s + 1) * cpt, :], s5s_ref[:, s * cpt:(s + 1) * cpt, :])
    a = jax.nn.gelu(_group_rows_to_chunk_rows([ys5[g].astype(F32) for g in range(S5_G)], perm_ref[...]))
    s5 = a * jax.nn.sigmoid(_bdot(a, wglu_ref[...]) + bglu_ref[...])
    mix = (_bdot(s5, wo_ref[0:256, :]) + _bdot(pick(mlap_ref, mlas_ref), wo_ref[256:512, :])
           + _bdot(pick(ssdp_ref, ssds_ref), wo_ref[512:768, :]) + _bdot(pick(confp_ref, confs_ref), wo_ref[768:1024, :]))
    x1 = pick(xp_ref, xs_ref) + _rms(mix, npost_ref[...] * ada[2:3])
    x1_ref[rows, :] = x1
    h2 = _rms(x1, npre_ref[...] * (1.0 + ada[4:5])) + ada[3:4]
    h2_ref[rows, :] = h2.astype(BF16)
    hi, lo = _split_bf16(h2)
    lg = (jnp.dot(hi, wrh_ref[...], preferred_element_type=F32) + jnp.dot(lo, wrh_ref[...], preferred_element_type=F32)
          + jnp.dot(hi, wrl_ref[...], preferred_element_type=F32) + br_ref[...])
    lane = lax.broadcasted_iota(jnp.int32, lg.shape, 1)
    neg = -jnp.inf
    big = jnp.int32(1 << 20)
    first = lambda hit: jnp.min(jnp.where(hit, lane, big), axis=-1, keepdims=True)
    glm = jnp.where((lane >= MOE_E) & (lane < MOE_E + MOE_GROUPS), lg, neg)
    gmax = jnp.max(glm, axis=-1, keepdims=True)
    p_group = 1.0 / jnp.sum(jnp.exp(glm - gmax), axis=-1, keepdims=True)
    gsel = first(glm == gmax) - MOE_E
    elm = jnp.where((lane < MOE_E) & ((lane // MOE_PER_GROUP) == gsel), lg, neg)
    v1 = jnp.max(elm, axis=-1, keepdims=True)
    i1 = first(elm == v1)
    elm2 = jnp.where(lane == i1, neg, elm)
    v2 = jnp.max(elm2, axis=-1, keepdims=True)
    i2 = first(elm2 == v2)
    e2 = jnp.exp(v2 - v1)
    w1 = p_group / (1.0 + e2)
    w2 = p_group * e2 / (1.0 + e2)
    gate_ref[rows, :] = jnp.where(lane == i1, w1, 0.0) + jnp.where(lane == i2, w2, 0.0)


def _outproj_call(x, ada_t, ys5, ymla, yssd, yconf, w):
    t = x[0].shape[0] + x[1].shape[0]
    tm = TOKEN_SUB * TM
    n_first = x[0].shape[0] // tm
    row = lambda n: pl.BlockSpec((tm, n), lambda i: (i, 0))
    full = lambda a: pl.BlockSpec(a.shape, lambda i: (0,) * a.ndim)
    w = tuple(w) + (_chunk_perm(),)
    pair = lambda n: _two_group_specs((tm, n), n_first)
    return pl.pallas_call(
        functools.partial(_outproj_kernel, n_first=n_first), grid=(t // tm,),
        in_specs=pair(D_MODEL) + [pl.BlockSpec((TOKEN_SUB, 8, D_MODEL), lambda i: (i, 0, 0))]
        + _two_group_specs((tm // S5_CHUNK, S5_W), n_first, lead=(S5_G,)) + pair(256) + pair(256) + pair(256)
        + [full(a) for a in w],
        out_specs=[row(D_MODEL), row(D_MODEL), row(LANE)],
        out_shape=[jax.ShapeDtypeStruct((t, D_MODEL), F32), jax.ShapeDtypeStruct((t, D_MODEL), BF16),
                   jax.ShapeDtypeStruct((t, LANE), F32)],
        compiler_params=_cparams("arbitrary"), name="outproj",
    )(*x, ada_t, *ys5, *ymla, *yssd, *yconf, *w)


def _moe_kernel(h_ref, gate_ref, x1_ref, ada_ref, wg_ref, wu_ref, wd_ref, npost_ref, op_ref, os_ref, hid_scr,
                res_scr, *, n_first):
    i = pl.program_id(0)
    sub = MOE_TM // MOE_SUB
    for s in range(MOE_SUB):
        rows = slice(s * sub, (s + 1) * sub)
        h = h_ref[rows, :]
        gate = gate_ref[rows, :]
        lane = lax.broadcasted_iota(jnp.int32, gate.shape, 1)
        for e in range(MOE_E):
            ge = jnp.sum(jnp.where(lane == e, gate, 0.0), axis=-1, keepdims=True)
            hid = (jax.nn.silu(jnp.dot(h, wg_ref[e], preferred_element_type=F32))
                   * jnp.dot(h, wu_ref[e], preferred_element_type=F32) * ge)
            hid_scr[rows, e * MOE_HID:(e + 1) * MOE_HID] = hid.astype(BF16)
        out = jnp.dot(hid_scr[rows, :], wd_ref[...], preferred_element_type=F32)
        res_scr[rows, :] = x1_ref[rows, :] + _rms(out, npost_ref[...] * ada_ref[0][5:6])

    @pl.when(i < n_first)
    def _store_first():
        op_ref[...] = res_scr[...]

    @pl.when(i >= n_first)
    def _store_second():
        os_ref[...] = res_scr[...]


def _moe_call(h2, gate, x1, ada_t, wg, wu, wd, layer, npost, t_first):
    t = h2.shape[0]
    tm = MOE_TM
    n_first = t_first // tm
    row = lambda n: pl.BlockSpec((tm, n), lambda i: (i, 0))
    resident = lambda a: pl.BlockSpec((None,) + a.shape[1:], lambda i: (layer,) + (0,) * (a.ndim - 1),
                                      pipeline_mode=pl.Buffered(1))
    wd = wd.reshape(wd.shape[0], MOE_E * MOE_HID, D_MODEL)
    return pl.pallas_call(
        functools.partial(_moe_kernel, n_first=n_first), grid=(t // tm,),
        in_specs=[row(D_MODEL), row(LANE), row(D_MODEL),
                  pl.BlockSpec((1, 8, D_MODEL), lambda i: (i * (tm // TM), 0, 0)),
                  resident(wg), resident(wu), resident(wd), pl.BlockSpec(npost.shape, lambda i: (0, 0))],
        out_specs=_two_group_specs((tm, D_MODEL), n_first),
        out_shape=[jax.ShapeDtypeStruct((t_first, D_MODEL), F32), jax.ShapeDtypeStruct((t - t_first, D_MODEL), F32)],
        scratch_shapes=[pltpu.VMEM((tm, MOE_E * MOE_HID), BF16), pltpu.VMEM((tm, D_MODEL), F32)],
        compiler_params=_cparams("arbitrary"), name="moe",
    )(h2, gate, x1, ada_t, wg, wu, wd, npost)


def _pack_w_in(w):
    lead = w.shape[:-1]
    z32, z64 = jnp.zeros(lead + (32,), F32), jnp.zeros(lead + (64,), F32)
    kr = w[..., OFF_MLA_KR:OFF_SSD_Z]
    zdt = jnp.zeros(lead + (LANE - SSD_H,), F32)
    cols = [w[..., OFF_S5:OFF_MLA_KR], z64, kr, z32, z64, _swap8(kr), z32,
            w[..., OFF_SSD_Z:OFF_SSD_DT], w[..., OFF_SSD_DT:OFF_CONF], zdt, w[..., OFF_CONF:IN_COLS]]
    return jnp.concatenate(cols, -1).astype(BF16)


def _row(v):
    return v.reshape(1, -1)


def kernel(x_prompt, x_sample, cache_mla_ckv, cache_mla_krope, state_s5, state_ssd, c, c_ctx, w_ada, b_ada, norm_pre1, norm_post1, norm_pre2, norm_post2, w_in, w_out, s5_lam_re, s5_lam_im, s5_log_dt, s5_b_re, s5_b_im, s5_c_re, s5_c_im, s5_d, s5_w_glu, s5_b_glu, mla_qnorm, mla_kvnorm, mla_wuq, mla_wuk, mla_wuv, ssd_conv_w, ssd_conv_b, ssd_dt_bias, ssd_a_log, ssd_d, ssd_norm_w, conf_dw_w, conf_dw_b, conf_ln_g, conf_ln_b, moe_wg, moe_bg, moe_we, moe_be, moe_w_gate, moe_w_up, moe_w_down):
    bp, lp, d = x_prompt.shape
    bs, ls, _ = x_sample.shape
    tp, ts = bp * lp, bs * ls
    x = (x_prompt.reshape(tp, d), x_sample.reshape(ts, d))

    crows = jnp.concatenate([c_ctx[None], c, jnp.zeros((8 - 1 - bs, d), F32)], 0)
    ada = _ada_call(crows, w_ada, b_ada)
    tile_row = np.concatenate([np.zeros(tp // TM, np.int32), 1 + np.repeat(np.arange(bs, dtype=np.int32), ls // TM)])

    rope_p = _rope_tables(lp, rotate=False)
    rope_s = _rope_tables(ls, rotate=True)
    zeros_s5 = tuple(jnp.zeros((bp, S5_G * 2 * S5_P), F32) for _ in range(4))
    zeros_ssd = jnp.zeros((bp, 2, SSD_H * SSD_P, SSD_N), F32)
    pad_kr = lambda kr: jnp.pad(kr, ((0, 0), (0, 0), (MLA_NOPE, LANE - MLA_NOPE - MLA_ROPE)))

    w_in_packed = _pack_w_in(w_in)
    moe_w = [w.astype(BF16) for w in (moe_w_gate, moe_w_up, moe_w_down)]

    ckv_l, kr_l, s5_l, ssd_l = [], [], [], []
    for l in range(DEPTH):
        ada_t = jnp.pad(ada[l][tile_row].reshape(-1, 6, d), ((0, 0), (0, 2), (0, 0)))
        u, cq, ckv, krab, z, xbc, dt, v = _inproj_call(
            *x, ada_t, _row(norm_pre1[l]), w_in_packed, l, _row(mla_qnorm[l]), _row(mla_kvnorm[l]))

        tabs = _s5_tables(s5_lam_re[l], s5_lam_im[l], s5_log_dt[l], s5_b_re[l], s5_b_im[l], s5_c_re[l], s5_c_im[l],
                          s5_d[l])
        ncp = tp // S5_CHUNK
        ncs = ts // S5_CHUNK
        yp, stf, stb = _s5_call(u, tabs, zeros_s5, bp, ncp, 0)
        ys, _, _ = _s5_call(u, tabs, _s5_state_in(state_s5[:, l]), bs, ncs, ncp // ncs)
        y_s5 = (yp, ys)
        s5_l.append(_s5_state_out(stf, stb))

        mw = _mla_weights(mla_wuq[l], mla_wuk[l], mla_wuv[l])
        y_mla = (_mla_call(cq, ckv, krab, rope_p, None, mw, bp, lp, 0),
                 _mla_call(cq, ckv, krab, rope_s, (cache_mla_ckv[:, l], pad_kr(cache_mla_krope[:, l])), mw, bs, ls, tp))
        ckv_l.append(ckv[:tp].reshape(bp, lp, MLA_KVLORA))
        kr_l.append(krab[:tp, MLA_NOPE:MLA_NOPE + MLA_ROPE].reshape(bp, lp, MLA_ROPE))

        sw = _ssd_weights(ssd_conv_w[l], ssd_conv_b[l], ssd_dt_bias[l], ssd_a_log[l], ssd_d[l], ssd_norm_w[l])
        yssd_p, st_ssd = _ssd_call(xbc, z, dt, sw, zeros_ssd, bp, lp, 0)
        yssd_s, _ = _ssd_call(xbc, z, dt, sw, state_ssd[:, l].reshape(bs, 2, SSD_H * SSD_P, SSD_N), bs, ls, tp)
        ssd_l.append(st_ssd.reshape(bp, 2, SSD_H, SSD_P, SSD_N))

        cw = (jnp.repeat(conf_dw_w[l], SUBLANE, axis=0), _row(conf_dw_b[l]),
              _row(conf_ln_g[l]), _row(conf_ln_b[l]))
        y_conf = (_conf_call(v, cw, bp, lp, 0), _conf_call(v, cw, bs, ls, tp))

        wr = jnp.pad(jnp.concatenate([moe_we[l], moe_wg[l]], -1), ((0, 0), (0, LANE - MOE_E - MOE_GROUPS)))
        wrh, wrl = _split_bf16(wr)
        br = jnp.pad(jnp.concatenate([moe_be[l], moe_bg[l]]), (0, LANE - MOE_E - MOE_GROUPS)).reshape(1, LANE)
        ow = (s5_w_glu[l].astype(BF16), _row(s5_b_glu[l]), w_out[l].astype(BF16), _row(norm_post1[l]),
              _row(norm_pre2[l]), wrh, wrl, br)
        x1, h2, gate = _outproj_call(x, ada_t, y_s5, y_mla, (yssd_p, yssd_s), y_conf, ow)

        x = _moe_call(h2, gate, x1, ada_t, *moe_w, l, _row(norm_post2[l]), tp)

    return (x[0].reshape(bp, lp, d), x[1].reshape(bs, ls, d),
            jnp.stack(ckv_l, 1), jnp.stack(kr_l, 1), jnp.stack(s5_l, 1), jnp.stack(ssd_l, 1))
```
